```python
import math
import jax
import jax.numpy as jnp
from jax import lax
import numpy as np

D_MODEL = 4096
BATCH = 8
SEQ = 2048
DEPTH = 2

GRID_W = 64
CTX_LEN = 256
N_BRANCH = 4
BRANCH_W = D_MODEL // N_BRANCH
HEAD_DIM = 128
N_HEADS = BRANCH_W // HEAD_DIM
DIFF_DIM = HEAD_DIM // 2
HALF_FF = (5 * D_MODEL) // 4
MOD_CHUNKS = 9
ROPE_THETA = 10000.0
Q_BLOCK = 128
CHUNK = 64
CONV_W = 3
HY_EMB = 33
HY_ORDER = 64
HY_FAST_DECAY = 0.3
HY_SLOW_DECAY = 1.5
HY_TARGET = 1e-2
EPS = 1e-6
MASK_NEG = -1e30
LB_FLOOR = 1e-20
IN_SPLITS = (
    BRANCH_W, BRANCH_W, BRANCH_W,
    BRANCH_W, BRANCH_W, BRANCH_W, BRANCH_W, 2 * N_HEADS, 2 * N_HEADS,
    BRANCH_W, 2 * BRANCH_W, BRANCH_W, BRANCH_W,
    BRANCH_W, BRANCH_W, BRANCH_W,
)
IN_COLS = 15 * BRANCH_W + 4 * N_HEADS

kernel_name = 'hybrid_diffattn_gdn_hgrn2_hyena_prefix'


def rms_norm(x, g):
    xf = x.astype(jnp.float32)
    y = xf * lax.rsqrt(jnp.mean(xf * xf, axis=-1, keepdims=True) + EPS)
    return (y * g.astype(jnp.float32)).astype(x.dtype)


def _l2norm(x):
    return x * lax.rsqrt(jnp.sum(x * x, axis=-1, keepdims=True) + EPS)


def _heads(t):
    B, L, W = t.shape
    return jnp.moveaxis(t.astype(jnp.float32).reshape(B, L, W // HEAD_DIM, HEAD_DIM), 1, 2)


def _merge_heads(o):
    B, H, L, hd = o.shape
    return jnp.moveaxis(o, 1, 2).reshape(B, L, H * hd)


def _gated_head_norm(o, z, g):
    return _merge_heads(rms_norm(o, g) * jax.nn.silu(_heads(z)))


def _modulate(h, mod, i):
    return h * (1.0 + mod[:, :, 3 * i + 1]) + mod[:, :, 3 * i]


def _half_ffn(h, mod, i, g, w_i, w_o):
    hn = _modulate(rms_norm(h, g), mod, i)
    gate, up = jnp.split(hn @ w_i, 2, axis=-1)
    return h + 0.5 * mod[:, :, 3 * i + 2] * ((jax.nn.silu(gate) * up) @ w_o)


def _split_cols(t):
    cuts = []
    acc = 0
    for s in IN_SPLITS[:-1]:
        acc += s
        cuts.append(acc)
    return jnp.split(t, cuts, axis=-1)


def _centred_conv(x, w):
    width = w.shape[0]
    pad = width // 2
    L = x.shape[1]
    xp = jnp.pad(x, ((0, 0), (pad, pad), (0, 0)))
    return sum(xp[:, j:j + L] * w[j] for j in range(width))


def axial_rope(rows):
    n_freq = DIFF_DIM // 4
    inv = ROPE_THETA ** (-jnp.arange(n_freq, dtype=jnp.float32) / n_freq)
    r = jnp.repeat(jnp.arange(rows, dtype=jnp.float32), GRID_W)
    col = jnp.tile(jnp.arange(GRID_W, dtype=jnp.float32), rows)
    ang = jnp.concatenate([r[:, None] * inv, col[:, None] * inv], axis=-1)
    return jnp.cos(ang), jnp.sin(ang)


def _rope(t, cos, sin):
    cos = cos[None, :, None, None]
    sin = sin[None, :, None, None]
    t1, t2 = jnp.split(t, 2, axis=-1)
    return jnp.concatenate([t1 * cos - t2 * sin, t1 * sin + t2 * cos], axis=-1)


def _diff_softmax(q, k, v, lam):
    s = jnp.einsum('bhqcd,bhkcd->bhcqk', q, k) * DIFF_DIM ** -0.5
    p = jax.nn.softmax(s, axis=-1)
    return jnp.einsum('bhqk,bhkv->bhqv', p[:, :, 0] - lam * p[:, :, 1], v)


def diff_attention(px, pe, cos, sin, qk_g, lam_p, subln_g, lam_init, need_ctx):
    qx, kx, vx = px
    qe, ke, ve = pe
    B, L, _ = qx.shape

    def qk(t, g):
        t = t.astype(jnp.float32).reshape(t.shape[0], t.shape[1], N_HEADS, 2, DIFF_DIM)
        return rms_norm(t, g)

    qx = jnp.moveaxis(_rope(qk(qx, qk_g[0]), cos, sin), 1, 2)
    kx = jnp.moveaxis(_rope(qk(kx, qk_g[1]), cos, sin), 1, 2)
    ke = jnp.moveaxis(qk(ke, qk_g[1]), 1, 2)
    vx = _heads(vx)
    ve = _heads(ve)
    lam_p = lam_p.astype(jnp.float32)
    lam = jnp.exp(jnp.sum(lam_p[0] * lam_p[1])) - jnp.exp(jnp.sum(lam_p[2] * lam_p[3])) + lam_init
    k_all = jnp.concatenate([kx, ke], axis=2)
    v_all = jnp.concatenate([vx, ve], axis=2)
    nb = L // Q_BLOCK
    qb = jnp.moveaxis(qx.reshape(B, N_HEADS, nb, Q_BLOCK, 2, DIFF_DIM), 2, 0)
    ob = lax.map(lambda qq: _diff_softmax(qq, k_all, v_all, lam), qb)
    ox = jnp.moveaxis(ob, 0, 2).reshape(B, N_HEADS, L, HEAD_DIM)
    out_x = _merge_heads(rms_norm(ox, subln_g) * (1.0 - lam_init))
    out_e = None
    if need_ctx:
        qe = jnp.moveaxis(qk(qe, qk_g[0]), 1, 2)
        oe = _diff_softmax(qe, ke, ve, lam)
        out_e = _merge_heads(rms_norm(oe, subln_g) * (1.0 - lam_init))
    return out_x, out_e


def _unit_lower_inverse(a):
    C = a.shape[-1]
    n = -a
    p = jnp.eye(C, dtype=a.dtype) + n
    m = n
    for _ in range(int(math.log2(C)) - 1):
        m = m @ m
        p = p + p @ m
    return p


def _gdn_chunked(q, k, v, beta, g, s0):
    B, H, L, dk = q.shape
    dv = v.shape[-1]
    n = L // CHUNK
    q = q.reshape(B, H, n, CHUNK, dk)
    k = k.reshape(B, H, n, CHUNK, dk)
    v = v.reshape(B, H, n, CHUNK, dv)
    beta = beta.reshape(B, H, n, CHUNK)
    gc = jnp.cumsum(g.reshape(B, H, n, CHUNK), axis=-1)
    idx = jnp.arange(CHUNK)
    incl = idx[:, None] >= idx[None, :]
    strict = idx[:, None] > idx[None, :]
    decay = jnp.exp(jnp.where(incl, gc[..., :, None] - gc[..., None, :], MASK_NEG))
    kb = k * beta[..., None]
    a = jnp.where(strict, jnp.einsum('bhnid,bhnjd->bhnij', kb, k) * decay, 0.0)
    t = _unit_lower_inverse(a)
    u = t @ (v * beta[..., None])
    w = t @ (kb * jnp.exp(gc)[..., None])
    attn = jnp.einsum('bhnid,bhnjd->bhnij', q, k) * decay
    qg = q * jnp.exp(gc)[..., None]
    kg = k * jnp.exp(gc[..., -1:] - gc)[..., None]
    dl = jnp.exp(gc[..., -1])
    xs = tuple(jnp.moveaxis(z, 2, 0) for z in (u, w, attn, qg, kg, dl))

    def step(s, inp):
        u_i, w_i, a_i, qg_i, kg_i, d_i = inp
        v_new = u_i - w_i @ s
        o_i = qg_i @ s + a_i @ v_new
        s = s * d_i[..., None, None] + jnp.swapaxes(kg_i, -1, -2) @ v_new
        return s, o_i

    s, o = lax.scan(step, s0, xs)
    return jnp.moveaxis(o, 0, 2).reshape(B, H, L, dv), s


def _hgrn_chunked(q, k, v, g, s0):
    B, H, L, dk = q.shape
    dv = v.shape[-1]
    n = L // CHUNK
    q, k, g = (z.reshape(B, H, n, CHUNK, dk) for z in (q, k, g))
    v = v.reshape(B, H, n, CHUNK, dv)
    gc = jnp.cumsum(g, axis=3)
    qg = q * jnp.exp(gc)
    kg = k * jnp.exp(gc[..., -1:, :] - gc)
    dl = jnp.exp(gc[..., -1, :])
    idx = jnp.arange(CHUNK)
    incl = (idx[:, None] >= idx[None, :])[:, :, None]
    xs = tuple(jnp.moveaxis(z, 2, 0) for z in (q, k, v, gc, qg, kg, dl))

    def step(s, inp):
        q_i, k_i, v_i, gc_i, qg_i, kg_i, d_i = inp
        rel = jnp.exp(jnp.where(incl, gc_i[:, :, :, None, :] - gc_i[:, :, None, :, :], MASK_NEG))
        a_i = jnp.einsum('bhik,bhjk,bhijk->bhij', q_i, k_i, rel)
        o_i = qg_i @ s + a_i @ v_i
        s = s * d_i[..., None] + jnp.swapaxes(kg_i, -1, -2) @ v_i
        return s, o_i

    s, o = lax.scan(step, s0, xs)
    return jnp.moveaxis(o, 0, 2).reshape(B, H, L, dv), s


def _two_way(run, ctx_dirs, lat_dirs, s0):
    outs_c, outs_l = [], []
    for d in range(2):
        ca, la = ctx_dirs[d], lat_dirs[d]
        if d == 1:
            ca = tuple(jnp.flip(a, axis=2) for a in ca)
            la = tuple(jnp.flip(a, axis=2) for a in la)
        oc, sc = run(*ca, s0)
        ol, _ = run(*la, sc)
        if d == 1:
            oc, ol = jnp.flip(oc, axis=2), jnp.flip(ol, axis=2)
        outs_c.append(oc)
        outs_l.append(ol)
    return outs_c[0] + outs_c[1], outs_l[0] + outs_l[1]


def _gdn_prep(q, k, v, a, b, conv_w, a_log, dt_bias):
    B, L, _ = q.shape
    qkv = jax.nn.silu(_centred_conv(jnp.concatenate([q, k, v], axis=-1), conv_w)).astype(jnp.float32)
    q, k, v = jnp.split(qkv, 3, axis=-1)
    q = _l2norm(_heads(q)) * HEAD_DIM ** -0.5
    k = _l2norm(_heads(k))
    v = _heads(v)
    a = a.astype(jnp.float32).reshape(B, L, 2, N_HEADS)
    b = b.astype(jnp.float32).reshape(B, L, 2, N_HEADS)
    g = -jnp.exp(a_log.astype(jnp.float32)) * jax.nn.softplus(a + dt_bias.astype(jnp.float32))
    beta = jax.nn.sigmoid(b)
    return tuple((q, k, v, jnp.moveaxis(beta[:, :, d], 1, 2), jnp.moveaxis(g[:, :, d], 1, 2))
                 for d in range(2))


def gated_deltanet(px, pe, conv_w, a_log, dt_bias, norm_g, need_ctx):
    lat = _gdn_prep(px[0], px[1], px[2], px[4], px[5], conv_w, a_log, dt_bias)
    con = _gdn_prep(pe[0], pe[1], pe[2], pe[4], pe[5], conv_w, a_log, dt_bias)
    s0 = jnp.zeros((px[0].shape[0], N_HEADS, HEAD_DIM, HEAD_DIM), jnp.float32)
    o_e, o_x = _two_way(_gdn_chunked, con, lat, s0)
    out_x = _gated_head_norm(o_x, px[3], norm_g)
    out_e = _gated_head_norm(o_e, pe[3], norm_g) if need_ctx else None
    return out_x, out_e


def _hgrn_prep(hq, hf, hi, lb):
    B, L, _ = hq.shape
    q = _heads(hq)
    v = _heads(hi)
    f = hf.astype(jnp.float32).reshape(B, L, 2, N_HEADS, HEAD_DIM)
    lbh = lb.reshape(2, N_HEADS, HEAD_DIM)
    logf = jnp.logaddexp(jnp.log(jnp.maximum(lbh, LB_FLOOR)), jnp.log1p(-lbh) + jax.nn.log_sigmoid(f))
    k = (1.0 - lbh) * jax.nn.sigmoid(-f)
    return tuple((q, jnp.moveaxis(k[:, :, d], 1, 2), v, jnp.moveaxis(logf[:, :, d], 1, 2))
                 for d in range(2))


def hgrn2(px, pe, lb, norm_g, need_ctx):
    lat = _hgrn_prep(px[0], px[1], px[2], lb)
    con = _hgrn_prep(pe[0], pe[1], pe[2], lb)
    s0 = jnp.zeros((px[0].shape[0], N_HEADS, HEAD_DIM, HEAD_DIM), jnp.float32)
    o_e, o_x = _two_way(_hgrn_chunked, con, lat, s0)
    out_x = _gated_head_norm(o_x, px[3], norm_g)
    out_e = _gated_head_norm(o_e, pe[3], norm_g) if need_ctx else None
    return out_x, out_e


def _hyena_filter(L, w1, b1, w2, b2, w3, freq):
    f32 = jnp.float32
    bands = (HY_EMB - 1) // 2
    t = jnp.linspace(0.0, 1.0, L, dtype=f32)[:, None]
    wpos = 2.0 * math.pi * jnp.arange(L, dtype=f32)[:, None] / L
    fb = jnp.linspace(1e-4, bands - 1, bands, dtype=f32)[None]
    z = jnp.concatenate([t, jnp.cos(fb * wpos), -jnp.sin(fb * wpos)], axis=-1)
    freq = freq.astype(f32)
    h = jnp.sin(freq[0] * (z @ w1.astype(f32) + b1.astype(f32)))
    h = jnp.sin(freq[1] * (h @ w2.astype(f32) + b2.astype(f32)))
    h = h @ w3.astype(f32)
    deltas = jnp.abs(jnp.linspace(math.log(HY_TARGET) / HY_SLOW_DECAY,
                                  math.log(HY_TARGET) / HY_FAST_DECAY, BRANCH_W, dtype=f32))
    h = h * jnp.exp(-t * jnp.tile(deltas, 2))
    h_fwd, h_bwd = jnp.split(h, 2, axis=-1)
    return jnp.concatenate([h_fwd, jnp.zeros((1, BRANCH_W), f32), h_bwd[:L - 1][::-1]], axis=0)


def _fft_conv(u, kern):
    L = u.shape[1]
    n = 2 * L
    y = jnp.fft.irfft(jnp.fft.rfft(u, n=n, axis=1) * jnp.fft.rfft(kern, n=n, axis=0), n=n, axis=1)
    return y[:, :L]


def hyena(p, conv_w, conv_b, w1, b1, w2, b2, w3, freq, bias):
    B, L, _ = p[0].shape
    u = _centred_conv(jnp.concatenate(p, axis=-1), conv_w) + conv_b
    v, x0, x1 = jnp.split(u.astype(jnp.float32), 3, axis=-1)
    vg = v * x1
    kern = _hyena_filter(L, w1, b1, w2, b2, w3, freq)
    y = _fft_conv(vg, kern) + vg * bias.astype(jnp.float32)
    return x0 * y


def _merge(xn, branches, w_gate, w_up, w_out):
    acc = 0.0
    for i, o in enumerate(branches):
        acc = acc + jax.nn.sigmoid(xn @ w_gate[i]) * (o.astype(xn.dtype) @ w_up[i])
    return acc @ w_out


def setup_inputs(seed: int = 0) -> dict:
    key = jax.random.key(seed)
    ks = jax.random.split(key, 32)
    f32 = jnp.float32
    D = D_MODEL

    def nrm(i, shape, s):
        return jax.random.normal(ks[i], shape, f32) * s

    dt = jnp.exp(jax.random.uniform(ks[15], (DEPTH, 2, N_HEADS), f32, math.log(1e-3), math.log(1e-1)))
    return {
        'x': nrm(0, (BATCH, SEQ, D), 1.0),
        'c': nrm(1, (BATCH, D), 1.0),
        'ctx': nrm(2, (BATCH, CTX_LEN, D), 1.0),
        'c_ctx': nrm(3, (D,), 1.0),
        'norm_g': 1.0 + nrm(4, (DEPTH, 3, D), 0.02),
        'w_mod': nrm(5, (DEPTH, D, MOD_CHUNKS * D), 0.5 * D ** -0.5),
        'b_mod': nrm(6, (DEPTH, MOD_CHUNKS * D), 0.01),
        'ffn_w_in': nrm(7, (DEPTH, 2, D, 2 * HALF_FF), D ** -0.5),
        'ffn_w_out': nrm(8, (DEPTH, 2, HALF_FF, D), HALF_FF ** -0.5),
        'w_in': nrm(9, (DEPTH, D, IN_COLS), D ** -0.5),
        'attn_qk_g': 1.0 + nrm(10, (DEPTH, 2, DIFF_DIM), 0.02),
        'attn_lambda': nrm(11, (DEPTH, 4, DIFF_DIM), 0.1),
        'attn_subln_g': 1.0 + nrm(12, (DEPTH, HEAD_DIM), 0.02),
        'gdn_conv_w': nrm(13, (DEPTH, CONV_W, 3 * BRANCH_W), CONV_W ** -0.5),
        'gdn_a_log': jnp.log(jax.random.uniform(ks[14], (DEPTH, 2, N_HEADS), f32, 1.0, 16.0)),
        'gdn_dt_bias': dt + jnp.log(-jnp.expm1(-dt)),
        'gdn_norm_g': 1.0 + nrm(16, (DEPTH, HEAD_DIM), 0.02),
        'hg_lb_logits': nrm(17, (DEPTH, 2, BRANCH_W), 0.5),
        'hg_norm_g': 1.0 + nrm(18, (DEPTH, HEAD_DIM), 0.02),
        'hy_conv_w': nrm(19, (DEPTH, CONV_W, 3 * BRANCH_W), CONV_W ** -0.5),
        'hy_conv_b': nrm(20, (DEPTH, 3 * BRANCH_W), 0.01),
        'hy_w1': nrm(21, (DEPTH, HY_EMB, HY_ORDER), HY_EMB ** -0.5),
        'hy_b1': nrm(22, (DEPTH, HY_ORDER), 0.1),
        'hy_w2': nrm(23, (DEPTH, HY_ORDER, HY_ORDER), HY_ORDER ** -0.5),
        'hy_b2': nrm(24, (DEPTH, HY_ORDER), 0.1),
        'hy_w3': nrm(25, (DEPTH, HY_ORDER, 2 * BRANCH_W), 0.05 * HY_ORDER ** -0.5),
        'hy_freq': 1.0 + nrm(26, (DEPTH, 2, HY_ORDER), 0.01),
        'hy_bias': nrm(27, (DEPTH, BRANCH_W), 0.1),
        'w_gate': nrm(28, (DEPTH, N_BRANCH, D, D), D ** -0.5),
        'w_up': nrm(29, (DEPTH, N_BRANCH, BRANCH_W, D), BRANCH_W ** -0.5),
        'w_out': nrm(30, (DEPTH, D, D), D ** -0.5),
    }


def reference(x, c, ctx, c_ctx, norm_g, w_mod, b_mod, ffn_w_in, ffn_w_out, w_in,
              attn_qk_g, attn_lambda, attn_subln_g,
              gdn_conv_w, gdn_a_log, gdn_dt_bias, gdn_norm_g,
              hg_lb_logits, hg_norm_g,
              hy_conv_w, hy_conv_b, hy_w1, hy_b1, hy_w2, hy_b2, hy_w3, hy_freq, hy_bias,
              w_gate, w_up, w_out):
    rows = x.shape[1] // GRID_W
    cos, sin = axial_rope(rows)
    p_lb = jax.nn.softmax(hg_lb_logits.astype(jnp.float32), axis=0)
    lb_all = jnp.cumsum(p_lb, axis=0) - p_lb[:1]
    e = ctx
    for l in range(DEPTH):
        last = l == DEPTH - 1
        mod_x = (jax.nn.silu(c) @ w_mod[l] + b_mod[l]).reshape(c.shape[0], 1, MOD_CHUNKS, D_MODEL)
        mod_e = (jax.nn.silu(c_ctx)[None] @ w_mod[l] + b_mod[l]).reshape(1, 1, MOD_CHUNKS, D_MODEL)
        x = _half_ffn(x, mod_x, 0, norm_g[l, 0], ffn_w_in[l, 0], ffn_w_out[l, 0])
        e = _half_ffn(e, mod_e, 0, norm_g[l, 0], ffn_w_in[l, 0], ffn_w_out[l, 0])
        xn = _modulate(rms_norm(x, norm_g[l, 1]), mod_x, 1)
        en = _modulate(rms_norm(e, norm_g[l, 1]), mod_e, 1)
        px = _split_cols(xn @ w_in[l])
        pe = _split_cols(en @ w_in[l])
        lam_init = 0.8 - 0.6 * math.exp(-0.3 * l)
        att_x, att_e = diff_attention(px[0:3], pe[0:3], cos, sin, attn_qk_g[l], attn_lambda[l],
                                      attn_subln_g[l], lam_init, not last)
        gdn_x, gdn_e = gated_deltanet(px[3:9], pe[3:9], gdn_conv_w[l], gdn_a_log[l],
                                      gdn_dt_bias[l], gdn_norm_g[l], not last)
        hg_x, hg_e = hgrn2(px[9:13], pe[9:13], lb_all[l], hg_norm_g[l], not last)
        hy_x = hyena(px[13:16], hy_conv_w[l], hy_conv_b[l], hy_w1[l], hy_b1[l], hy_w2[l],
                     hy_b2[l], hy_w3[l], hy_freq[l], hy_bias[l])
        x = x + mod_x[:, :, 5] * _merge(xn, (att_x, gdn_x, hg_x, hy_x), w_gate[l], w_up[l], w_out[l])
        x = _half_ffn(x, mod_x, 2, norm_g[l, 2], ffn_w_in[l, 1], ffn_w_out[l, 1])
        if not last:
            hy_e = hyena(pe[13:16], hy_conv_w[l], hy_conv_b[l], hy_w1[l], hy_b1[l], hy_w2[l],
                         hy_b2[l], hy_w3[l], hy_freq[l], hy_bias[l])
            e = e + mod_e[:, :, 5] * _merge(en, (att_e, gdn_e, hg_e, hy_e), w_gate[l], w_up[l], w_out[l])
            e = _half_ffn(e, mod_e, 2, norm_g[l, 2], ffn_w_in[l, 1], ffn_w_out[l, 1])
    return x
```

```python
import functools
import math

import jax
import jax.numpy as jnp
from jax import lax
from jax.experimental import pallas as pl
from jax.experimental.pallas import tpu as pltpu

F32 = jnp.float32
BF16 = jnp.bfloat16

GRID_W = 64
N_BRANCH = 4
HEAD_DIM = 128
DIFF_DIM = HEAD_DIM // 2
MOD_CHUNKS = 9
ROPE_THETA = 10000.0
CHUNK = 64
HY_EMB = 33
HY_FAST_DECAY = 0.3
HY_SLOW_DECAY = 1.5
HY_TARGET = 1e-2
EPS = 1e-6
MASK_NEG = -1e30
LB_FLOOR = 1e-20

LANES = 128
V7X_VMEM_LIMIT = 56 * 1024 * 1024
HIGHEST = lax.Precision.HIGHEST

_NT = (((1,), (1,)), ((), ()))
_TN = (((0,), (0,)), ((), ()))


def _params(*sem):
    return pltpu.CompilerParams(dimension_semantics=sem, vmem_limit_bytes=V7X_VMEM_LIMIT)


def _tile(n, pref):
    t = min(n, pref)
    while n % t:
        t //= 2
    return t


def _sigmoid(x):
    return 1.0 / (1.0 + jnp.exp(-x))


def _silu(x):
    return x * _sigmoid(x)


def _softplus(x):
    return jnp.maximum(x, 0.0) + jnp.log1p(jnp.exp(-jnp.abs(x)))


class Layout:
    def __init__(self, B, L, Lc):
        self.B, self.L, self.Lc = B, L, Lc
        self.S = L + Lc
        self.T = B * self.S
        self.gr = math.gcd(L, Lc)
        self.NG = self.S // self.gr
        self.NGc = Lc // self.gr


def _mod_row(mod_ref, lay, r, k):
    b = r // lay.NG
    g = jnp.where(r - b * lay.NG < lay.NGc, lay.B, b)
    return mod_ref[pl.ds(g, 1), k:k + 1, :][0]


def _mod_body(c_ref, w_ref, b_ref, o_ref):
    a = _silu(c_ref[...]).astype(BF16)
    o_ref[...] = jnp.dot(a, w_ref[...].astype(BF16), preferred_element_type=F32) + b_ref[...]


def mod_table(c16, w_mod, b_mod):
    R, D = c16.shape
    N = w_mod.shape[1]
    tn = _tile(N, 512)
    return pl.pallas_call(
        _mod_body,
        out_shape=jax.ShapeDtypeStruct((R, N), F32),
        grid=(N // tn,),
        in_specs=[pl.BlockSpec((R, D), lambda j: (0, 0)),
                  pl.BlockSpec((D, tn), lambda j: (0, j)),
                  pl.BlockSpec((1, tn), lambda j: (0, j))],
        out_specs=pl.BlockSpec((R, tn), lambda j: (0, j)),
        compiler_params=_params("parallel"),
        name="mod_table",
    )(c16, w_mod, b_mod.reshape(1, N))


def _norm_mod_body(x_ref, g_ref, mod_ref, o_ref, *, lay, k):
    i = pl.program_id(0)
    gr = lay.gr
    n_sub = x_ref.shape[0] // gr
    for s in range(n_sub):
        x = x_ref[s * gr:(s + 1) * gr, :]
        y = x * lax.rsqrt(jnp.mean(x * x, axis=-1, keepdims=True) + EPS) * g_ref[...]
        r = i * n_sub + s
        shift = _mod_row(mod_ref, lay, r, 3 * k)
        scale = _mod_row(mod_ref, lay, r, 3 * k + 1)
        o_ref[s * gr:(s + 1) * gr, :] = (y * (1.0 + scale) + shift).astype(BF16)


def norm_mod(h, g, mod, lay, k):
    T, D = h.shape
    tm = lay.gr
    G = mod.shape[0]
    return pl.pallas_call(
        functools.partial(_norm_mod_body, lay=lay, k=k),
        out_shape=jax.ShapeDtypeStruct((T, D), BF16),
        grid=(T // tm,),
        in_specs=[pl.BlockSpec((tm, D), lambda i: (i, 0)),
                  pl.BlockSpec((1, D), lambda i: (0, 0)),
                  pl.BlockSpec((G, MOD_CHUNKS, D), lambda i: (0, 0, 0))],
        out_specs=pl.BlockSpec((tm, D), lambda i: (i, 0)),
        compiler_params=_params("parallel"),
        name="norm_mod",
    )(h, g.reshape(1, D), mod)


def _mm_swiglu_body(a_ref, wg_ref, wu_ref, o_ref, wgb, wub):
    @pl.when(pl.program_id(1) == 0)
    def _():
        wgb[...] = wg_ref[...].astype(BF16)
        wub[...] = wu_ref[...].astype(BF16)

    a = a_ref[...]
    g = jnp.dot(a, wgb[...], preferred_element_type=F32)
    u = jnp.dot(a, wub[...], preferred_element_type=F32)
    o_ref[...] = (_silu(g) * u).astype(BF16)


def mm_swiglu(a, w):
    T, K = a.shape
    F = w.shape[1] // 2
    tm, tn = _tile(T, 1024), _tile(F, 256)
    nj = F // tn
    return pl.pallas_call(
        _mm_swiglu_body,
        out_shape=jax.ShapeDtypeStruct((T, F), BF16),
        grid=(nj, T // tm),
        in_specs=[pl.BlockSpec((tm, K), lambda j, i: (i, 0)),
                  pl.BlockSpec((K, tn), lambda j, i: (0, j)),
                  pl.BlockSpec((K, tn), lambda j, i: (0, j + nj))],
        out_specs=pl.BlockSpec((tm, tn), lambda j, i: (i, j)),
        scratch_shapes=[pltpu.VMEM((K, tn), BF16), pltpu.VMEM((K, tn), BF16)],
        compiler_params=_params("parallel", "arbitrary"),
        name="mm_swiglu",
    )(a, w, w)


def _mm_resid_body(a_ref, w_ref, h_ref, mod_ref, o_ref, wb, *, lay, k, scale):
    i = pl.program_id(1)

    @pl.when(i == 0)
    def _():
        wb[...] = w_ref[...].astype(BF16)

    y = jnp.dot(a_ref[...], wb[...], preferred_element_type=F32)
    gr = lay.gr
    n_sub = a_ref.shape[0] // gr
    for s in range(n_sub):
        gate = _mod_row(mod_ref, lay, i * n_sub + s, k)
        sl = slice(s * gr, (s + 1) * gr)
        o_ref[sl, :] = h_ref[sl, :] + (scale * gate) * y[sl, :]


def mm_resid(a, w, h, mod, lay, k, scale):
    T, K = a.shape
    N = w.shape[1]
    G = mod.shape[0]
    tm, tn = _tile(T, 512), _tile(N, 512)
    tm = max(tm, lay.gr)
    return pl.pallas_call(
        functools.partial(_mm_resid_body, lay=lay, k=k, scale=scale),
        out_shape=jax.ShapeDtypeStruct((T, N), F32),
        grid=(N // tn, T // tm),
        in_specs=[pl.BlockSpec((tm, K), lambda j, i: (i, 0)),
                  pl.BlockSpec((K, tn), lambda j, i: (0, j)),
                  pl.BlockSpec((tm, tn), lambda j, i: (i, j)),
                  pl.BlockSpec((G, MOD_CHUNKS, tn), lambda j, i: (0, 0, j))],
        out_specs=pl.BlockSpec((tm, tn), lambda j, i: (i, j)),
        scratch_shapes=[pltpu.VMEM((K, tn), BF16)],
        compiler_params=_params("parallel", "arbitrary"),
        name="mm_resid",
    )(a, w, h, mod)


def _mm_plain_body(a_ref, w_ref, o_ref, wb):
    @pl.when(pl.program_id(1) == 0)
    def _():
        wb[...] = w_ref[...].astype(BF16)

    o_ref[...] = jnp.dot(a_ref[...], wb[...], preferred_element_type=F32)


def mm_plain(a, w, tn_pref=512):
    T, K = a.shape
    N = w.shape[1]
    tm, tn = _tile(T, 1024), _tile(N, tn_pref)
    return pl.pallas_call(
        _mm_plain_body,
        out_shape=jax.ShapeDtypeStruct((T, N), F32),
        grid=(N // tn, T // tm),
        in_specs=[pl.BlockSpec((tm, K), lambda j, i: (i, 0)),
                  pl.BlockSpec((K, tn), lambda j, i: (0, j))],
        out_specs=pl.BlockSpec((tm, tn), lambda j, i: (i, j)),
        scratch_shapes=[pltpu.VMEM((K, tn), BF16)],
        compiler_params=_params("parallel", "arbitrary"),
        name="mm_plain",
    )(a, w)


def _merge_body(xn_ref, o0_ref, o1_ref, o2_ref, o3_ref, wg_ref, wu_ref, out_ref):
    xn = xn_ref[...]
    acc = None
    for br, o_ref in enumerate((o0_ref, o1_ref, o2_ref, o3_ref)):
        g = jnp.dot(xn, wg_ref[br], preferred_element_type=F32)
        u = jnp.dot(o_ref[...], wu_ref[br], preferred_element_type=F32)
        term = _sigmoid(g) * u
        acc = term if acc is None else acc + term
    out_ref[...] = acc.astype(BF16)


def merge_branches(xn, branches, w_gate_b, w_up_b):
    T, D = xn.shape
    W = branches[0].shape[1]
    tm, tn = _tile(T, 512), _tile(D, 256)
    bspec = pl.BlockSpec((tm, W), lambda i, j: (i, 0))
    return pl.pallas_call(
        _merge_body,
        out_shape=jax.ShapeDtypeStruct((T, D), BF16),
        grid=(T // tm, D // tn),
        in_specs=[pl.BlockSpec((tm, D), lambda i, j: (i, 0)), bspec, bspec, bspec, bspec,
                  pl.BlockSpec((N_BRANCH, D, tn), lambda i, j: (0, 0, j)),
                  pl.BlockSpec((N_BRANCH, W, tn), lambda i, j: (0, 0, j))],
        out_specs=pl.BlockSpec((tm, tn), lambda i, j: (i, j)),
        compiler_params=_params("parallel", "arbitrary"),
        name="merge_branches",
    )(xn, *branches, w_gate_b, w_up_b)


def _attn_prep_body(q_ref, k_ref, cos_ref, sin_ref, g_ref, qo_ref, ko_ref):
    cos = cos_ref[...]
    sin = sin_ref[...]
    lane = lax.broadcasted_iota(jnp.int32, cos.shape, 1)
    lo = lane < DIFF_DIM
    first = (lane % DIFF_DIM) < DIFF_DIM // 2
    n_heads = q_ref.shape[1] // HEAD_DIM
    for src, dst, gi, sc in ((q_ref, qo_ref, 0, DIFF_DIM ** -0.5), (k_ref, ko_ref, 1, 1.0)):
        g = g_ref[gi:gi + 1, :]
        for h in range(n_heads):
            cs = slice(h * HEAD_DIM, (h + 1) * HEAD_DIM)
            t = src[:, cs]
            ss = t * t
            s_lo = jnp.sum(jnp.where(lo, ss, 0.0), axis=-1, keepdims=True)
            s_hi = jnp.sum(jnp.where(lo, 0.0, ss), axis=-1, keepdims=True)
            inv = jnp.where(lo, lax.rsqrt(s_lo / DIFF_DIM + EPS), lax.rsqrt(s_hi / DIFF_DIM + EPS))
            y = t * inv * g
            partner = jnp.where(first, pltpu.roll(y, HEAD_DIM - DIFF_DIM // 2, 1), pltpu.roll(y, DIFF_DIM // 2, 1))
            dst[:, cs] = ((y * cos + partner * sin) * sc).astype(BF16)


def attn_prep(px, cos_t, sin_t, qk_g2, lay, W):
    T = px.shape[0]
    tr = lay.gr
    return pl.pallas_call(
        _attn_prep_body,
        out_shape=(jax.ShapeDtypeStruct((T, W), BF16), jax.ShapeDtypeStruct((T, W), BF16)),
        grid=(T // tr,),
        in_specs=[pl.BlockSpec((tr, W), lambda i: (i, 0)),
                  pl.BlockSpec((tr, W), lambda i: (i, 1)),
                  pl.BlockSpec((tr, LANES), lambda i: (i % lay.NG, 0)),
                  pl.BlockSpec((tr, LANES), lambda i: (i % lay.NG, 0)),
                  pl.BlockSpec((2, LANES), lambda i: (0, 0))],
        out_specs=(pl.BlockSpec((tr, W), lambda i: (i, 0)), pl.BlockSpec((tr, W), lambda i: (i, 0))),
        compiler_params=_params("parallel"),
        name="attn_prep",
    )(px, px, cos_t, sin_t, qk_g2)


def _attn_body(q_ref, k_ref, v_ref, lam_ref, g_ref, o_ref, *, lay, lam_init):
    qi = pl.program_id(2)
    lp = lam_ref[...]
    lam = (jnp.exp(jnp.sum(lp[0:1] * lp[1:2], axis=-1, keepdims=True))
           - jnp.exp(jnp.sum(lp[2:3] * lp[3:4], axis=-1, keepdims=True)) + lam_init)

    def run(nk):
        q = q_ref[...]
        k = k_ref[0:nk, :]
        v = v_ref[0:nk, :].astype(BF16)
        lane = lax.broadcasted_iota(jnp.int32, q.shape, 1)
        zero = jnp.zeros_like(q)
        outs = []
        for qq in (jnp.where(lane < DIFF_DIM, q, zero), jnp.where(lane < DIFF_DIM, zero, q)):
            s = lax.dot_general(qq, k, _NT, preferred_element_type=F32)
            p = jnp.exp(s - jnp.max(s, axis=-1, keepdims=True))
            l = jnp.sum(p, axis=-1, keepdims=True)
            outs.append(jnp.dot(p.astype(BF16), v, preferred_element_type=F32) / l)
        o = outs[0] - lam * outs[1]
        y = o * lax.rsqrt(jnp.mean(o * o, axis=-1, keepdims=True) + EPS) * g_ref[...]
        o_ref[...] = (y * (1.0 - lam_init)).astype(BF16)

    @pl.when(qi < lay.NGc)
    def _():
        run(lay.Lc)

    @pl.when(qi >= lay.NGc)
    def _():
        run(lay.S)


def attention(q_rot, k_rot, px, lam_p, subln_g, lay, W, lam_init):
    T = px.shape[0]
    H = W // HEAD_DIM
    tq = lay.gr
    S = lay.S
    vb = W // HEAD_DIM * 2
    return pl.pallas_call(
        functools.partial(_attn_body, lay=lay, lam_init=lam_init),
        out_shape=jax.ShapeDtypeStruct((T, W), BF16),
        grid=(lay.B, H, lay.NG),
        in_specs=[pl.BlockSpec((tq, HEAD_DIM), lambda b, h, i: (b * lay.NG + i, h)),
                  pl.BlockSpec((S, HEAD_DIM), lambda b, h, i: (b, h)),
                  pl.BlockSpec((S, HEAD_DIM), lambda b, h, i: (b, vb + h)),
                  pl.BlockSpec((4, DIFF_DIM), lambda b, h, i: (0, 0)),
                  pl.BlockSpec((1, HEAD_DIM), lambda b, h, i: (0, 0))],
        out_specs=pl.BlockSpec((tq, HEAD_DIM), lambda b, h, i: (b * lay.NG + i, h)),
        compiler_params=_params("parallel", "parallel", "arbitrary"),
        name="diff_attention",
    )(q_rot, k_rot, px, lam_p, subln_g.reshape(1, HEAD_DIM))


def _chunk_scan(y, reverse):
    S = y.shape[0]
    pos = lax.broadcasted_iota(jnp.int32, y.shape, 0) % CHUNK
    sh = 1
    while sh < CHUNK:
        if reverse:
            y = y + jnp.where(pos < CHUNK - sh, pltpu.roll(y, S - sh, 0), 0.0)
        else:
            y = y + jnp.where(pos >= sh, pltpu.roll(y, sh, 0), 0.0)
        sh *= 2
    return y


def _chunk_of_step(n, d, lay):
    ncc = lay.Lc // CHUNK
    nc = lay.S // CHUNK
    if d == 0:
        return n
    return jnp.where(n < ncc, ncc - 1 - n, nc - 1 - (n - ncc))


def _gated_head_norm(o, z, g):
    y = o * lax.rsqrt(jnp.mean(o * o, axis=-1, keepdims=True) + EPS) * g
    return (y * _silu(z)).astype(BF16)


def _centred_conv(x, w, prev_ok, next_ok):
    S = x.shape[0]
    xp = jnp.where(prev_ok, pltpu.roll(x, 1, 0), 0.0)
    xn = jnp.where(next_ok, pltpu.roll(x, S - 1, 0), 0.0)
    return xp * w[0:1] + x * w[1:2] + xn * w[2:3]


def _conv_masks(shape, lay):
    row = lax.broadcasted_iota(jnp.int32, shape, 0)
    prev_ok = (row * (row - lay.Lc)) != 0
    next_ok = ((row - (lay.Lc - 1)) * (row - (lay.S - 1))) != 0
    return prev_ok, next_ok


def _gdn_body(q_ref, k_ref, v_ref, z_ref, ps_ref, cq_ref, ck_ref, cv_ref, alog_ref, dtb_ref, g_ref, o_ref,
              qs, ks, vs, gc, bb, u_s, w_s, at_s, qg_s, kg_s, dl_s, o_s, *, lay):
    h = pl.program_id(1)
    S = lay.S
    nc = S // CHUNK
    shape = (S, HEAD_DIM)
    prev_ok, next_ok = _conv_masks(shape, lay)
    lane = lax.broadcasted_iota(jnp.int32, shape, 1)

    q = _silu(_centred_conv(q_ref[...], cq_ref[...], prev_ok, next_ok))
    qs[...] = q * lax.rsqrt(jnp.sum(q * q, axis=-1, keepdims=True) + EPS) * HEAD_DIM ** -0.5
    k = _silu(_centred_conv(k_ref[...], ck_ref[...], prev_ok, next_ok))
    ks[...] = k * lax.rsqrt(jnp.sum(k * k, axis=-1, keepdims=True) + EPS)
    vs[...] = _silu(_centred_conv(v_ref[...], cv_ref[...], prev_ok, next_ok))

    p = ps_ref[...]
    g_all = -jnp.exp(alog_ref[...]) * _softplus(p + dtb_ref[...])
    beta_all = _sigmoid(p)
    n_heads = pl.num_programs(1)
    for d in range(2):
        gsel = jnp.sum(jnp.where(lane == d * n_heads + h, g_all, 0.0), axis=-1, keepdims=True)
        gc[d] = _chunk_scan(jnp.broadcast_to(gsel, shape), reverse=(d == 1))
        bsel = jnp.sum(jnp.where(lane == (2 + d) * n_heads + h, beta_all, 0.0), axis=-1, keepdims=True)
        bb[d] = jnp.broadcast_to(bsel, shape)

    ii = lax.broadcasted_iota(jnp.int32, (CHUNK, CHUNK), 0)
    jj = lax.broadcasted_iota(jnp.int32, (CHUNK, CHUNK), 1)
    eye = jnp.where(ii == jj, 1.0, 0.0)
    clane = lax.broadcasted_iota(jnp.int32, (CHUNK, HEAD_DIM), 1)
    n_sq = int(math.log2(CHUNK)) - 1

    def prep(c, d):
        sl = pl.ds(pl.multiple_of(c * CHUNK, CHUNK), CHUNK)
        qc, kc, vc, gcb, bc = qs[sl, :], ks[sl, :], vs[sl, :], gc[d, sl, :], bb[d, sl, :]
        glast = gcb[CHUNK - 1:CHUNK, :] if d == 0 else gcb[0:1, :]
        eg = jnp.exp(gcb)
        kb = kc * bc
        lhs = jnp.where(clane == 0, gcb, jnp.where(clane == 1, 1.0, 0.0))
        rhs = jnp.where(clane == 0, 1.0, jnp.where(clane == 1, -gcb, 0.0))
        diff = lax.dot_general(lhs, rhs, _NT, precision=HIGHEST, preferred_element_type=F32)
        incl = (ii >= jj) if d == 0 else (ii <= jj)
        strict = (ii > jj) if d == 0 else (ii < jj)
        decay = jnp.exp(jnp.where(incl, diff, MASK_NEG))
        kbf = kc.astype(BF16)
        qbf = qc.astype(BF16)
        a = jnp.where(strict, lax.dot_general(kb.astype(BF16), kbf, _NT, preferred_element_type=F32) * decay, 0.0)
        m = -a
        t = eye + m
        for _ in range(n_sq):
            mb = m.astype(BF16)
            m = jnp.dot(mb, mb, preferred_element_type=F32)
            t = t + jnp.dot(t.astype(BF16), m.astype(BF16), preferred_element_type=F32)
        tb = t.astype(BF16)
        u_s[d, sl, :] = jnp.dot(tb, (vc * bc).astype(BF16), preferred_element_type=F32)
        w_s[d, sl, :] = jnp.dot(tb, (kb * eg).astype(BF16), preferred_element_type=F32).astype(BF16)
        at_s[d, sl, :] = (lax.dot_general(qbf, kbf, _NT, preferred_element_type=F32) * decay).astype(BF16)
        qg_s[d, sl, :] = (qc * eg).astype(BF16)
        kg_s[d, sl, :] = (kc * jnp.exp(glast - gcb)).astype(BF16)
        dl_s[d, pl.ds(c, 1), :, :] = jnp.broadcast_to(jnp.exp(glast), (1, 8, HEAD_DIM))

    group = 4 if nc % 4 == 0 else 1

    def prep_loop(gidx, carry):
        for uu in range(group):
            for d in range(2):
                prep(gidx * group + uu, d)
        return carry

    lax.fori_loop(0, nc // group, prep_loop, 0)

    def step_d(d, n, s):
        c = _chunk_of_step(n, d, lay)
        sl = pl.ds(pl.multiple_of(c * CHUNK, CHUNK), CHUNK)
        sb = s.astype(BF16)
        v_new = u_s[d, sl, :] - jnp.dot(w_s[d, sl, :], sb, preferred_element_type=F32)
        vb = v_new.astype(BF16)
        o_s[d, sl, :] = (jnp.dot(qg_s[d, sl, :], sb, preferred_element_type=F32)
                         + jnp.dot(at_s[d, sl, :], vb, preferred_element_type=F32))
        dl = dl_s[d, pl.ds(c, 1), :, :][0][0:1, :]
        return s * dl + lax.dot_general(kg_s[d, sl, :], vb, _TN, preferred_element_type=F32)

    def both(n, carry):
        return (step_d(0, n, carry[0]), step_d(1, n, carry[1]))

    zero = jnp.zeros((HEAD_DIM, HEAD_DIM), F32)
    lax.fori_loop(0, nc, both, (zero, zero))
    o_ref[...] = _gated_head_norm(o_s[0] + o_s[1], z_ref[...], g_ref[...])


def gated_deltanet(px, ps, conv_w, alog_l, dtb_l, norm_g, lay, W, col0):
    T = px.shape[0]
    H = W // HEAD_DIM
    S = lay.S
    nc = S // CHUNK

    def col(o):
        return pl.BlockSpec((S, HEAD_DIM), lambda b, h: (b, col0 + o * H + h))

    def cw(o):
        return pl.BlockSpec((3, HEAD_DIM), lambda b, h: (0, o * H + h))

    one = pl.BlockSpec((1, HEAD_DIM), lambda b, h: (0, 0))
    sh = (S, HEAD_DIM)
    return pl.pallas_call(
        functools.partial(_gdn_body, lay=lay),
        out_shape=jax.ShapeDtypeStruct((T, W), BF16),
        grid=(lay.B, H),
        in_specs=[col(0), col(1), col(2), col(3),
                  pl.BlockSpec((S, LANES), lambda b, h: (b, 0)),
                  cw(0), cw(1), cw(2), one, one, one],
        out_specs=pl.BlockSpec((S, HEAD_DIM), lambda b, h: (b, h)),
        scratch_shapes=[pltpu.VMEM(sh, F32), pltpu.VMEM(sh, F32), pltpu.VMEM(sh, F32),
                        pltpu.VMEM((2,) + sh, F32), pltpu.VMEM((2,) + sh, F32),
                        pltpu.VMEM((2,) + sh, F32), pltpu.VMEM((2,) + sh, BF16),
                        pltpu.VMEM((2, S, CHUNK), BF16), pltpu.VMEM((2,) + sh, BF16),
                        pltpu.VMEM((2,) + sh, BF16), pltpu.VMEM((2, nc, 8, HEAD_DIM), F32),
                        pltpu.VMEM((2,) + sh, F32)],
        compiler_params=_params("parallel", "arbitrary"),
        name="gated_deltanet",
    )(px, px, px, px, ps, conv_w, conv_w, conv_w, alog_l, dtb_l, norm_g.reshape(1, HEAD_DIM))


def _hgrn_body(q_ref, f0_ref, f1_ref, i_ref, z_ref, lb0_ref, lb1_ref, g_ref, o_ref,
               gc, kk, o_s, *, lay, layer):
    S = lay.S
    nc = S // CHUNK
    for d, (f_ref, lb_ref) in enumerate(((f0_ref, lb0_ref), (f1_ref, lb1_ref))):
        lg = lb_ref[...]
        depth = lg.shape[0]
        mx = lg[0:1]
        for r in range(1, depth):
            mx = jnp.maximum(mx, lg[r:r + 1])
        e = [jnp.exp(lg[r:r + 1] - mx) for r in range(depth)]
        tot = e[0]
        for r in range(1, depth):
            tot = tot + e[r]
        lb = jnp.zeros_like(mx)
        for r in range(1, layer + 1):
            lb = lb + e[r] / tot
        f = f_ref[...]
        log_sig = jnp.minimum(f, 0.0) - jnp.log1p(jnp.exp(-jnp.abs(f)))
        a = jnp.log(jnp.maximum(lb, LB_FLOOR))
        b = jnp.log1p(-lb) + log_sig
        logf = jnp.maximum(a, b) + jnp.log1p(jnp.exp(-jnp.abs(a - b)))
        gc[d] = _chunk_scan(logf, reverse=(d == 1))
        kk[d] = (1.0 - lb) * _sigmoid(-f)

    jrow = lax.broadcasted_iota(jnp.int32, (CHUNK, HEAD_DIM), 0)
    alane = lax.broadcasted_iota(jnp.int32, (CHUNK, CHUNK), 1)

    def step_d(d, n, st):
        c = _chunk_of_step(n, d, lay)
        sl = pl.ds(pl.multiple_of(c * CHUNK, CHUNK), CHUNK)
        qc, kc, vc, g = q_ref[sl, :], kk[d, sl, :], i_ref[sl, :], gc[d, sl, :]
        glast = g[CHUNK - 1:CHUNK, :] if d == 0 else g[0:1, :]
        a_t = jnp.zeros((CHUNK, CHUNK), F32)
        for i in range(CHUNK):
            keep = (jrow <= i) if d == 0 else (jrow >= i)
            rel = jnp.exp(jnp.where(keep, g[i:i + 1, :] - g, MASK_NEG))
            colv = jnp.sum(rel * kc * qc[i:i + 1, :], axis=-1, keepdims=True)
            a_t = jnp.where(alane == i, colv, a_t)
        vb = vc.astype(BF16)
        o = (lax.dot_general((qc * jnp.exp(g)).astype(BF16), st.astype(BF16), _NT, preferred_element_type=F32)
             + lax.dot_general(a_t.astype(BF16), vb, _TN, preferred_element_type=F32))
        o_s[d, sl, :] = o
        kg = (kc * jnp.exp(glast - g)).astype(BF16)
        return st * jnp.exp(glast) + lax.dot_general(vb, kg, _TN, preferred_element_type=F32)

    def both(n, carry):
        return (step_d(0, n, carry[0]), step_d(1, n, carry[1]))

    zero = jnp.zeros((HEAD_DIM, HEAD_DIM), F32)
    lax.fori_loop(0, nc, both, (zero, zero))
    o_ref[...] = _gated_head_norm(o_s[0] + o_s[1], z_ref[...], g_ref[...])


def hgrn2(px, lb_logits, norm_g, lay, W, col0, layer):
    T = px.shape[0]
    H = W // HEAD_DIM
    S = lay.S
    depth = lb_logits.shape[0]

    def col(o):
        return pl.BlockSpec((S, HEAD_DIM), lambda b, h: (b, col0 + o * H + h))

    def lbs(d):
        return pl.BlockSpec((depth, HEAD_DIM), lambda b, h: (0, d * H + h))

    sh = (S, HEAD_DIM)
    return pl.pallas_call(
        functools.partial(_hgrn_body, lay=lay, layer=layer),
        out_shape=jax.ShapeDtypeStruct((T, W), BF16),
        grid=(lay.B, H),
        in_specs=[col(0), col(1), col(2), col(3), col(4), lbs(0), lbs(1),
                  pl.BlockSpec((1, HEAD_DIM), lambda b, h: (0, 0))],
        out_specs=pl.BlockSpec((S, HEAD_DIM), lambda b, h: (b, h)),
        scratch_shapes=[pltpu.VMEM((2,) + sh, F32), pltpu.VMEM((2,) + sh, F32), pltpu.VMEM((2,) + sh, F32)],
        compiler_params=_params("parallel", "arbitrary"),
        name="hgrn2",
    )(px, px, px, px, px, lb_logits, lb_logits, norm_g.reshape(1, HEAD_DIM))


def _split_bf16(x):
    hi = x.astype(BF16)
    return hi, (x - hi.astype(F32)).astype(BF16)


def _dot3(a_hi, a_lo, b_hi, b_lo):
    return (jnp.dot(a_hi, b_hi, preferred_element_type=F32)
            + jnp.dot(a_hi, b_lo, preferred_element_type=F32)
            + jnp.dot(a_lo, b_hi, preferred_element_type=F32))


def _hy_filter_body(z_ref, w1_ref, b1_ref, w2_ref, b2_ref, w3_ref, fr_ref, dl_ref, o_ref, *, L):
    i = pl.program_id(0)
    z = z_ref[...]
    tr = z.shape[0]
    W = o_ref.shape[1]
    h = jnp.sin(fr_ref[0:1, :] * (jnp.dot(z, w1_ref[...], precision=HIGHEST, preferred_element_type=F32) + b1_ref[...]))
    h = jnp.sin(fr_ref[1:2, :] * (jnp.dot(h, w2_ref[...], precision=HIGHEST, preferred_element_type=F32) + b2_ref[...]))
    h = jnp.dot(h, w3_ref[...], precision=HIGHEST, preferred_element_type=F32)
    h = h * jnp.exp(-z[:, 0:1] * dl_ref[...])
    row = lax.broadcasted_iota(jnp.int32, (tr, W), 0) + i * tr
    o_ref[...] = jnp.where(row < L, h[:, :W], jnp.where(row == L, 0.0, h[:, W:]))


def hyena_filter(z2, w1p, b1, w2, b2, w3, freq, deltas2, L):
    n = z2.shape[0]
    W = w3.shape[1] // 2
    O = w2.shape[0]
    tr = _tile(n, 512)
    full = lambda a: pl.BlockSpec(a.shape, lambda i: (0,) * a.ndim)
    args = (w1p, b1.reshape(1, O), w2, b2.reshape(1, O), w3, freq, deltas2)
    return pl.pallas_call(
        functools.partial(_hy_filter_body, L=L),
        out_shape=jax.ShapeDtypeStruct((n, W), F32),
        grid=(n // tr,),
        in_specs=[pl.BlockSpec((tr, LANES), lambda i: (i, 0))] + [full(a) for a in args],
        out_specs=pl.BlockSpec((tr, W), lambda i: (i, 0)),
        compiler_params=_params("parallel"),
        name="hyena_filter",
    )(z2, *args)


def _hy_prep_body(v_ref, x0_ref, x1_ref, cv_ref, c0_ref, c1_ref, bv_ref, b0_ref, b1_ref,
                  vgx_ref, vge_ref, x0u_ref, *, lay):
    shape = v_ref.shape
    prev_ok, next_ok = _conv_masks(shape, lay)
    v = _centred_conv(v_ref[...], cv_ref[...], prev_ok, next_ok) + bv_ref[...]
    x0 = _centred_conv(x0_ref[...], c0_ref[...], prev_ok, next_ok) + b0_ref[...]
    x1 = _centred_conv(x1_ref[...], c1_ref[...], prev_ok, next_ok) + b1_ref[...]
    vg = v * x1
    vge_ref[...] = vg[:lay.Lc, :]
    vgx_ref[...] = vg[lay.Lc:, :]
    x0u_ref[...] = x0


def hyena_prep(px, conv_w, conv_b, lay, W, col0):
    T = px.shape[0]
    H = W // HEAD_DIM
    S, L, Lc, B = lay.S, lay.L, lay.Lc, lay.B

    def col(o):
        return pl.BlockSpec((S, HEAD_DIM), lambda b, j: (b, col0 + o * H + j))

    def cw(o, rows):
        return pl.BlockSpec((rows, HEAD_DIM), lambda b, j: (0, o * H + j))

    return pl.pallas_call(
        functools.partial(_hy_prep_body, lay=lay),
        out_shape=(jax.ShapeDtypeStruct((B * L, W), F32), jax.ShapeDtypeStruct((B * Lc, W), F32),
                   jax.ShapeDtypeStruct((T, W), F32)),
        grid=(B, H),
        in_specs=[col(0), col(1), col(2), cw(0, 3), cw(1, 3), cw(2, 3), cw(0, 1), cw(1, 1), cw(2, 1)],
        out_specs=(pl.BlockSpec((L, HEAD_DIM), lambda b, j: (b, j)),
                   pl.BlockSpec((Lc, HEAD_DIM), lambda b, j: (b, j)),
                   pl.BlockSpec((S, HEAD_DIM), lambda b, j: (b, j))),
        compiler_params=_params("parallel", "parallel"),
        name="hyena_prep",
    )(px, px, px, conv_w, conv_w, conv_w, conv_b, conv_b, conv_b)


def _dft_tables(L):
    n = 2 * L
    FT = min(256, L)
    idx = jnp.arange(n, dtype=jnp.int32)
    t = idx // (2 * FT)
    r = idx % (2 * FT)
    is_im = r >= FT
    f = t * FT + jnp.where(is_im, r - FT, r)
    nyq = is_im & (f == 0)
    f = jnp.where(nyq, L, f)
    s = jnp.arange(n, dtype=jnp.int32)
    ang = (2.0 * math.pi / n) * ((f[:, None] * s[None, :]) % n).astype(F32)
    fwd = jnp.where((is_im & ~nyq)[:, None], -jnp.sin(ang), jnp.cos(ang))
    wgt = jnp.where((f == 0) | (f == L), 1.0 / n, 2.0 / n)[:, None]
    inv = (fwd * wgt)[:, :L].T
    return fwd, inv, FT


def _hy_fwd_body(ah_ref, al_ref, u_ref, kf_ref, yh_ref, yl_ref, *, FT):
    t = pl.program_id(2)
    uh, ul = _split_bf16(u_ref[...])
    x = _dot3(ah_ref[...], al_ref[...], uh, ul)
    kf = kf_ref[...]
    xr, xi, kr, ki = x[:FT], x[FT:], kf[:FT], kf[FT:]
    row = lax.broadcasted_iota(jnp.int32, xr.shape, 0)
    packed = (row + t) == 0
    yr = jnp.where(packed, xr * kr, xr * kr - xi * ki)
    yi = jnp.where(packed, xi * ki, xr * ki + xi * kr)
    y = jnp.concatenate([yr, yi], axis=0)
    yh, yl = _split_bf16(y)
    yh_ref[...] = yh
    yl_ref[...] = yl


def _hy_spec_body(ah_ref, al_ref, u_ref, o_ref):
    uh, ul = _split_bf16(u_ref[...])
    o_ref[...] = _dot3(ah_ref[...], al_ref[...], uh, ul)


def hyena_spectrum(fh, fl, kern, FT):
    n, W = kern.shape
    tn = _tile(W, 256)
    return pl.pallas_call(
        _hy_spec_body,
        out_shape=jax.ShapeDtypeStruct((n, W), F32),
        grid=(W // tn, n // (2 * FT)),
        in_specs=[pl.BlockSpec((2 * FT, n), lambda j, t: (t, 0)),
                  pl.BlockSpec((2 * FT, n), lambda j, t: (t, 0)),
                  pl.BlockSpec((n, tn), lambda j, t: (0, j))],
        out_specs=pl.BlockSpec((2 * FT, tn), lambda j, t: (t, j)),
        compiler_params=_params("parallel", "arbitrary"),
        name="hyena_spectrum",
    )(fh, fl, kern)


def hyena_fwd(fh, fl, vg, kf, B, Lq, FT):
    W = vg.shape[1]
    n = 2 * Lq
    nt = n // (2 * FT)
    tn = _tile(W, 256)
    out = jax.ShapeDtypeStruct((B * n, W), BF16)
    ospec = pl.BlockSpec((2 * FT, tn), lambda b, j, t: (b * nt + t, j))
    return pl.pallas_call(
        functools.partial(_hy_fwd_body, FT=FT),
        out_shape=(out, out),
        grid=(B, W // tn, nt),
        in_specs=[pl.BlockSpec((2 * FT, Lq), lambda b, j, t: (t, 0)),
                  pl.BlockSpec((2 * FT, Lq), lambda b, j, t: (t, 0)),
                  pl.BlockSpec((Lq, tn), lambda b, j, t: (b, j)),
                  pl.BlockSpec((2 * FT, tn), lambda b, j, t: (t, j))],
        out_specs=(ospec, ospec),
        compiler_params=_params("parallel", "parallel", "arbitrary"),
        name="hyena_fwd",
    )(fh, fl, vg, kf)


def _hy_inv_body(gh_ref, gl_ref, yh_ref, yl_ref, vg_ref, x0_ref, bias_ref, o_ref):
    y = _dot3(gh_ref[...], gl_ref[...], yh_ref[...], yl_ref[...])
    o_ref[...] = (x0_ref[...] * (y + vg_ref[...] * bias_ref[...])).astype(BF16)


def hyena_inv(gh, gl, yh, yl, vg, x0u, bias, lay, Lq, row0):
    W = vg.shape[1]
    n = 2 * Lq
    B = lay.B
    tt = lay.gr
    ntt = Lq // tt
    tn = _tile(W, 256)
    g0 = row0 // tt
    return pl.pallas_call(
        _hy_inv_body,
        out_shape=jax.ShapeDtypeStruct((B * Lq, W), BF16),
        grid=(B, W // tn, ntt),
        in_specs=[pl.BlockSpec((tt, n), lambda b, j, i: (i, 0)),
                  pl.BlockSpec((tt, n), lambda b, j, i: (i, 0)),
                  pl.BlockSpec((n, tn), lambda b, j, i: (b, j)),
                  pl.BlockSpec((n, tn), lambda b, j, i: (b, j)),
                  pl.BlockSpec((tt, tn), lambda b, j, i: (b * ntt + i, j)),
                  pl.BlockSpec((tt, tn), lambda b, j, i: (b * lay.NG + g0 + i, j)),
                  pl.BlockSpec((1, tn), lambda b, j, i: (0, j))],
        out_specs=pl.BlockSpec((tt, tn), lambda b, j, i: (b * ntt + i, j)),
        compiler_params=_params("parallel", "parallel", "arbitrary"),
        name="hyena_inv",
    )(gh, gl, yh, yl, vg, x0u, bias.reshape(1, W))


def _hyena_features(L):
    bands = (HY_EMB - 1) // 2
    t = jnp.linspace(0.0, 1.0, L, dtype=F32)[:, None]
    wpos = 2.0 * math.pi * jnp.arange(L, dtype=F32)[:, None] / L
    fb = jnp.linspace(1e-4, bands - 1, bands, dtype=F32)[None]
    z = jnp.concatenate([t, jnp.cos(fb * wpos), -jnp.sin(fb * wpos)], axis=-1)
    z2 = jnp.concatenate([z, z[::-1]], axis=0)
    return jnp.pad(z2, ((0, 0), (0, LANES - HY_EMB)))


def hyena_stream(vg, x0u, bias, wts, lay, Lq, row0):
    w1p, b1, w2, b2, w3, freq, deltas2 = wts
    fwd, inv, FT = _dft_tables(Lq)
    fh, fl = _split_bf16(fwd)
    gh, gl = _split_bf16(inv)
    kern = hyena_filter(_hyena_features(Lq), w1p, b1, w2, b2, w3, freq, deltas2, Lq)
    kf = hyena_spectrum(fh, fl, kern, FT)
    yh, yl = hyena_fwd(fh[:, :Lq], fl[:, :Lq], vg, kf, lay.B, Lq, FT)
    return hyena_inv(gh, gl, yh, yl, vg, x0u, bias, lay, Lq, row0)


def _rope_tables(lay):
    n_freq = DIFF_DIM // 4
    inv = ROPE_THETA ** (-jnp.arange(n_freq, dtype=F32) / n_freq)
    rows = lay.L // GRID_W
    r = jnp.repeat(jnp.arange(rows, dtype=F32), GRID_W)
    col = jnp.tile(jnp.arange(GRID_W, dtype=F32), rows)
    ang = jnp.concatenate([r[:, None] * inv, col[:, None] * inv], axis=-1)
    cos, sin = jnp.cos(ang), jnp.sin(ang)
    cos = jnp.concatenate([jnp.ones((lay.Lc, DIFF_DIM // 2), F32), cos], axis=0)
    sin = jnp.concatenate([jnp.zeros((lay.Lc, DIFF_DIM // 2), F32), sin], axis=0)
    cos_t = jnp.tile(cos, (1, 4))
    sin_t = jnp.tile(jnp.concatenate([-sin, sin], axis=-1), (1, 2))
    return cos_t, sin_t


def _main_cols(W, H):
    small0 = 7 * W
    small1 = small0 + 4 * H
    return small0, small1


def mixers(px, ps, lay, W, layer, need_ctx, p):
    H = W // HEAD_DIM
    lam_init = 0.8 - 0.6 * math.exp(-0.3 * layer)
    q_rot, k_rot = attn_prep(px, p["cos_t"], p["sin_t"], p["qk_g2"], lay, W)
    att = attention(q_rot, k_rot, px, p["attn_lambda"], p["attn_subln_g"], lay, W, lam_init)
    gdn = gated_deltanet(px, ps, p["gdn_conv_w"], p["alog"], p["dtb"], p["gdn_norm_g"], lay, W, 3 * H)
    hg = hgrn2(px, p["hg_lb_logits"], p["hg_norm_g"], lay, W, 7 * H, layer)
    vgx, vge, x0u = hyena_prep(px, p["hy_conv_w"], p["hy_conv_b"], lay, W, 12 * H)
    hy_x = hyena_stream(vgx, x0u, p["hy_bias"], p["hy_wts"], lay, lay.L, lay.Lc)
    if need_ctx:
        hy_e = hyena_stream(vge, x0u, p["hy_bias"], p["hy_wts"], lay, lay.Lc, 0)
    else:
        hy_e = jnp.zeros((lay.B, lay.Lc, W), BF16)
    hy = jnp.concatenate([hy_e.reshape(lay.B, lay.Lc, W), hy_x.reshape(lay.B, lay.L, W)], axis=1)
    return att, gdn, hg, hy.reshape(lay.T, W)


def _layer_params(l, lay, W, attn_qk_g, attn_lambda, attn_subln_g, gdn_conv_w, gdn_a_log, gdn_dt_bias, gdn_norm_g,
                  hg_lb_logits, hg_norm_g, hy_conv_w, hy_conv_b, hy_w1, hy_b1, hy_w2, hy_b2, hy_w3, hy_freq,
                  hy_bias, cos_t, sin_t):
    H = W // HEAD_DIM
    pad16 = lambda a: jnp.pad(a.astype(F32).reshape(1, 2 * H), ((0, 0), (0, LANES - 2 * H)))
    deltas = jnp.abs(jnp.linspace(math.log(HY_TARGET) / HY_SLOW_DECAY, math.log(HY_TARGET) / HY_FAST_DECAY, W, dtype=F32))
    return dict(
        cos_t=cos_t, sin_t=sin_t,
        qk_g2=jnp.tile(attn_qk_g[l].astype(F32), (1, 2)),
        attn_lambda=attn_lambda[l].astype(F32), attn_subln_g=attn_subln_g[l].astype(F32),
        gdn_conv_w=gdn_conv_w[l], alog=pad16(gdn_a_log[l]), dtb=pad16(gdn_dt_bias[l]), gdn_norm_g=gdn_norm_g[l],
        hg_lb_logits=hg_lb_logits.astype(F32).reshape(hg_lb_logits.shape[0], 2 * W), hg_norm_g=hg_norm_g[l],
        hy_conv_w=hy_conv_w[l], hy_conv_b=hy_conv_b[l].reshape(1, 3 * W), hy_bias=hy_bias[l].astype(F32),
        hy_wts=(jnp.pad(hy_w1[l].astype(F32), ((0, LANES - HY_EMB), (0, 0))), hy_b1[l].astype(F32),
                hy_w2[l].astype(F32), hy_b2[l].astype(F32), hy_w3[l].astype(F32), hy_freq[l].astype(F32),
                jnp.tile(deltas, 2).reshape(1, 2 * W)),
    )


def kernel(x, c, ctx, c_ctx, norm_g, w_mod, b_mod, ffn_w_in, ffn_w_out, w_in, attn_qk_g, attn_lambda, attn_subln_g, gdn_conv_w, gdn_a_log, gdn_dt_bias, gdn_norm_g, hg_lb_logits, hg_norm_g, hy_conv_w, hy_conv_b, hy_w1, hy_b1, hy_w2, hy_b2, hy_w3, hy_freq, hy_bias, w_gate, w_up, w_out):
    B, L, D = x.shape
    Lc = ctx.shape[1]
    depth = w_mod.shape[0]
    W = D // N_BRANCH
    H = W // HEAD_DIM
    lay = Layout(B, L, Lc)
    h = jnp.concatenate([ctx, x], axis=1).reshape(lay.T, D)
    G = -(-(B + 1) // 8) * 8
    c_all = jnp.concatenate([c, c_ctx[None], jnp.zeros((G - B - 1, D), F32)], axis=0)
    cos_t, sin_t = _rope_tables(lay)
    small0, small1 = _main_cols(W, H)
    for l in range(depth):
        last = l == depth - 1
        p = _layer_params(l, lay, W, attn_qk_g, attn_lambda, attn_subln_g, gdn_conv_w, gdn_a_log, gdn_dt_bias,
                          gdn_norm_g, hg_lb_logits, hg_norm_g, hy_conv_w, hy_conv_b, hy_w1, hy_b1, hy_w2, hy_b2,
                          hy_w3, hy_freq, hy_bias, cos_t, sin_t)
        mod = mod_table(c_all, w_mod[l], b_mod[l]).reshape(G, MOD_CHUNKS, D)
        act = mm_swiglu(norm_mod(h, norm_g[l, 0], mod, lay, 0), ffn_w_in[l, 0])
        h = mm_resid(act, ffn_w_out[l, 0], h, mod, lay, 2, 0.5)
        xn = norm_mod(h, norm_g[l, 1], mod, lay, 1)
        w_main = jnp.concatenate([w_in[l][:, :small0], w_in[l][:, small1:]], axis=1)
        w_small = jnp.pad(w_in[l][:, small0:small1], ((0, 0), (0, LANES - (small1 - small0))))
        px = mm_plain(xn, w_main)
        ps = mm_plain(xn, w_small, LANES)
        branches = mixers(px, ps, lay, W, l, not last, p)
        acc = merge_branches(xn, branches, w_gate[l].astype(BF16), w_up[l].astype(BF16))
        h = mm_resid(acc, w_out[l], h, mod, lay, 5, 1.0)
        act = mm_swiglu(norm_mod(h, norm_g[l, 2], mod, lay, 2), ffn_w_in[l, 1])
        h = mm_resid(act, ffn_w_out[l, 1], h, mod, lay, 8, 0.5)
    return h.reshape(B, lay.S, D)[:, Lc:, :]
```

```python
import functools
import math

import jax
import jax.numpy as jnp
from jax import lax
from jax.experimental import pallas as pl
from jax.experimental.pallas import tpu as pltpu

F32 = jnp.float32
BF16 = jnp.bfloat16

GRID_W = 64
N_BRANCH = 4
HEAD_DIM = 128
DIFF_DIM = HEAD_DIM // 2
MOD_CHUNKS = 9
ROPE_THETA = 10000.0
CHUNK = 64
HY_EMB = 33
HY_FAST_DECAY = 0.3
HY_SLOW_DECAY = 1.5
HY_TARGET = 1e-2
EPS = 1e-6
MASK_NEG = -1e30
LB_FLOOR = 1e-20

LANES = 128
V7X_VMEM_LIMIT = 56 * 1024 * 1024
HIGHEST = lax.Precision.HIGHEST

_NT = (((1,), (1,)), ((), ()))
_TN = (((0,), (0,)), ((), ()))


def _params(*sem):
    return pltpu.CompilerParams(dimension_semantics=sem, vmem_limit_bytes=V7X_VMEM_LIMIT)


def _tile(n, pref):
    t = min(n, pref)
    while n % t:
        t //= 2
    return t


def _sigmoid(x):
    return 1.0 / (1.0 + jnp.exp(-x))


def _silu(x):
    return x * _sigmoid(x)


def _softplus(x):
    return jnp.maximum(x, 0.0) + jnp.log1p(jnp.exp(-jnp.abs(x)))


class Layout:
    def __init__(self, B, L, Lc):
        self.B, self.L, self.Lc = B, L, Lc
        self.S = L + Lc
        self.T = B * self.S
        self.gr = math.gcd(L, Lc)
        self.NG = self.S // self.gr
        self.NGc = Lc // self.gr


def _mod_row(mod_ref, lay, r, k):
    b = r // lay.NG
    g = jnp.where(r - b * lay.NG < lay.NGc, lay.B, b)
    return mod_ref[pl.ds(g, 1), k:k + 1, :][0]


def _mod_body(c_ref, w_ref, b_ref, o_ref):
    a = _silu(c_ref[...]).astype(BF16)
    o_ref[...] = jnp.dot(a, w_ref[...].astype(BF16), preferred_element_type=F32) + b_ref[...]


def mod_table(c16, w_mod, b_mod):
    R, D = c16.shape
    N = w_mod.shape[1]
    tn = _tile(N, 512)
    return pl.pallas_call(
        _mod_body,
        out_shape=jax.ShapeDtypeStruct((R, N), F32),
        grid=(N // tn,),
        in_specs=[pl.BlockSpec((R, D), lambda j: (0, 0)),
                  pl.BlockSpec((D, tn), lambda j: (0, j)),
                  pl.BlockSpec((1, tn), lambda j: (0, j))],
        out_specs=pl.BlockSpec((R, tn), lambda j: (0, j)),
        compiler_params=_params("parallel"),
        name="mod_table",
    )(c16, w_mod, b_mod.reshape(1, N))


def _norm_mod_body(x_ref, g_ref, mod_ref, o_ref, *, lay, k):
    i = pl.program_id(0)
    gr = lay.gr
    n_sub = x_ref.shape[0] // gr
    for s in range(n_sub):
        x = x_ref[s * gr:(s + 1) * gr, :]
        y = x * lax.rsqrt(jnp.mean(x * x, axis=-1, keepdims=True) + EPS) * g_ref[...]
        r = i * n_sub + s
        shift = _mod_row(mod_ref, lay, r, 3 * k)
        scale = _mod_row(mod_ref, lay, r, 3 * k + 1)
        o_ref[s * gr:(s + 1) * gr, :] = (y * (1.0 + scale) + shift).astype(BF16)


def norm_mod(h, g, mod, lay, k):
    T, D = h.shape
    tm = lay.gr
    G = mod.shape[0]
    return pl.pallas_call(
        functools.partial(_norm_mod_body, lay=lay, k=k),
        out_shape=jax.ShapeDtypeStruct((T, D), BF16),
        grid=(T // tm,),
        in_specs=[pl.BlockSpec((tm, D), lambda i: (i, 0)),
                  pl.BlockSpec((1, D), lambda i: (0, 0)),
                  pl.BlockSpec((G, MOD_CHUNKS, D), lambda i: (0, 0, 0))],
        out_specs=pl.BlockSpec((tm, D), lambda i: (i, 0)),
        compiler_params=_params("parallel"),
        name="norm_mod",
    )(h, g.reshape(1, D), mod)


def _mm_swiglu_body(a_ref, wg_ref, wu_ref, o_ref, wgb, wub):
    @pl.when(pl.program_id(1) == 0)
    def _():
        wgb[...] = wg_ref[...].astype(BF16)
        wub[...] = wu_ref[...].astype(BF16)

    a = a_ref[...]
    g = jnp.dot(a, wgb[...], preferred_element_type=F32)
    u = jnp.dot(a, wub[...], preferred_element_type=F32)
    o_ref[...] = (_silu(g) * u).astype(BF16)


def mm_swiglu(a, w):
    T, K = a.shape
    F = w.shape[1] // 2
    tm, tn = _tile(T, 1024), _tile(F, 256)
    nj = F // tn
    return pl.pallas_call(
        _mm_swiglu_body,
        out_shape=jax.ShapeDtypeStruct((T, F), BF16),
        grid=(nj, T // tm),
        in_specs=[pl.BlockSpec((tm, K), lambda j, i: (i, 0)),
                  pl.BlockSpec((K, tn), lambda j, i: (0, j)),
                  pl.BlockSpec((K, tn), lambda j, i: (0, j + nj))],
        out_specs=pl.BlockSpec((tm, tn), lambda j, i: (i, j)),
        scratch_shapes=[pltpu.VMEM((K, tn), BF16), pltpu.VMEM((K, tn), BF16)],
        compiler_params=_params("parallel", "arbitrary"),
        name="mm_swiglu",
    )(a, w, w)


def _mm_resid_body(a_ref, w_ref, h_ref, mod_ref, o_ref, wb, *, lay, k, scale):
    i = pl.program_id(1)

    @pl.when(i == 0)
    def _():
        wb[...] = w_ref[...].astype(BF16)

    y = jnp.dot(a_ref[...], wb[...], preferred_element_type=F32)
    gr = lay.gr
    n_sub = a_ref.shape[0] // gr
    for s in range(n_sub):
        gate = _mod_row(mod_ref, lay, i * n_sub + s, k)
        sl = slice(s * gr, (s + 1) * gr)
        o_ref[sl, :] = h_ref[sl, :] + (scale * gate) * y[sl, :]


def mm_resid(a, w, h, mod, lay, k, scale):
    T, K = a.shape
    N = w.shape[1]
    G = mod.shape[0]
    tm, tn = _tile(T, 512), _tile(N, 512)
    tm = max(tm, lay.gr)
    return pl.pallas_call(
        functools.partial(_mm_resid_body, lay=lay, k=k, scale=scale),
        out_shape=jax.ShapeDtypeStruct((T, N), F32),
        grid=(N // tn, T // tm),
        in_specs=[pl.BlockSpec((tm, K), lambda j, i: (i, 0)),
                  pl.BlockSpec((K, tn), lambda j, i: (0, j)),
                  pl.BlockSpec((tm, tn), lambda j, i: (i, j)),
                  pl.BlockSpec((G, MOD_CHUNKS, tn), lambda j, i: (0, 0, j))],
        out_specs=pl.BlockSpec((tm, tn), lambda j, i: (i, j)),
        scratch_shapes=[pltpu.VMEM((K, tn), BF16)],
        compiler_params=_params("parallel", "arbitrary"),
        name="mm_resid",
    )(a, w, h, mod)


def _mm_plain_body(a_ref, w_ref, o_ref, wb):
    @pl.when(pl.program_id(1) == 0)
    def _():
        wb[...] = w_ref[...].astype(BF16)

    o_ref[...] = jnp.dot(a_ref[...], wb[...], preferred_element_type=F32)


def mm_plain(a, w, tn_pref=512):
    T, K = a.shape
    N = w.shape[1]
    tm, tn = _tile(T, 1024), _tile(N, tn_pref)
    return pl.pallas_call(
        _mm_plain_body,
        out_shape=jax.ShapeDtypeStruct((T, N), F32),
        grid=(N // tn, T // tm),
        in_specs=[pl.BlockSpec((tm, K), lambda j, i: (i, 0)),
                  pl.BlockSpec((K, tn), lambda j, i: (0, j))],
        out_specs=pl.BlockSpec((tm, tn), lambda j, i: (i, j)),
        scratch_shapes=[pltpu.VMEM((K, tn), BF16)],
        compiler_params=_params("parallel", "arbitrary"),
        name="mm_plain",
    )(a, w)


def _merge_body(xn_ref, o0_ref, o1_ref, o2_ref, o3_ref, wg_ref, wu_ref, out_ref):
    xn = xn_ref[...]
    acc = None
    for br, o_ref in enumerate((o0_ref, o1_ref, o2_ref, o3_ref)):
        g = jnp.dot(xn, wg_ref[br], preferred_element_type=F32)
        u = jnp.dot(o_ref[...], wu_ref[br], preferred_element_type=F32)
        term = _sigmoid(g) * u
        acc = term if acc is None else acc + term
    out_ref[...] = acc.astype(BF16)


def merge_branches(xn, branches, w_gate_b, w_up_b):
    T, D = xn.shape
    W = branches[0].shape[1]
    tm, tn = _tile(T, 512), _tile(D, 256)
    bspec = pl.BlockSpec((tm, W), lambda i, j: (i, 0))
    return pl.pallas_call(
        _merge_body,
        out_shape=jax.ShapeDtypeStruct((T, D), BF16),
        grid=(T // tm, D // tn),
        in_specs=[pl.BlockSpec((tm, D), lambda i, j: (i, 0)), bspec, bspec, bspec, bspec,
                  pl.BlockSpec((N_BRANCH, D, tn), lambda i, j: (0, 0, j)),
                  pl.BlockSpec((N_BRANCH, W, tn), lambda i, j: (0, 0, j))],
        out_specs=pl.BlockSpec((tm, tn), lambda i, j: (i, j)),
        compiler_params=_params("parallel", "arbitrary"),
        name="merge_branches",
    )(xn, *branches, w_gate_b, w_up_b)


def _attn_prep_body(q_ref, k_ref, cos_ref, sin_ref, g_ref, qo_ref, ko_ref):
    cos = cos_ref[...]
    sin = sin_ref[...]
    lane = lax.broadcasted_iota(jnp.int32, cos.shape, 1)
    lo = lane < DIFF_DIM
    first = (lane % DIFF_DIM) < DIFF_DIM // 2
    n_heads = q_ref.shape[1] // HEAD_DIM
    for src, dst, gi, sc in ((q_ref, qo_ref, 0, DIFF_DIM ** -0.5), (k_ref, ko_ref, 1, 1.0)):
        g = g_ref[gi:gi + 1, :]
        for h in range(n_heads):
            cs = slice(h * HEAD_DIM, (h + 1) * HEAD_DIM)
            t = src[:, cs]
            ss = t * t
            s_lo = jnp.sum(jnp.where(lo, ss, 0.0), axis=-1, keepdims=True)
            s_hi = jnp.sum(jnp.where(lo, 0.0, ss), axis=-1, keepdims=True)
            inv = jnp.where(lo, lax.rsqrt(s_lo / DIFF_DIM + EPS), lax.rsqrt(s_hi / DIFF_DIM + EPS))
            y = t * inv * g
            partner = jnp.where(first, pltpu.roll(y, HEAD_DIM - DIFF_DIM // 2, 1), pltpu.roll(y, DIFF_DIM // 2, 1))
            dst[:, cs] = ((y * cos + partner * sin) * sc).astype(BF16)


def attn_prep(px, cos_t, sin_t, qk_g2, lay, W):
    T = px.shape[0]
    tr = lay.gr
    return pl.pallas_call(
        _attn_prep_body,
        out_shape=(jax.ShapeDtypeStruct((T, W), BF16), jax.ShapeDtypeStruct((T, W), BF16)),
        grid=(T // tr,),
        in_specs=[pl.BlockSpec((tr, W), lambda i: (i, 0)),
                  pl.BlockSpec((tr, W), lambda i: (i, 1)),
                  pl.BlockSpec((tr, LANES), lambda i: (i % lay.NG, 0)),
                  pl.BlockSpec((tr, LANES), lambda i: (i % lay.NG, 0)),
                  pl.BlockSpec((2, LANES), lambda i: (0, 0))],
        out_specs=(pl.BlockSpec((tr, W), lambda i: (i, 0)), pl.BlockSpec((tr, W), lambda i: (i, 0))),
        compiler_params=_params("parallel"),
        name="attn_prep",
    )(px, px, cos_t, sin_t, qk_g2)


def _attn_body(q_ref, k_ref, v_ref, lam_ref, g_ref, o_ref, *, lay, lam_init):
    qi = pl.program_id(2)
    lp = lam_ref[...]
    lam = (jnp.exp(jnp.sum(lp[0:1] * lp[1:2], axis=-1, keepdims=True))
           - jnp.exp(jnp.sum(lp[2:3] * lp[3:4], axis=-1, keepdims=True)) + lam_init)

    def run(nk):
        q = q_ref[...]
        k = k_ref[0:nk, :]
        v = v_ref[0:nk, :].astype(BF16)
        lane = lax.broadcasted_iota(jnp.int32, q.shape, 1)
        zero = jnp.zeros_like(q)
        outs = []
        for qq in (jnp.where(lane < DIFF_DIM, q, zero), jnp.where(lane < DIFF_DIM, zero, q)):
            s = lax.dot_general(qq, k, _NT, preferred_element_type=F32)
            p = jnp.exp(s - jnp.max(s, axis=-1, keepdims=True))
            l = jnp.sum(p, axis=-1, keepdims=True)
            outs.append(jnp.dot(p.astype(BF16), v, preferred_element_type=F32) / l)
        o = outs[0] - lam * outs[1]
        y = o * lax.rsqrt(jnp.mean(o * o, axis=-1, keepdims=True) + EPS) * g_ref[...]
        o_ref[...] = (y * (1.0 - lam_init)).astype(BF16)

    @pl.when(qi < lay.NGc)
    def _():
        run(lay.Lc)

    @pl.when(qi >= lay.NGc)
    def _():
        run(lay.S)


def attention(q_rot, k_rot, px, lam_p, subln_g, lay, W, lam_init):
    T = px.shape[0]
    H = W // HEAD_DIM
    tq = lay.gr
    S = lay.S
    vb = W // HEAD_DIM * 2
    return pl.pallas_call(
        functools.partial(_attn_body, lay=lay, lam_init=lam_init),
        out_shape=jax.ShapeDtypeStruct((T, W), BF16),
        grid=(lay.B, H, lay.NG),
        in_specs=[pl.BlockSpec((tq, HEAD_DIM), lambda b, h, i: (b * lay.NG + i, h)),
                  pl.BlockSpec((S, HEAD_DIM), lambda b, h, i: (b, h)),
                  pl.BlockSpec((S, HEAD_DIM), lambda b, h, i: (b, vb + h)),
                  pl.BlockSpec((4, DIFF_DIM), lambda b, h, i: (0, 0)),
                  pl.BlockSpec((1, HEAD_DIM), lambda b, h, i: (0, 0))],
        out_specs=pl.BlockSpec((tq, HEAD_DIM), lambda b, h, i: (b * lay.NG + i, h)),
        compiler_params=_params("parallel", "parallel", "arbitrary"),
        name="diff_attention",
    )(q_rot, k_rot, px, lam_p, subln_g.reshape(1, HEAD_DIM))


def _chunk_scan(y, reverse):
    S = y.shape[0]
    pos = lax.broadcasted_iota(jnp.int32, y.shape, 0) % CHUNK
    sh = 1
    while sh < CHUNK:
        if reverse:
            y = y + jnp.where(pos < CHUNK - sh, pltpu.roll(y, S - sh, 0), 0.0)
        else:
            y = y + jnp.where(pos >= sh, pltpu.roll(y, sh, 0), 0.0)
        sh *= 2
    return y


def _chunk_of_step(n, d, lay):
    ncc = lay.Lc // CHUNK
    nc = lay.S // CHUNK
    if d == 0:
        return n
    return jnp.where(n < ncc, ncc - 1 - n, nc - 1 - (n - ncc))


def _gated_head_norm(o, z, g):
    y = o * lax.rsqrt(jnp.mean(o * o, axis=-1, keepdims=True) + EPS) * g
    return (y * _silu(z)).astype(BF16)


def _centred_conv(x, w, prev_ok, next_ok):
    S = x.shape[0]
    xp = jnp.where(prev_ok, pltpu.roll(x, 1, 0), 0.0)
    xn = jnp.where(next_ok, pltpu.roll(x, S - 1, 0), 0.0)
    return xp * w[0:1] + x * w[1:2] + xn * w[2:3]


def _conv_masks(shape, lay):
    row = lax.broadcasted_iota(jnp.int32, shape, 0)
    prev_ok = (row * (row - lay.Lc)) != 0
    next_ok = ((row - (lay.Lc - 1)) * (row - (lay.S - 1))) != 0
    return prev_ok, next_ok


def _gdn_body(q_ref, k_ref, v_ref, z_ref, ps_ref, cq_ref, ck_ref, cv_ref, alog_ref, dtb_ref, g_ref, o_ref,
              qs, ks, vs, gc, bb, p_s, r_s, k_s, n_s, dl_s, o_s, *, lay, n_heads, group):
    h = pl.program_id(1)
    S = lay.S
    nc = S // CHUNK
    shape = (S, HEAD_DIM)
    prev_ok, next_ok = _conv_masks(shape, lay)
    lane = lax.broadcasted_iota(jnp.int32, shape, 1)

    q = _silu(_centred_conv(q_ref[...], cq_ref[...], prev_ok, next_ok))
    qs[...] = q * lax.rsqrt(jnp.sum(q * q, axis=-1, keepdims=True) + EPS) * HEAD_DIM ** -0.5
    k = _silu(_centred_conv(k_ref[...], ck_ref[...], prev_ok, next_ok))
    ks[...] = k * lax.rsqrt(jnp.sum(k * k, axis=-1, keepdims=True) + EPS)
    vs[...] = _silu(_centred_conv(v_ref[...], cv_ref[...], prev_ok, next_ok))

    p = ps_ref[...]
    g_all = -jnp.exp(alog_ref[...]) * _softplus(p + dtb_ref[...])
    beta_all = _sigmoid(p)
    for d in range(2):
        gsel = jnp.sum(jnp.where(lane == d * n_heads + h, g_all, 0.0), axis=-1, keepdims=True)
        gc[d] = _chunk_scan(jnp.broadcast_to(gsel, shape), reverse=(d == 1))
        bsel = jnp.sum(jnp.where(lane == (2 + d) * n_heads + h, beta_all, 0.0), axis=-1, keepdims=True)
        bb[d] = jnp.broadcast_to(bsel, shape)

    G = group
    ii = lax.broadcasted_iota(jnp.int32, (G, CHUNK, CHUNK), 1)
    jj = lax.broadcasted_iota(jnp.int32, (G, CHUNK, CHUNK), 2)
    eye = jnp.where(ii == jj, 1.0, 0.0)
    n_sq = int(math.log2(CHUNK)) - 1

    def bmm(x, y):
        return jnp.einsum('gij,gjk->gik', x, y, preferred_element_type=F32)

    def bmm_nt(x, y):
        return jnp.einsum('gik,gjk->gij', x, y, preferred_element_type=F32)

    def bmm_tn(x, y):
        return jnp.einsum('gjk,gjl->gkl', x, y, preferred_element_type=F32)

    def prep(gi, d):
        rows = G * CHUNK
        sl = pl.ds(pl.multiple_of(gi * rows, rows), rows)
        cs = pl.ds(gi * G, G)
        r3 = lambda x: x.reshape(G, CHUNK, HEAD_DIM)
        q3, k3, v3, gcb, b3 = r3(qs[sl, :]), r3(ks[sl, :]), r3(vs[sl, :]), r3(gc[d, sl, :]), r3(bb[d, sl, :])
        glast = gcb[:, CHUNK - 1:CHUNK, :] if d == 0 else gcb[:, 0:1, :]
        eg = jnp.exp(gcb)
        kb = k3 * b3
        diff = gcb[:, :, :CHUNK] - jnp.swapaxes(gcb, 1, 2)[:, :CHUNK, :]
        incl = (ii >= jj) if d == 0 else (ii <= jj)
        strict = (ii > jj) if d == 0 else (ii < jj)
        decay = jnp.exp(jnp.where(incl, diff, MASK_NEG))
        kbf = k3.astype(BF16)
        a = jnp.where(strict, bmm_nt(kb.astype(BF16), kbf) * decay, 0.0)
        m = -a
        t = eye + m
        for _ in range(n_sq):
            mb = m.astype(BF16)
            m = bmm(mb, mb)
            t = t + bmm(t.astype(BF16), m.astype(BF16))
        tb = t.astype(BF16)
        ub = bmm(tb, (v3 * b3).astype(BF16)).astype(BF16)
        wb = bmm(tb, (kb * eg).astype(BF16)).astype(BF16)
        attb = (bmm_nt(q3.astype(BF16), kbf) * decay).astype(BF16)
        kgb = (k3 * jnp.exp(glast - gcb)).astype(BF16)
        p_s[d, sl, :] = (q3 * eg - bmm(attb, wb)).astype(BF16).reshape(rows, HEAD_DIM)
        r_s[d, sl, :] = bmm(attb, ub).reshape(rows, HEAD_DIM)
        k_s[d, cs, :, :] = bmm_tn(kgb, wb).astype(BF16)
        n_s[d, cs, :, :] = bmm_tn(kgb, ub)
        dl_s[d, cs, :, :] = jnp.broadcast_to(jnp.exp(glast), (G, 8, HEAD_DIM))

    def prep_loop(gi, carry):
        prep(gi, 0)
        prep(gi, 1)
        return carry

    lax.fori_loop(0, nc // G, prep_loop, 0)

    def step_d(d, n, s):
        c = _chunk_of_step(n, d, lay)
        sl = pl.ds(pl.multiple_of(c * CHUNK, CHUNK), CHUNK)
        sb = s.astype(BF16)
        o_s[d, sl, :] = r_s[d, sl, :] + jnp.dot(p_s[d, sl, :], sb, preferred_element_type=F32)
        dl = dl_s[d, pl.ds(c, 1), :, :][0][0:1, :]
        return (s * dl + n_s[d, pl.ds(c, 1), :, :][0]
                - jnp.dot(k_s[d, pl.ds(c, 1), :, :][0], sb, preferred_element_type=F32))

    def both(n, carry):
        return (step_d(0, n, carry[0]), step_d(1, n, carry[1]))

    zero = jnp.zeros((HEAD_DIM, HEAD_DIM), F32)
    lax.fori_loop(0, nc, both, (zero, zero))
    o_ref[...] = _gated_head_norm(o_s[0] + o_s[1], z_ref[...], g_ref[...])


def gated_deltanet(px, ps, conv_w, alog_l, dtb_l, norm_g, lay, W, col0):
    T = px.shape[0]
    H = W // HEAD_DIM
    S = lay.S
    nc = S // CHUNK

    def col(o):
        return pl.BlockSpec((S, HEAD_DIM), lambda b, h: (b, col0 + o * H + h))

    def cw(o):
        return pl.BlockSpec((3, HEAD_DIM), lambda b, h: (0, o * H + h))

    one = pl.BlockSpec((1, HEAD_DIM), lambda b, h: (0, 0))
    sh = (S, HEAD_DIM)
    st = (2, nc, HEAD_DIM, HEAD_DIM)
    group = max(g for g in range(1, 10) if nc % g == 0)
    return pl.pallas_call(
        functools.partial(_gdn_body, lay=lay, n_heads=H, group=group),
        out_shape=jax.ShapeDtypeStruct((T, W), BF16),
        grid=(lay.B, H),
        in_specs=[col(0), col(1), col(2), col(3),
                  pl.BlockSpec((S, LANES), lambda b, h: (b, 0)),
                  cw(0), cw(1), cw(2), one, one, one],
        out_specs=pl.BlockSpec((S, HEAD_DIM), lambda b, h: (b, h)),
        scratch_shapes=[pltpu.VMEM(sh, F32), pltpu.VMEM(sh, F32), pltpu.VMEM(sh, F32),
                        pltpu.VMEM((2,) + sh, F32), pltpu.VMEM((2,) + sh, F32),
                        pltpu.VMEM((2,) + sh, BF16), pltpu.VMEM((2,) + sh, F32),
                        pltpu.VMEM(st, BF16), pltpu.VMEM(st, F32),
                        pltpu.VMEM((2, nc, 8, HEAD_DIM), F32), pltpu.VMEM((2,) + sh, F32)],
        compiler_params=_params("parallel", "arbitrary"),
        name="gated_deltanet",
    )(px, px, px, px, ps, conv_w, conv_w, conv_w, alog_l, dtb_l, norm_g.reshape(1, HEAD_DIM))


def _hgrn_body(q_ref, f0_ref, f1_ref, i_ref, z_ref, lb0_ref, lb1_ref, g_ref, o_ref,
               gc, kk, o_s, *, lay, layer):
    S = lay.S
    nc = S // CHUNK
    for d, (f_ref, lb_ref) in enumerate(((f0_ref, lb0_ref), (f1_ref, lb1_ref))):
        lg = lb_ref[...]
        depth = lg.shape[0]
        mx = lg[0:1]
        for r in range(1, depth):
            mx = jnp.maximum(mx, lg[r:r + 1])
        e = [jnp.exp(lg[r:r + 1] - mx) for r in range(depth)]
        tot = e[0]
        for r in range(1, depth):
            tot = tot + e[r]
        lb = jnp.zeros_like(mx)
        for r in range(1, layer + 1):
            lb = lb + e[r] / tot
        f = f_ref[...]
        log_sig = jnp.minimum(f, 0.0) - jnp.log1p(jnp.exp(-jnp.abs(f)))
        a = jnp.log(jnp.maximum(lb, LB_FLOOR))
        b = jnp.log1p(-lb) + log_sig
        logf = jnp.maximum(a, b) + jnp.log1p(jnp.exp(-jnp.abs(a - b)))
        gc[d] = _chunk_scan(logf, reverse=(d == 1))
        kk[d] = (1.0 - lb) * _sigmoid(-f)

    SB = 16
    nsb = CHUNK // SB
    jrow = lax.broadcasted_iota(jnp.int32, (CHUNK, HEAD_DIM), 0)
    jsub = lax.broadcasted_iota(jnp.int32, (SB, HEAD_DIM), 0)
    alane = lax.broadcasted_iota(jnp.int32, (SB, CHUNK), 1)

    def intra_t(d, qc, kc, g):
        if d == 0:
            refs = [g[0:1, :]] + [g[SB * I - 1:SB * I, :] for I in range(1, nsb)]
            far = range(1, nsb)
        else:
            refs = [g[SB * (I + 1):SB * (I + 1) + 1, :] for I in range(nsb - 1)] + [g[CHUNK - 1:CHUNK, :]]
            far = range(nsb - 1)
        rvec = jnp.concatenate([jnp.broadcast_to(r, (SB, HEAD_DIM)) for r in refs], axis=0)
        qt = qc * jnp.exp(g - rvec)
        off = None
        for I in far:
            seen = (jrow < SB * I) if d == 0 else (jrow >= SB * (I + 1))
            kt = (kc * jnp.exp(jnp.where(seen, refs[I] - g, MASK_NEG))).astype(BF16)
            qi = jnp.where((jrow // SB) == I, qt, 0.0).astype(BF16)
            part = lax.dot_general(kt, qi, _NT, preferred_element_type=F32)
            off = part if off is None else off + part
        blocks = []
        for I in range(nsb):
            rs = slice(SB * I, SB * (I + 1))
            g_i, k_i = g[rs, :], kc[rs, :]
            acc = jnp.zeros((SB, CHUNK), F32)
            for r in range(SB):
                i = SB * I + r
                keep = (jsub <= r) if d == 0 else (jsub >= r)
                rel = jnp.exp(jnp.where(keep, g[i:i + 1, :] - g_i, MASK_NEG))
                colv = jnp.sum(rel * k_i * qc[i:i + 1, :], axis=-1, keepdims=True)
                acc = jnp.where(alane == i, colv, acc)
            blocks.append(acc)
        return jnp.concatenate(blocks, axis=0) + off

    def step_d(d, n, st):
        c = _chunk_of_step(n, d, lay)
        sl = pl.ds(pl.multiple_of(c * CHUNK, CHUNK), CHUNK)
        qc, kc, vc, g = q_ref[sl, :], kk[d, sl, :], i_ref[sl, :], gc[d, sl, :]
        glast = g[CHUNK - 1:CHUNK, :] if d == 0 else g[0:1, :]
        a_t = intra_t(d, qc, kc, g)
        vb = vc.astype(BF16)
        o = (lax.dot_general((qc * jnp.exp(g)).astype(BF16), st.astype(BF16), _NT, preferred_element_type=F32)
             + lax.dot_general(a_t.astype(BF16), vb, _TN, preferred_element_type=F32))
        o_s[d, sl, :] = o
        kg = (kc * jnp.exp(glast - g)).astype(BF16)
        return st * jnp.exp(glast) + lax.dot_general(vb, kg, _TN, preferred_element_type=F32)

    def both(n, carry):
        return (step_d(0, n, carry[0]), step_d(1, n, carry[1]))

    zero = jnp.zeros((HEAD_DIM, HEAD_DIM), F32)
    lax.fori_loop(0, nc, both, (zero, zero))
    o_ref[...] = _gated_head_norm(o_s[0] + o_s[1], z_ref[...], g_ref[...])


def hgrn2(px, lb_logits, norm_g, lay, W, col0, layer):
    T = px.shape[0]
    H = W // HEAD_DIM
    S = lay.S
    depth = lb_logits.shape[0]

    def col(o):
        return pl.BlockSpec((S, HEAD_DIM), lambda b, h: (b, col0 + o * H + h))

    def lbs(d):
        return pl.BlockSpec((depth, HEAD_DIM), lambda b, h: (0, d * H + h))

    sh = (S, HEAD_DIM)
    return pl.pallas_call(
        functools.partial(_hgrn_body, lay=lay, layer=layer),
        out_shape=jax.ShapeDtypeStruct((T, W), BF16),
        grid=(lay.B, H),
        in_specs=[col(0), col(1), col(2), col(3), col(4), lbs(0), lbs(1),
                  pl.BlockSpec((1, HEAD_DIM), lambda b, h: (0, 0))],
        out_specs=pl.BlockSpec((S, HEAD_DIM), lambda b, h: (b, h)),
        scratch_shapes=[pltpu.VMEM((2,) + sh, F32), pltpu.VMEM((2,) + sh, F32), pltpu.VMEM((2,) + sh, F32)],
        compiler_params=_params("parallel", "arbitrary"),
        name="hgrn2",
    )(px, px, px, px, px, lb_logits, lb_logits, norm_g.reshape(1, HEAD_DIM))


def _split_bf16(x):
    hi = x.astype(BF16)
    return hi, (x - hi.astype(F32)).astype(BF16)


def _dot3(a_hi, a_lo, b_hi, b_lo):
    return (jnp.dot(a_hi, b_hi, preferred_element_type=F32)
            + jnp.dot(a_hi, b_lo, preferred_element_type=F32)
            + jnp.dot(a_lo, b_hi, preferred_element_type=F32))


def _hy_filter_body(z_ref, w1_ref, b1_ref, w2_ref, b2_ref, w3_ref, fr_ref, dl_ref, o_ref, *, L):
    i = pl.program_id(0)
    z = z_ref[...]
    tr = z.shape[0]
    W = o_ref.shape[1]
    h = jnp.sin(fr_ref[0:1, :] * (jnp.dot(z, w1_ref[...], precision=HIGHEST, preferred_element_type=F32) + b1_ref[...]))
    h = jnp.sin(fr_ref[1:2, :] * (jnp.dot(h, w2_ref[...], precision=HIGHEST, preferred_element_type=F32) + b2_ref[...]))
    h = jnp.dot(h, w3_ref[...], precision=HIGHEST, preferred_element_type=F32)
    h = h * jnp.exp(-z[:, 0:1] * dl_ref[...])
    row = lax.broadcasted_iota(jnp.int32, (tr, W), 0) + i * tr
    o_ref[...] = jnp.where(row < L, h[:, :W], jnp.where(row == L, 0.0, h[:, W:]))


def hyena_filter(z2, w1p, b1, w2, b2, w3, freq, deltas2, L):
    n = z2.shape[0]
    W = w3.shape[1] // 2
    O = w2.shape[0]
    tr = _tile(n, 512)
    full = lambda a: pl.BlockSpec(a.shape, lambda i: (0,) * a.ndim)
    args = (w1p, b1.reshape(1, O), w2, b2.reshape(1, O), w3, freq, deltas2)
    return pl.pallas_call(
        functools.partial(_hy_filter_body, L=L),
        out_shape=jax.ShapeDtypeStruct((n, W), F32),
        grid=(n // tr,),
        in_specs=[pl.BlockSpec((tr, LANES), lambda i: (i, 0))] + [full(a) for a in args],
        out_specs=pl.BlockSpec((tr, W), lambda i: (i, 0)),
        compiler_params=_params("parallel"),
        name="hyena_filter",
    )(z2, *args)


def _hy_prep_body(v_ref, x0_ref, x1_ref, cv_ref, c0_ref, c1_ref, bv_ref, b0_ref, b1_ref,
                  vgx_ref, vge_ref, x0u_ref, *, lay):
    shape = v_ref.shape
    prev_ok, next_ok = _conv_masks(shape, lay)
    v = _centred_conv(v_ref[...], cv_ref[...], prev_ok, next_ok) + bv_ref[...]
    x0 = _centred_conv(x0_ref[...], c0_ref[...], prev_ok, next_ok) + b0_ref[...]
    x1 = _centred_conv(x1_ref[...], c1_ref[...], prev_ok, next_ok) + b1_ref[...]
    vg = v * x1
    vge_ref[...] = vg[:lay.Lc, :]
    vgx_ref[...] = vg[lay.Lc:, :]
    x0u_ref[...] = x0


def hyena_prep(px, conv_w, conv_b, lay, W, col0):
    T = px.shape[0]
    H = W // HEAD_DIM
    S, L, Lc, B = lay.S, lay.L, lay.Lc, lay.B

    def col(o):
        return pl.BlockSpec((S, HEAD_DIM), lambda b, j: (b, col0 + o * H + j))

    def cw(o, rows):
        return pl.BlockSpec((rows, HEAD_DIM), lambda b, j: (0, o * H + j))

    return pl.pallas_call(
        functools.partial(_hy_prep_body, lay=lay),
        out_shape=(jax.ShapeDtypeStruct((B * L, W), F32), jax.ShapeDtypeStruct((B * Lc, W), F32),
                   jax.ShapeDtypeStruct((T, W), F32)),
        grid=(B, H),
        in_specs=[col(0), col(1), col(2), cw(0, 3), cw(1, 3), cw(2, 3), cw(0, 1), cw(1, 1), cw(2, 1)],
        out_specs=(pl.BlockSpec((L, HEAD_DIM), lambda b, j: (b, j)),
                   pl.BlockSpec((Lc, HEAD_DIM), lambda b, j: (b, j)),
                   pl.BlockSpec((S, HEAD_DIM), lambda b, j: (b, j))),
        compiler_params=_params("parallel", "parallel"),
        name="hyena_prep",
    )(px, px, px, conv_w, conv_w, conv_w, conv_b, conv_b, conv_b)


def _dft_tables(L):
    n = 2 * L
    FT = min(256, L)
    idx = jnp.arange(n, dtype=jnp.int32)
    t = idx // (2 * FT)
    r = idx % (2 * FT)
    is_im = r >= FT
    f = t * FT + jnp.where(is_im, r - FT, r)
    nyq = is_im & (f == 0)
    f = jnp.where(nyq, L, f)
    s = jnp.arange(n, dtype=jnp.int32)
    ang = (2.0 * math.pi / n) * ((f[:, None] * s[None, :]) % n).astype(F32)
    fwd = jnp.where((is_im & ~nyq)[:, None], -jnp.sin(ang), jnp.cos(ang))
    wgt = jnp.where((f == 0) | (f == L), 1.0 / n, 2.0 / n)[:, None]
    inv = (fwd * wgt)[:, :L].T
    return fwd, inv, FT


def _hy_fwd_body(ah_ref, al_ref, u_ref, kf_ref, yh_ref, yl_ref, *, FT):
    t = pl.program_id(2)
    uh, ul = _split_bf16(u_ref[...])
    x = _dot3(ah_ref[...], al_ref[...], uh, ul)
    kf = kf_ref[...]
    xr, xi, kr, ki = x[:FT], x[FT:], kf[:FT], kf[FT:]
    row = lax.broadcasted_iota(jnp.int32, xr.shape, 0)
    packed = (row + t) == 0
    yr = jnp.where(packed, xr * kr, xr * kr - xi * ki)
    yi = jnp.where(packed, xi * ki, xr * ki + xi * kr)
    y = jnp.concatenate([yr, yi], axis=0)
    yh, yl = _split_bf16(y)
    yh_ref[...] = yh
    yl_ref[...] = yl


def _hy_spec_body(ah_ref, al_ref, u_ref, o_ref):
    uh, ul = _split_bf16(u_ref[...])
    o_ref[...] = _dot3(ah_ref[...], al_ref[...], uh, ul)


def hyena_spectrum(fh, fl, kern, FT):
    n, W = kern.shape
    tn = _tile(W, 256)
    return pl.pallas_call(
        _hy_spec_body,
        out_shape=jax.ShapeDtypeStruct((n, W), F32),
        grid=(W // tn, n // (2 * FT)),
        in_specs=[pl.BlockSpec((2 * FT, n), lambda j, t: (t, 0)),
                  pl.BlockSpec((2 * FT, n), lambda j, t: (t, 0)),
                  pl.BlockSpec((n, tn), lambda j, t: (0, j))],
        out_specs=pl.BlockSpec((2 * FT, tn), lambda j, t: (t, j)),
        compiler_params=_params("parallel", "arbitrary"),
        name="hyena_spectrum",
    )(fh, fl, kern)


def hyena_fwd(fh, fl, vg, kf, B, Lq, FT):
    W = vg.shape[1]
    n = 2 * Lq
    nt = n // (2 * FT)
    tn = _tile(W, 256)
    out = jax.ShapeDtypeStruct((B * n, W), BF16)
    ospec = pl.BlockSpec((2 * FT, tn), lambda b, j, t: (b * nt + t, j))
    return pl.pallas_call(
        functools.partial(_hy_fwd_body, FT=FT),
        out_shape=(out, out),
        grid=(B, W // tn, nt),
        in_specs=[pl.BlockSpec((2 * FT, Lq), lambda b, j, t: (t, 0)),
                  pl.BlockSpec((2 * FT, Lq), lambda b, j, t: (t, 0)),
                  pl.BlockSpec((Lq, tn), lambda b, j, t: (b, j)),
                  pl.BlockSpec((2 * FT, tn), lambda b, j, t: (t, j))],
        out_specs=(ospec, ospec),
        compiler_params=_params("parallel", "parallel", "arbitrary"),
        name="hyena_fwd",
    )(fh, fl, vg, kf)


def _hy_inv_body(gh_ref, gl_ref, yh_ref, yl_ref, vg_ref, x0_ref, bias_ref, o_ref):
    y = _dot3(gh_ref[...], gl_ref[...], yh_ref[...], yl_ref[...])
    o_ref[...] = (x0_ref[...] * (y + vg_ref[...] * bias_ref[...])).astype(BF16)


def hyena_inv(gh, gl, yh, yl, vg, x0u, bias, lay, Lq, row0):
    W = vg.shape[1]
    n = 2 * Lq
    B = lay.B
    tt = lay.gr
    ntt = Lq // tt
    tn = _tile(W, 256)
    g0 = row0 // tt
    return pl.pallas_call(
        _hy_inv_body,
        out_shape=jax.ShapeDtypeStruct((B * Lq, W), BF16),
        grid=(B, W // tn, ntt),
        in_specs=[pl.BlockSpec((tt, n), lambda b, j, i: (i, 0)),
                  pl.BlockSpec((tt, n), lambda b, j, i: (i, 0)),
                  pl.BlockSpec((n, tn), lambda b, j, i: (b, j)),
                  pl.BlockSpec((n, tn), lambda b, j, i: (b, j)),
                  pl.BlockSpec((tt, tn), lambda b, j, i: (b * ntt + i, j)),
                  pl.BlockSpec((tt, tn), lambda b, j, i: (b * lay.NG + g0 + i, j)),
                  pl.BlockSpec((1, tn), lambda b, j, i: (0, j))],
        out_specs=pl.BlockSpec((tt, tn), lambda b, j, i: (b * ntt + i, j)),
        compiler_params=_params("parallel", "parallel", "arbitrary"),
        name="hyena_inv",
    )(gh, gl, yh, yl, vg, x0u, bias.reshape(1, W))


def _hyena_features(L):
    bands = (HY_EMB - 1) // 2
    t = jnp.linspace(0.0, 1.0, L, dtype=F32)[:, None]
    wpos = 2.0 * math.pi * jnp.arange(L, dtype=F32)[:, None] / L
    fb = jnp.linspace(1e-4, bands - 1, bands, dtype=F32)[None]
    z = jnp.concatenate([t, jnp.cos(fb * wpos), -jnp.sin(fb * wpos)], axis=-1)
    z2 = jnp.concatenate([z, z[::-1]], axis=0)
    return jnp.pad(z2, ((0, 0), (0, LANES - HY_EMB)))


def hyena_stream(vg, x0u, bias, wts, lay, Lq, row0):
    w1p, b1, w2, b2, w3, freq, deltas2 = wts
    fwd, inv, FT = _dft_tables(Lq)
    fh, fl = _split_bf16(fwd)
    gh, gl = _split_bf16(inv)
    kern = hyena_filter(_hyena_features(Lq), w1p, b1, w2, b2, w3, freq, deltas2, Lq)
    kf = hyena_spectrum(fh, fl, kern, FT)
    yh, yl = hyena_fwd(fh[:, :Lq], fl[:, :Lq], vg, kf, lay.B, Lq, FT)
    return hyena_inv(gh, gl, yh, yl, vg, x0u, bias, lay, Lq, row0)


def _rope_tables(lay):
    n_freq = DIFF_DIM // 4
    inv = ROPE_THETA ** (-jnp.arange(n_freq, dtype=F32) / n_freq)
    rows = lay.L // GRID_W
    r = jnp.repeat(jnp.arange(rows, dtype=F32), GRID_W)
    col = jnp.tile(jnp.arange(GRID_W, dtype=F32), rows)
    ang = jnp.concatenate([r[:, None] * inv, col[:, None] * inv], axis=-1)
    cos, sin = jnp.cos(ang), jnp.sin(ang)
    cos = jnp.concatenate([jnp.ones((lay.Lc, DIFF_DIM // 2), F32), cos], axis=0)
    sin = jnp.concatenate([jnp.zeros((lay.Lc, DIFF_DIM // 2), F32), sin], axis=0)
    cos_t = jnp.tile(cos, (1, 4))
    sin_t = jnp.tile(jnp.concatenate([-sin, sin], axis=-1), (1, 2))
    return cos_t, sin_t


def _main_cols(W, H):
    small0 = 7 * W
    small1 = small0 + 4 * H
    return small0, small1


def mixers(px, ps, lay, W, layer, need_ctx, p):
    H = W // HEAD_DIM
    lam_init = 0.8 - 0.6 * math.exp(-0.3 * layer)
    q_rot, k_rot = attn_prep(px, p["cos_t"], p["sin_t"], p["qk_g2"], lay, W)
    att = attention(q_rot, k_rot, px, p["attn_lambda"], p["attn_subln_g"], lay, W, lam_init)
    gdn = gated_deltanet(px, ps, p["gdn_conv_w"], p["alog"], p["dtb"], p["gdn_norm_g"], lay, W, 3 * H)
    hg = hgrn2(px, p["hg_lb_logits"], p["hg_norm_g"], lay, W, 7 * H, layer)
    vgx, vge, x0u = hyena_prep(px, p["hy_conv_w"], p["hy_conv_b"], lay, W, 12 * H)
    hy_x = hyena_stream(vgx, x0u, p["hy_bias"], p["hy_wts"], lay, lay.L, lay.Lc)
    if need_ctx:
        hy_e = hyena_stream(vge, x0u, p["hy_bias"], p["hy_wts"], lay, lay.Lc, 0)
    else:
        hy_e = jnp.zeros((lay.B, lay.Lc, W), BF16)
    hy = jnp.concatenate([hy_e.reshape(lay.B, lay.Lc, W), hy_x.reshape(lay.B, lay.L, W)], axis=1)
    return att, gdn, hg, hy.reshape(lay.T, W)


def _layer_params(l, lay, W, attn_qk_g, attn_lambda, attn_subln_g, gdn_conv_w, gdn_a_log, gdn_dt_bias, gdn_norm_g,
                  hg_lb_logits, hg_norm_g, hy_conv_w, hy_conv_b, hy_w1, hy_b1, hy_w2, hy_b2, hy_w3, hy_freq,
                  hy_bias, cos_t, sin_t):
    H = W // HEAD_DIM
    pad16 = lambda a: jnp.pad(a.astype(F32).reshape(1, 2 * H), ((0, 0), (0, LANES - 2 * H)))
    deltas = jnp.abs(jnp.linspace(math.log(HY_TARGET) / HY_SLOW_DECAY, math.log(HY_TARGET) / HY_FAST_DECAY, W, dtype=F32))
    return dict(
        cos_t=cos_t, sin_t=sin_t,
        qk_g2=jnp.tile(attn_qk_g[l].astype(F32), (1, 2)),
        attn_lambda=attn_lambda[l].astype(F32), attn_subln_g=attn_subln_g[l].astype(F32),
        gdn_conv_w=gdn_conv_w[l], alog=pad16(gdn_a_log[l]), dtb=pad16(gdn_dt_bias[l]), gdn_norm_g=gdn_norm_g[l],
        hg_lb_logits=hg_lb_logits.astype(F32).reshape(hg_lb_logits.shape[0], 2 * W), hg_norm_g=hg_norm_g[l],
        hy_conv_w=hy_conv_w[l], hy_conv_b=hy_conv_b[l].reshape(1, 3 * W), hy_bias=hy_bias[l].astype(F32),
        hy_wts=(jnp.pad(hy_w1[l].astype(F32), ((0, LANES - HY_EMB), (0, 0))), hy_b1[l].astype(F32),
                hy_w2[l].astype(F32), hy_b2[l].astype(F32), hy_w3[l].astype(F32), hy_freq[l].astype(F32),
                jnp.tile(deltas, 2).reshape(1, 2 * W)),
    )


def kernel(x, c, ctx, c_ctx, norm_g, w_mod, b_mod, ffn_w_in, ffn_w_out, w_in, attn_qk_g, attn_lambda, attn_subln_g, gdn_conv_w, gdn_a_log, gdn_dt_bias, gdn_norm_g, hg_lb_logits, hg_norm_g, hy_conv_w, hy_conv_b, hy_w1, hy_b1, hy_w2, hy_b2, hy_w3, hy_freq, hy_bias, w_gate, w_up, w_out):
    B, L, D = x.shape
    Lc = ctx.shape[1]
    depth = w_mod.shape[0]
    W = D // N_BRANCH
    H = W // HEAD_DIM
    lay = Layout(B, L, Lc)
    h = jnp.concatenate([ctx, x], axis=1).reshape(lay.T, D)
    G = -(-(B + 1) // 8) * 8
    c_all = jnp.concatenate([c, c_ctx[None], jnp.zeros((G - B - 1, D), F32)], axis=0)
    cos_t, sin_t = _rope_tables(lay)
    small0, small1 = _main_cols(W, H)
    for l in range(depth):
        last = l == depth - 1
        p = _layer_params(l, lay, W, attn_qk_g, attn_lambda, attn_subln_g, gdn_conv_w, gdn_a_log, gdn_dt_bias,
                          gdn_norm_g, hg_lb_logits, hg_norm_g, hy_conv_w, hy_conv_b, hy_w1, hy_b1, hy_w2, hy_b2,
                          hy_w3, hy_freq, hy_bias, cos_t, sin_t)
        mod = mod_table(c_all, w_mod[l], b_mod[l]).reshape(G, MOD_CHUNKS, D)
        act = mm_swiglu(norm_mod(h, norm_g[l, 0], mod, lay, 0), ffn_w_in[l, 0])
        h = mm_resid(act, ffn_w_out[l, 0], h, mod, lay, 2, 0.5)
        xn = norm_mod(h, norm_g[l, 1], mod, lay, 1)
        w_main = jnp.concatenate([w_in[l][:, :small0], w_in[l][:, small1:]], axis=1)
        w_small = jnp.pad(w_in[l][:, small0:small1], ((0, 0), (0, LANES - (small1 - small0))))
        px = mm_plain(xn, w_main)
        ps = mm_plain(xn, w_small, LANES)
        branches = mixers(px, ps, lay, W, l, not last, p)
        acc = merge_branches(xn, branches, w_gate[l].astype(BF16), w_up[l].astype(BF16))
        h = mm_resid(acc, w_out[l], h, mod, lay, 5, 1.0)
        act = mm_swiglu(norm_mod(h, norm_g[l, 2], mod, lay, 2), ffn_w_in[l, 1])
        h = mm_resid(act, ffn_w_out[l, 1], h, mod, lay, 8, 0.5)
    return h.reshape(B, lay.S, D)[:, Lc:, :]
```

```python
import functools
import math

import jax
import jax.numpy as jnp
from jax import lax
from jax.experimental import pallas as pl
from jax.experimental.pallas import tpu as pltpu

F32 = jnp.float32
BF16 = jnp.bfloat16

GRID_W = 64
N_BRANCH = 4
HEAD_DIM = 128
DIFF_DIM = HEAD_DIM // 2
MOD_CHUNKS = 9
ROPE_THETA = 10000.0
CHUNK = 64
HY_EMB = 33
HY_FAST_DECAY = 0.3
HY_SLOW_DECAY = 1.5
HY_TARGET = 1e-2
EPS = 1e-6
MASK_NEG = -1e30
LB_FLOOR = 1e-20

LANES = 128
V7X_VMEM_LIMIT = 56 * 1024 * 1024
HIGHEST = lax.Precision.HIGHEST

_NT = (((1,), (1,)), ((), ()))
_TN = (((0,), (0,)), ((), ()))


def _params(*sem):
    return pltpu.CompilerParams(dimension_semantics=sem, vmem_limit_bytes=V7X_VMEM_LIMIT)


def _tile(n, pref):
    t = min(n, pref)
    while n % t:
        t //= 2
    return t


def _wspec(at, K, tn, col):
    return pl.BlockSpec((None,) * len(at) + (K, tn), lambda *ids: tuple(at) + (0, col(*ids)))


def _sigmoid(x):
    return 1.0 / (1.0 + jnp.exp(-x))


def _silu(x):
    return x * _sigmoid(x)


def _softplus(x):
    return jnp.maximum(x, 0.0) + jnp.log1p(jnp.exp(-jnp.abs(x)))


class Layout:
    def __init__(self, B, L, Lc):
        self.B, self.L, self.Lc = B, L, Lc
        self.S = L + Lc
        self.T = B * self.S
        self.gr = math.gcd(L, Lc)
        self.NG = self.S // self.gr
        self.NGc = Lc // self.gr


def _mod_row(mod_ref, lay, r, k):
    b = r // lay.NG
    g = jnp.where(r - b * lay.NG < lay.NGc, lay.B, b)
    return mod_ref[pl.ds(g, 1), k:k + 1, :][0]


def _mod_body(c_ref, w_ref, b_ref, o_ref):
    a = _silu(c_ref[...]).astype(BF16)
    o_ref[...] = jnp.dot(a, w_ref[...].astype(BF16), preferred_element_type=F32) + b_ref[...]


def mod_table(c16, w_mod, b_mod, l):
    R, D = c16.shape
    N = w_mod.shape[-1]
    tn = _tile(N, 512)
    return pl.pallas_call(
        _mod_body,
        out_shape=jax.ShapeDtypeStruct((R, N), F32),
        grid=(N // tn,),
        in_specs=[pl.BlockSpec((R, D), lambda j: (0, 0)),
                  _wspec((l,), D, tn, lambda j: j),
                  _wspec((l,), 1, tn, lambda j: j)],
        out_specs=pl.BlockSpec((R, tn), lambda j: (0, j)),
        compiler_params=_params("parallel"),
        name="mod_table",
    )(c16, w_mod, b_mod.reshape(b_mod.shape[0], 1, N))


def _norm_mod_body(x_ref, g_ref, mod_ref, o_ref, *, lay, k):
    i = pl.program_id(0)
    gr = lay.gr
    n_sub = x_ref.shape[0] // gr
    for s in range(n_sub):
        x = x_ref[s * gr:(s + 1) * gr, :]
        y = x * lax.rsqrt(jnp.mean(x * x, axis=-1, keepdims=True) + EPS) * g_ref[...]
        r = i * n_sub + s
        shift = _mod_row(mod_ref, lay, r, 3 * k)
        scale = _mod_row(mod_ref, lay, r, 3 * k + 1)
        o_ref[s * gr:(s + 1) * gr, :] = (y * (1.0 + scale) + shift).astype(BF16)


def norm_mod(h, g, mod, lay, k):
    T, D = h.shape
    tm = lay.gr
    G = mod.shape[0]
    return pl.pallas_call(
        functools.partial(_norm_mod_body, lay=lay, k=k),
        out_shape=jax.ShapeDtypeStruct((T, D), BF16),
        grid=(T // tm,),
        in_specs=[pl.BlockSpec((tm, D), lambda i: (i, 0)),
                  pl.BlockSpec((1, D), lambda i: (0, 0)),
                  pl.BlockSpec((G, MOD_CHUNKS, D), lambda i: (0, 0, 0))],
        out_specs=pl.BlockSpec((tm, D), lambda i: (i, 0)),
        compiler_params=_params("parallel"),
        name="norm_mod",
    )(h, g.reshape(1, D), mod)


def _mm_swiglu_body(a_ref, wg_ref, wu_ref, o_ref, wgb, wub):
    @pl.when(pl.program_id(1) == 0)
    def _():
        wgb[...] = wg_ref[...].astype(BF16)
        wub[...] = wu_ref[...].astype(BF16)

    a = a_ref[...]
    g = jnp.dot(a, wgb[...], preferred_element_type=F32)
    u = jnp.dot(a, wub[...], preferred_element_type=F32)
    o_ref[...] = (_silu(g) * u).astype(BF16)


def mm_swiglu(a, w, at):
    T, K = a.shape
    F = w.shape[-1] // 2
    tm, tn = _tile(T, 1024), _tile(F, 256)
    nj = F // tn
    return pl.pallas_call(
        _mm_swiglu_body,
        out_shape=jax.ShapeDtypeStruct((T, F), BF16),
        grid=(nj, T // tm),
        in_specs=[pl.BlockSpec((tm, K), lambda j, i: (i, 0)),
                  _wspec(at, K, tn, lambda j, i: j),
                  _wspec(at, K, tn, lambda j, i: j + nj)],
        out_specs=pl.BlockSpec((tm, tn), lambda j, i: (i, j)),
        scratch_shapes=[pltpu.VMEM((K, tn), BF16), pltpu.VMEM((K, tn), BF16)],
        compiler_params=_params("parallel", "arbitrary"),
        name="mm_swiglu",
    )(a, w, w)


def _mm_resid_body(a_ref, w_ref, h_ref, mod_ref, o_ref, wb, *, lay, k, scale):
    i = pl.program_id(1)

    @pl.when(i == 0)
    def _():
        wb[...] = w_ref[...].astype(BF16)

    y = jnp.dot(a_ref[...], wb[...], preferred_element_type=F32)
    gr = lay.gr
    n_sub = a_ref.shape[0] // gr
    for s in range(n_sub):
        gate = _mod_row(mod_ref, lay, i * n_sub + s, k)
        sl = slice(s * gr, (s + 1) * gr)
        o_ref[sl, :] = h_ref[sl, :] + (scale * gate) * y[sl, :]


def mm_resid(a, w, at, h, mod, lay, k, scale):
    T, K = a.shape
    N = w.shape[-1]
    G = mod.shape[0]
    tm, tn = _tile(T, 512), _tile(N, 512)
    tm = max(tm, lay.gr)
    return pl.pallas_call(
        functools.partial(_mm_resid_body, lay=lay, k=k, scale=scale),
        out_shape=jax.ShapeDtypeStruct((T, N), F32),
        grid=(N // tn, T // tm),
        in_specs=[pl.BlockSpec((tm, K), lambda j, i: (i, 0)),
                  _wspec(at, K, tn, lambda j, i: j),
                  pl.BlockSpec((tm, tn), lambda j, i: (i, j)),
                  pl.BlockSpec((G, MOD_CHUNKS, tn), lambda j, i: (0, 0, j))],
        out_specs=pl.BlockSpec((tm, tn), lambda j, i: (i, j)),
        scratch_shapes=[pltpu.VMEM((K, tn), BF16)],
        compiler_params=_params("parallel", "arbitrary"),
        name="mm_resid",
    )(a, w, h, mod)


def _mm_plain_body(a_ref, w_ref, o_ref, wb):
    @pl.when(pl.program_id(1) == 0)
    def _():
        wb[...] = w_ref[...].astype(BF16)

    o_ref[...] = jnp.dot(a_ref[...], wb[...], preferred_element_type=F32)


def mm_plain(a, w, at=(), n_cols=None, tn_pref=512):
    T, K = a.shape
    N = n_cols or w.shape[-1]
    tm, tn = _tile(T, 1024), _tile(N, tn_pref)
    return pl.pallas_call(
        _mm_plain_body,
        out_shape=jax.ShapeDtypeStruct((T, N), F32),
        grid=(N // tn, T // tm),
        in_specs=[pl.BlockSpec((tm, K), lambda j, i: (i, 0)),
                  _wspec(at, K, tn, lambda j, i: j)],
        out_specs=pl.BlockSpec((tm, tn), lambda j, i: (i, j)),
        scratch_shapes=[pltpu.VMEM((K, tn), BF16)],
        compiler_params=_params("parallel", "arbitrary"),
        name="mm_plain",
    )(a, w)


def _merge_body(xn_ref, o0_ref, o1_ref, o2_ref, o3_ref, wg_ref, wu_ref, out_ref):
    xn = xn_ref[...]
    acc = None
    for br, o_ref in enumerate((o0_ref, o1_ref, o2_ref, o3_ref)):
        g = jnp.dot(xn, wg_ref[br], preferred_element_type=F32)
        u = jnp.dot(o_ref[...], wu_ref[br], preferred_element_type=F32)
        term = _sigmoid(g) * u
        acc = term if acc is None else acc + term
    out_ref[...] = acc.astype(BF16)


def merge_branches(xn, branches, w_gate_b, w_up_b):
    T, D = xn.shape
    W = branches[0].shape[1]
    tm, tn = _tile(T, 512), _tile(D, 256)
    bspec = pl.BlockSpec((tm, W), lambda i, j: (i, 0))
    return pl.pallas_call(
        _merge_body,
        out_shape=jax.ShapeDtypeStruct((T, D), BF16),
        grid=(T // tm, D // tn),
        in_specs=[pl.BlockSpec((tm, D), lambda i, j: (i, 0)), bspec, bspec, bspec, bspec,
                  pl.BlockSpec((N_BRANCH, D, tn), lambda i, j: (0, 0, j)),
                  pl.BlockSpec((N_BRANCH, W, tn), lambda i, j: (0, 0, j))],
        out_specs=pl.BlockSpec((tm, tn), lambda i, j: (i, j)),
        compiler_params=_params("parallel", "arbitrary"),
        name="merge_branches",
    )(xn, *branches, w_gate_b, w_up_b)


def _attn_prep_body(q_ref, k_ref, cos_ref, sin_ref, g_ref, qo_ref, ko_ref):
    cos = cos_ref[...]
    sin = sin_ref[...]
    lane = lax.broadcasted_iota(jnp.int32, cos.shape, 1)
    lo = lane < DIFF_DIM
    first = (lane % DIFF_DIM) < DIFF_DIM // 2
    n_heads = q_ref.shape[1] // HEAD_DIM
    for src, dst, gi, sc in ((q_ref, qo_ref, 0, DIFF_DIM ** -0.5), (k_ref, ko_ref, 1, 1.0)):
        g = g_ref[gi:gi + 1, :]
        for h in range(n_heads):
            cs = slice(h * HEAD_DIM, (h + 1) * HEAD_DIM)
            t = src[:, cs]
            ss = t * t
            s_lo = jnp.sum(jnp.where(lo, ss, 0.0), axis=-1, keepdims=True)
            s_hi = jnp.sum(jnp.where(lo, 0.0, ss), axis=-1, keepdims=True)
            inv = jnp.where(lo, lax.rsqrt(s_lo / DIFF_DIM + EPS), lax.rsqrt(s_hi / DIFF_DIM + EPS))
            y = t * inv * g
            partner = jnp.where(first, pltpu.roll(y, HEAD_DIM - DIFF_DIM // 2, 1), pltpu.roll(y, DIFF_DIM // 2, 1))
            dst[:, cs] = ((y * cos + partner * sin) * sc).astype(BF16)


def attn_prep(px, cos_t, sin_t, qk_g2, lay, W):
    T = px.shape[0]
    tr = lay.gr
    return pl.pallas_call(
        _attn_prep_body,
        out_shape=(jax.ShapeDtypeStruct((T, W), BF16), jax.ShapeDtypeStruct((T, W), BF16)),
        grid=(T // tr,),
        in_specs=[pl.BlockSpec((tr, W), lambda i: (i, 0)),
                  pl.BlockSpec((tr, W), lambda i: (i, 1)),
                  pl.BlockSpec((tr, LANES), lambda i: (i % lay.NG, 0)),
                  pl.BlockSpec((tr, LANES), lambda i: (i % lay.NG, 0)),
                  pl.BlockSpec((2, LANES), lambda i: (0, 0))],
        out_specs=(pl.BlockSpec((tr, W), lambda i: (i, 0)), pl.BlockSpec((tr, W), lambda i: (i, 0))),
        compiler_params=_params("parallel"),
        name="attn_prep",
    )(px, px, cos_t, sin_t, qk_g2)


def _attn_body(q_ref, k_ref, v_ref, lam_ref, g_ref, o_ref, *, lay, lam_init):
    qi = pl.program_id(2)
    lp = lam_ref[...]
    lam = (jnp.exp(jnp.sum(lp[0:1] * lp[1:2], axis=-1, keepdims=True))
           - jnp.exp(jnp.sum(lp[2:3] * lp[3:4], axis=-1, keepdims=True)) + lam_init)

    def run(nk):
        q = q_ref[...]
        k = k_ref[0:nk, :]
        v = v_ref[0:nk, :].astype(BF16)
        lane = lax.broadcasted_iota(jnp.int32, q.shape, 1)
        zero = jnp.zeros_like(q)
        outs = []
        for qq in (jnp.where(lane < DIFF_DIM, q, zero), jnp.where(lane < DIFF_DIM, zero, q)):
            s = lax.dot_general(qq, k, _NT, preferred_element_type=F32)
            p = jnp.exp(s - jnp.max(s, axis=-1, keepdims=True))
            l = jnp.sum(p, axis=-1, keepdims=True)
            outs.append(jnp.dot(p.astype(BF16), v, preferred_element_type=F32) / l)
        o = outs[0] - lam * outs[1]
        y = o * lax.rsqrt(jnp.mean(o * o, axis=-1, keepdims=True) + EPS) * g_ref[...]
        o_ref[...] = (y * (1.0 - lam_init)).astype(BF16)

    @pl.when(qi < lay.NGc)
    def _():
        run(lay.Lc)

    @pl.when(qi >= lay.NGc)
    def _():
        run(lay.S)


def attention(q_rot, k_rot, px, lam_p, subln_g, lay, W, lam_init):
    T = px.shape[0]
    H = W // HEAD_DIM
    tq = lay.gr
    S = lay.S
    vb = W // HEAD_DIM * 2
    return pl.pallas_call(
        functools.partial(_attn_body, lay=lay, lam_init=lam_init),
        out_shape=jax.ShapeDtypeStruct((T, W), BF16),
        grid=(lay.B, H, lay.NG),
        in_specs=[pl.BlockSpec((tq, HEAD_DIM), lambda b, h, i: (b * lay.NG + i, h)),
                  pl.BlockSpec((S, HEAD_DIM), lambda b, h, i: (b, h)),
                  pl.BlockSpec((S, HEAD_DIM), lambda b, h, i: (b, vb + h)),
                  pl.BlockSpec((4, DIFF_DIM), lambda b, h, i: (0, 0)),
                  pl.BlockSpec((1, HEAD_DIM), lambda b, h, i: (0, 0))],
        out_specs=pl.BlockSpec((tq, HEAD_DIM), lambda b, h, i: (b * lay.NG + i, h)),
        compiler_params=_params("parallel", "parallel", "arbitrary"),
        name="diff_attention",
    )(q_rot, k_rot, px, lam_p, subln_g.reshape(1, HEAD_DIM))


def _chunk_scan(y, reverse):
    S = y.shape[0]
    pos = lax.broadcasted_iota(jnp.int32, y.shape, 0) % CHUNK
    sh = 1
    while sh < CHUNK:
        if reverse:
            y = y + jnp.where(pos < CHUNK - sh, pltpu.roll(y, S - sh, 0), 0.0)
        else:
            y = y + jnp.where(pos >= sh, pltpu.roll(y, sh, 0), 0.0)
        sh *= 2
    return y


def _chunk_of_step(n, d, lay):
    ncc = lay.Lc // CHUNK
    nc = lay.S // CHUNK
    if d == 0:
        return n
    return jnp.where(n < ncc, ncc - 1 - n, nc - 1 - (n - ncc))


def _gated_head_norm(o, z, g):
    y = o * lax.rsqrt(jnp.mean(o * o, axis=-1, keepdims=True) + EPS) * g
    return (y * _silu(z)).astype(BF16)


def _centred_conv(x, w, prev_ok, next_ok):
    S = x.shape[0]
    xp = jnp.where(prev_ok, pltpu.roll(x, 1, 0), 0.0)
    xn = jnp.where(next_ok, pltpu.roll(x, S - 1, 0), 0.0)
    return xp * w[0:1] + x * w[1:2] + xn * w[2:3]


def _conv_masks(shape, lay):
    row = lax.broadcasted_iota(jnp.int32, shape, 0)
    prev_ok = (row * (row - lay.Lc)) != 0
    next_ok = ((row - (lay.Lc - 1)) * (row - (lay.S - 1))) != 0
    return prev_ok, next_ok


TRI_BLOCK = 16


def _bmm(x, y):
    return jnp.einsum('gij,gjk->gik', x, y, preferred_element_type=F32)


def _bmm_nt(x, y):
    return jnp.einsum('gik,gjk->gij', x, y, preferred_element_type=F32)


def _bmm_tn(x, y):
    return jnp.einsum('gjk,gjl->gkl', x, y, preferred_element_type=F32)


def _column_spread_matrix():
    m = jnp.arange(CHUNK)[:, None]
    c = jnp.arange(TRI_BLOCK * LANES)[None, :]
    j, lane = c // LANES, c % LANES
    hit = (lane < CHUNK) & (m % TRI_BLOCK == j) & (m // TRI_BLOCK == lane // TRI_BLOCK)
    return hit.astype(BF16)


def _unit_tri_inverse(a, spread, ii, jj, d):
    G = a.shape[0]
    nb = TRI_BLOCK
    a_d = jnp.where(ii // nb == jj // nb, a, 0.0).reshape(G * CHUNK, CHUNK)
    a_hi, a_lo = _split_bf16(a_d)
    cols = (jnp.dot(a_hi, spread, preferred_element_type=F32) + jnp.dot(a_lo, spread, preferred_element_type=F32))
    row = lax.broadcasted_iota(jnp.int32, (G, CHUNK, LANES), 1)
    lane = lax.broadcasted_iota(jnp.int32, (G, CHUNK, LANES), 2)
    x = jnp.where(row == lane, 1.0, 0.0)
    for j in (range(nb - 1) if d == 0 else range(nb - 1, 0, -1)):
        col_j = cols[:, j * LANES:(j + 1) * LANES].reshape(G, CHUNK, LANES)
        x4 = x.reshape(G, CHUNK // nb, nb, LANES)
        row_j = jnp.broadcast_to(x4[:, :, j:j + 1, :], x4.shape).reshape(G, CHUNK, LANES)
        x = x - col_j * row_j
    t = x[:, :, :CHUNK]
    width = nb
    while width < CHUNK:
        e = jnp.where(((ii // width) ^ (jj // width)) == 1, a, 0.0).astype(BF16)
        tb = t.astype(BF16)
        t = t - _bmm(tb, _bmm(e, tb).astype(BF16))
        width *= 2
    return t


def _gdn_body(q_ref, k_ref, v_ref, z_ref, ps_ref, cq_ref, ck_ref, cv_ref, alog_ref, dtb_ref, g_ref, sel_ref, o_ref,
              qs, ks, vs, gc, bb, p_s, r_s, k_s, n_s, dl_s, o_s, *, lay, n_heads, group):
    h = pl.program_id(1)
    S = lay.S
    nc = S // CHUNK
    shape = (S, HEAD_DIM)
    prev_ok, next_ok = _conv_masks(shape, lay)
    lane = lax.broadcasted_iota(jnp.int32, shape, 1)

    q = _silu(_centred_conv(q_ref[...], cq_ref[...], prev_ok, next_ok))
    qs[...] = q * lax.rsqrt(jnp.sum(q * q, axis=-1, keepdims=True) + EPS) * HEAD_DIM ** -0.5
    k = _silu(_centred_conv(k_ref[...], ck_ref[...], prev_ok, next_ok))
    ks[...] = k * lax.rsqrt(jnp.sum(k * k, axis=-1, keepdims=True) + EPS)
    vs[...] = _silu(_centred_conv(v_ref[...], cv_ref[...], prev_ok, next_ok))

    p = ps_ref[...]
    g_all = -jnp.exp(alog_ref[...]) * _softplus(p + dtb_ref[...])
    beta_all = _sigmoid(p)
    for d in range(2):
        gsel = jnp.sum(jnp.where(lane == d * n_heads + h, g_all, 0.0), axis=-1, keepdims=True)
        gc[d] = _chunk_scan(jnp.broadcast_to(gsel, shape), reverse=(d == 1))
        bsel = jnp.sum(jnp.where(lane == (2 + d) * n_heads + h, beta_all, 0.0), axis=-1, keepdims=True)
        bb[d] = jnp.broadcast_to(bsel, shape)

    G = group
    ii = lax.broadcasted_iota(jnp.int32, (G, CHUNK, CHUNK), 1)
    jj = lax.broadcasted_iota(jnp.int32, (G, CHUNK, CHUNK), 2)
    bmm, bmm_nt, bmm_tn = _bmm, _bmm_nt, _bmm_tn

    def prep(gi, d):
        rows = G * CHUNK
        sl = pl.ds(pl.multiple_of(gi * rows, rows), rows)
        cs = pl.ds(gi * G, G)
        r3 = lambda x: x.reshape(G, CHUNK, HEAD_DIM)
        q3, k3, v3, gcb, b3 = r3(qs[sl, :]), r3(ks[sl, :]), r3(vs[sl, :]), r3(gc[d, sl, :]), r3(bb[d, sl, :])
        glast = gcb[:, CHUNK - 1:CHUNK, :] if d == 0 else gcb[:, 0:1, :]
        eg = jnp.exp(gcb)
        kb = k3 * b3
        diff = gcb[:, :, :CHUNK] - jnp.swapaxes(gcb, 1, 2)[:, :CHUNK, :]
        incl = (ii >= jj) if d == 0 else (ii <= jj)
        strict = (ii > jj) if d == 0 else (ii < jj)
        decay = jnp.exp(jnp.where(incl, diff, MASK_NEG))
        kbf = k3.astype(BF16)
        a = jnp.where(strict, bmm_nt(kb.astype(BF16), kbf) * decay, 0.0)
        tb = _unit_tri_inverse(a, sel_ref[...], ii, jj, d).astype(BF16)
        ub = bmm(tb, (v3 * b3).astype(BF16)).astype(BF16)
        wb = bmm(tb, (kb * eg).astype(BF16)).astype(BF16)
        attb = (bmm_nt(q3.astype(BF16), kbf) * decay).astype(BF16)
        kgb = (k3 * jnp.exp(glast - gcb)).astype(BF16)
        p_s[d, sl, :] = (q3 * eg - bmm(attb, wb)).astype(BF16).reshape(rows, HEAD_DIM)
        r_s[d, sl, :] = bmm(attb, ub).reshape(rows, HEAD_DIM)
        k_s[d, cs, :, :] = bmm_tn(kgb, wb).astype(BF16)
        n_s[d, cs, :, :] = bmm_tn(kgb, ub)
        dl_s[d, cs, :, :] = jnp.broadcast_to(jnp.exp(glast), (G, 8, HEAD_DIM))

    def prep_loop(gi, carry):
        prep(gi, 0)
        prep(gi, 1)
        return carry

    lax.fori_loop(0, nc // G, prep_loop, 0)

    def step_d(d, n, s):
        c = _chunk_of_step(n, d, lay)
        sl = pl.ds(pl.multiple_of(c * CHUNK, CHUNK), CHUNK)
        sb = s.astype(BF16)
        o_s[d, sl, :] = r_s[d, sl, :] + jnp.dot(p_s[d, sl, :], sb, preferred_element_type=F32)
        dl = dl_s[d, pl.ds(c, 1), :, :][0][0:1, :]
        return (s * dl + n_s[d, pl.ds(c, 1), :, :][0]
                - jnp.dot(k_s[d, pl.ds(c, 1), :, :][0], sb, preferred_element_type=F32))

    def both(n, carry):
        return (step_d(0, n, carry[0]), step_d(1, n, carry[1]))

    zero = jnp.zeros((HEAD_DIM, HEAD_DIM), F32)
    lax.fori_loop(0, nc, both, (zero, zero))
    o_ref[...] = _gated_head_norm(o_s[0] + o_s[1], z_ref[...], g_ref[...])


def gated_deltanet(px, ps, conv_w, alog_l, dtb_l, norm_g, lay, W, col0):
    T = px.shape[0]
    H = W // HEAD_DIM
    S = lay.S
    nc = S // CHUNK

    def col(o):
        return pl.BlockSpec((S, HEAD_DIM), lambda b, h: (b, col0 + o * H + h))

    def cw(o):
        return pl.BlockSpec((3, HEAD_DIM), lambda b, h: (0, o * H + h))

    one = pl.BlockSpec((1, HEAD_DIM), lambda b, h: (0, 0))
    sh = (S, HEAD_DIM)
    st = (2, nc, HEAD_DIM, HEAD_DIM)
    group = max(g for g in range(1, 10) if nc % g == 0)
    return pl.pallas_call(
        functools.partial(_gdn_body, lay=lay, n_heads=H, group=group),
        out_shape=jax.ShapeDtypeStruct((T, W), BF16),
        grid=(lay.B, H),
        in_specs=[col(0), col(1), col(2), col(3),
                  pl.BlockSpec((S, LANES), lambda b, h: (b, 0)),
                  cw(0), cw(1), cw(2), one, one, one,
                  pl.BlockSpec((CHUNK, TRI_BLOCK * LANES), lambda b, h: (0, 0))],
        out_specs=pl.BlockSpec((S, HEAD_DIM), lambda b, h: (b, h)),
        scratch_shapes=[pltpu.VMEM(sh, F32), pltpu.VMEM(sh, F32), pltpu.VMEM(sh, F32),
                        pltpu.VMEM((2,) + sh, F32), pltpu.VMEM((2,) + sh, F32),
                        pltpu.VMEM((2,) + sh, BF16), pltpu.VMEM((2,) + sh, F32),
                        pltpu.VMEM(st, BF16), pltpu.VMEM(st, F32),
                        pltpu.VMEM((2, nc, 8, HEAD_DIM), F32), pltpu.VMEM((2,) + sh, F32)],
        compiler_params=_params("parallel", "arbitrary"),
        name="gated_deltanet",
    )(px, px, px, px, ps, conv_w, conv_w, conv_w, alog_l, dtb_l, norm_g.reshape(1, HEAD_DIM),
      _column_spread_matrix())


def _hgrn_body(q_ref, f0_ref, f1_ref, i_ref, z_ref, lb0_ref, lb1_ref, g_ref, o_ref,
               gc, kk, o_s, *, lay, layer):
    S = lay.S
    nc = S // CHUNK
    for d, (f_ref, lb_ref) in enumerate(((f0_ref, lb0_ref), (f1_ref, lb1_ref))):
        lg = lb_ref[...]
        depth = lg.shape[0]
        mx = lg[0:1]
        for r in range(1, depth):
            mx = jnp.maximum(mx, lg[r:r + 1])
        e = [jnp.exp(lg[r:r + 1] - mx) for r in range(depth)]
        tot = e[0]
        for r in range(1, depth):
            tot = tot + e[r]
        lb = jnp.zeros_like(mx)
        for r in range(1, layer + 1):
            lb = lb + e[r] / tot
        f = f_ref[...]
        log_sig = jnp.minimum(f, 0.0) - jnp.log1p(jnp.exp(-jnp.abs(f)))
        a = jnp.log(jnp.maximum(lb, LB_FLOOR))
        b = jnp.log1p(-lb) + log_sig
        logf = jnp.maximum(a, b) + jnp.log1p(jnp.exp(-jnp.abs(a - b)))
        gc[d] = _chunk_scan(logf, reverse=(d == 1))
        kk[d] = (1.0 - lb) * _sigmoid(-f)

    SB = 16
    nsb = CHUNK // SB
    jrow = lax.broadcasted_iota(jnp.int32, (CHUNK, HEAD_DIM), 0)
    jsub = lax.broadcasted_iota(jnp.int32, (SB, HEAD_DIM), 0)
    alane = lax.broadcasted_iota(jnp.int32, (SB, CHUNK), 1)

    def intra_t(d, qc, kc, g):
        if d == 0:
            refs = [g[0:1, :]] + [g[SB * I - 1:SB * I, :] for I in range(1, nsb)]
            far = range(1, nsb)
        else:
            refs = [g[SB * (I + 1):SB * (I + 1) + 1, :] for I in range(nsb - 1)] + [g[CHUNK - 1:CHUNK, :]]
            far = range(nsb - 1)
        rvec = jnp.concatenate([jnp.broadcast_to(r, (SB, HEAD_DIM)) for r in refs], axis=0)
        qt = qc * jnp.exp(g - rvec)
        off = None
        for I in far:
            seen = (jrow < SB * I) if d == 0 else (jrow >= SB * (I + 1))
            kt = (kc * jnp.exp(jnp.where(seen, refs[I] - g, MASK_NEG))).astype(BF16)
            qi = jnp.where((jrow // SB) == I, qt, 0.0).astype(BF16)
            part = lax.dot_general(kt, qi, _NT, preferred_element_type=F32)
            off = part if off is None else off + part
        blocks = []
        for I in range(nsb):
            rs = slice(SB * I, SB * (I + 1))
            g_i, k_i = g[rs, :], kc[rs, :]
            acc = jnp.zeros((SB, CHUNK), F32)
            for r in range(SB):
                i = SB * I + r
                keep = (jsub <= r) if d == 0 else (jsub >= r)
                rel = jnp.exp(jnp.where(keep, g[i:i + 1, :] - g_i, MASK_NEG))
                colv = jnp.sum(rel * k_i * qc[i:i + 1, :], axis=-1, keepdims=True)
                acc = jnp.where(alane == i, colv, acc)
            blocks.append(acc)
        return jnp.concatenate(blocks, axis=0) + off

    def step_d(d, n, st):
        c = _chunk_of_step(n, d, lay)
        sl = pl.ds(pl.multiple_of(c * CHUNK, CHUNK), CHUNK)
        qc, kc, vc, g = q_ref[sl, :], kk[d, sl, :], i_ref[sl, :], gc[d, sl, :]
        glast = g[CHUNK - 1:CHUNK, :] if d == 0 else g[0:1, :]
        a_t = intra_t(d, qc, kc, g)
        vb = vc.astype(BF16)
        o = (lax.dot_general((qc * jnp.exp(g)).astype(BF16), st.astype(BF16), _NT, preferred_element_type=F32)
             + lax.dot_general(a_t.astype(BF16), vb, _TN, preferred_element_type=F32))
        o_s[d, sl, :] = o
        kg = (kc * jnp.exp(glast - g)).astype(BF16)
        return st * jnp.exp(glast) + lax.dot_general(vb, kg, _TN, preferred_element_type=F32)

    def both(n, carry):
        return (step_d(0, n, carry[0]), step_d(1, n, carry[1]))

    zero = jnp.zeros((HEAD_DIM, HEAD_DIM), F32)
    lax.fori_loop(0, nc, both, (zero, zero))
    o_ref[...] = _gated_head_norm(o_s[0] + o_s[1], z_ref[...], g_ref[...])


def hgrn2(px, lb_logits, norm_g, lay, W, col0, layer):
    T = px.shape[0]
    H = W // HEAD_DIM
    S = lay.S
    depth = lb_logits.shape[0]

    def col(o):
        return pl.BlockSpec((S, HEAD_DIM), lambda b, h: (b, col0 + o * H + h))

    def lbs(d):
        return pl.BlockSpec((depth, HEAD_DIM), lambda b, h: (0, d * H + h))

    sh = (S, HEAD_DIM)
    return pl.pallas_call(
        functools.partial(_hgrn_body, lay=lay, layer=layer),
        out_shape=jax.ShapeDtypeStruct((T, W), BF16),
        grid=(lay.B, H),
        in_specs=[col(0), col(1), col(2), col(3), col(4), lbs(0), lbs(1),
                  pl.BlockSpec((1, HEAD_DIM), lambda b, h: (0, 0))],
        out_specs=pl.BlockSpec((S, HEAD_DIM), lambda b, h: (b, h)),
        scratch_shapes=[pltpu.VMEM((2,) + sh, F32), pltpu.VMEM((2,) + sh, F32), pltpu.VMEM((2,) + sh, F32)],
        compiler_params=_params("parallel", "arbitrary"),
        name="hgrn2",
    )(px, px, px, px, px, lb_logits, lb_logits, norm_g.reshape(1, HEAD_DIM))


def _split_bf16(x):
    hi = x.astype(BF16)
    return hi, (x - hi.astype(F32)).astype(BF16)


def _dot3(a_hi, a_lo, b_hi, b_lo):
    return (jnp.dot(a_hi, b_hi, preferred_element_type=F32)
            + jnp.dot(a_hi, b_lo, preferred_element_type=F32)
            + jnp.dot(a_lo, b_hi, preferred_element_type=F32))


def _hy_filter_body(z_ref, w1_ref, b1_ref, w2_ref, b2_ref, w3_ref, fr_ref, dl_ref, o_ref, *, L):
    i = pl.program_id(0)
    z = z_ref[...]
    tr = z.shape[0]
    W = o_ref.shape[1]
    h = jnp.sin(fr_ref[0:1, :] * (jnp.dot(z, w1_ref[...], precision=HIGHEST, preferred_element_type=F32) + b1_ref[...]))
    h = jnp.sin(fr_ref[1:2, :] * (jnp.dot(h, w2_ref[...], precision=HIGHEST, preferred_element_type=F32) + b2_ref[...]))
    h = jnp.dot(h, w3_ref[...], precision=HIGHEST, preferred_element_type=F32)
    h = h * jnp.exp(-z[:, 0:1] * dl_ref[...])
    row = lax.broadcasted_iota(jnp.int32, (tr, W), 0) + i * tr
    o_ref[...] = jnp.where(row < L, h[:, :W], jnp.where(row == L, 0.0, h[:, W:]))


def hyena_filter(z2, w1p, b1, w2, b2, w3, freq, deltas2, L):
    n = z2.shape[0]
    W = w3.shape[1] // 2
    O = w2.shape[0]
    tr = _tile(n, 512)
    full = lambda a: pl.BlockSpec(a.shape, lambda i: (0,) * a.ndim)
    args = (w1p, b1.reshape(1, O), w2, b2.reshape(1, O), w3, freq, deltas2)
    return pl.pallas_call(
        functools.partial(_hy_filter_body, L=L),
        out_shape=jax.ShapeDtypeStruct((n, W), F32),
        grid=(n // tr,),
        in_specs=[pl.BlockSpec((tr, LANES), lambda i: (i, 0))] + [full(a) for a in args],
        out_specs=pl.BlockSpec((tr, W), lambda i: (i, 0)),
        compiler_params=_params("parallel"),
        name="hyena_filter",
    )(z2, *args)


def _hy_prep_body(v_ref, x0_ref, x1_ref, cv_ref, c0_ref, c1_ref, bv_ref, b0_ref, b1_ref,
                  vgx_ref, vge_ref, x0u_ref, *, lay):
    shape = v_ref.shape
    prev_ok, next_ok = _conv_masks(shape, lay)
    v = _centred_conv(v_ref[...], cv_ref[...], prev_ok, next_ok) + bv_ref[...]
    x0 = _centred_conv(x0_ref[...], c0_ref[...], prev_ok, next_ok) + b0_ref[...]
    x1 = _centred_conv(x1_ref[...], c1_ref[...], prev_ok, next_ok) + b1_ref[...]
    vg = v * x1
    vge_ref[...] = vg[:lay.Lc, :]
    vgx_ref[...] = vg[lay.Lc:, :]
    x0u_ref[...] = x0


def hyena_prep(px, conv_w, conv_b, lay, W, col0):
    T = px.shape[0]
    H = W // HEAD_DIM
    S, L, Lc, B = lay.S, lay.L, lay.Lc, lay.B

    def col(o):
        return pl.BlockSpec((S, HEAD_DIM), lambda b, j: (b, col0 + o * H + j))

    def cw(o, rows):
        return pl.BlockSpec((rows, HEAD_DIM), lambda b, j: (0, o * H + j))

    return pl.pallas_call(
        functools.partial(_hy_prep_body, lay=lay),
        out_shape=(jax.ShapeDtypeStruct((B * L, W), F32), jax.ShapeDtypeStruct((B * Lc, W), F32),
                   jax.ShapeDtypeStruct((T, W), F32)),
        grid=(B, H),
        in_specs=[col(0), col(1), col(2), cw(0, 3), cw(1, 3), cw(2, 3), cw(0, 1), cw(1, 1), cw(2, 1)],
        out_specs=(pl.BlockSpec((L, HEAD_DIM), lambda b, j: (b, j)),
                   pl.BlockSpec((Lc, HEAD_DIM), lambda b, j: (b, j)),
                   pl.BlockSpec((S, HEAD_DIM), lambda b, j: (b, j))),
        compiler_params=_params("parallel", "parallel"),
        name="hyena_prep",
    )(px, px, px, conv_w, conv_w, conv_w, conv_b, conv_b, conv_b)


def _dft_tables(L):
    n = 2 * L
    FT = min(256, L)
    sub = 64
    idx = jnp.arange(n, dtype=jnp.int32)
    r = idx % (2 * FT)
    is_im = r >= FT
    f = (idx // (2 * FT)) * FT + jnp.where(is_im, r - FT, r)
    nyq = is_im & (f == 0)
    f = jnp.where(nyq, L, f)
    step = jnp.arange(sub, dtype=jnp.int32)
    ang_hi = (2.0 * math.pi / n) * ((f[:, None] * (step * sub)[None, :]) % n).astype(F32)
    ang_lo = (2.0 * math.pi / n) * ((f[:, None] * step[None, :]) % n).astype(F32)
    ch, sh, cl, sl = jnp.cos(ang_hi), jnp.sin(ang_hi), jnp.cos(ang_lo), jnp.sin(ang_lo)
    wgt = jnp.where((f == 0) | (f == L), 1.0 / n, 2.0 / n)
    minus_sin = is_im & ~nyq

    def table(hi_steps, transpose):
        if transpose:
            c_h, s_h, c_l, s_l = (t.T for t in (ch, sh, cl, sl))
            cos = c_h[:hi_steps, None, :] * c_l[None, :, :] - s_h[:hi_steps, None, :] * s_l[None, :, :]
            sin = s_h[:hi_steps, None, :] * c_l[None, :, :] + c_h[:hi_steps, None, :] * s_l[None, :, :]
            out = jnp.where(minus_sin[None, None, :], -sin, cos) * wgt[None, None, :]
            return out.reshape(hi_steps * sub, n)
        cos = ch[:, :hi_steps, None] * cl[:, None, :] - sh[:, :hi_steps, None] * sl[:, None, :]
        sin = sh[:, :hi_steps, None] * cl[:, None, :] + ch[:, :hi_steps, None] * sl[:, None, :]
        return jnp.where(minus_sin[:, None, None], -sin, cos).reshape(n, hi_steps * sub)

    return table(n // sub, False), table(L // sub, False), table(L // sub, True), FT


def _hy_fwd_body(ah_ref, al_ref, u_ref, kf_ref, yh_ref, yl_ref, *, FT):
    t = pl.program_id(2)
    uh, ul = _split_bf16(u_ref[...])
    x = _dot3(ah_ref[...], al_ref[...], uh, ul)
    kf = kf_ref[...]
    xr, xi, kr, ki = x[:FT], x[FT:], kf[:FT], kf[FT:]
    row = lax.broadcasted_iota(jnp.int32, xr.shape, 0)
    packed = (row + t) == 0
    yr = jnp.where(packed, xr * kr, xr * kr - xi * ki)
    yi = jnp.where(packed, xi * ki, xr * ki + xi * kr)
    y = jnp.concatenate([yr, yi], axis=0)
    yh, yl = _split_bf16(y)
    yh_ref[...] = yh
    yl_ref[...] = yl


def _hy_spec_body(ah_ref, al_ref, u_ref, o_ref):
    uh, ul = _split_bf16(u_ref[...])
    o_ref[...] = _dot3(ah_ref[...], al_ref[...], uh, ul)


def hyena_spectrum(fh, fl, kern, FT):
    n, W = kern.shape
    tn = _tile(W, 256)
    return pl.pallas_call(
        _hy_spec_body,
        out_shape=jax.ShapeDtypeStruct((n, W), F32),
        grid=(W // tn, n // (2 * FT)),
        in_specs=[pl.BlockSpec((2 * FT, n), lambda j, t: (t, 0)),
                  pl.BlockSpec((2 * FT, n), lambda j, t: (t, 0)),
                  pl.BlockSpec((n, tn), lambda j, t: (0, j))],
        out_specs=pl.BlockSpec((2 * FT, tn), lambda j, t: (t, j)),
        compiler_params=_params("parallel", "arbitrary"),
        name="hyena_spectrum",
    )(fh, fl, kern)


def hyena_fwd(fh, fl, vg, kf, B, Lq, FT):
    W = vg.shape[1]
    n = 2 * Lq
    nt = n // (2 * FT)
    tn = _tile(W, 256)
    out = jax.ShapeDtypeStruct((B * n, W), BF16)
    ospec = pl.BlockSpec((2 * FT, tn), lambda b, j, t: (b * nt + t, j))
    return pl.pallas_call(
        functools.partial(_hy_fwd_body, FT=FT),
        out_shape=(out, out),
        grid=(B, W // tn, nt),
        in_specs=[pl.BlockSpec((2 * FT, Lq), lambda b, j, t: (t, 0)),
                  pl.BlockSpec((2 * FT, Lq), lambda b, j, t: (t, 0)),
                  pl.BlockSpec((Lq, tn), lambda b, j, t: (b, j)),
                  pl.BlockSpec((2 * FT, tn), lambda b, j, t: (t, j))],
        out_specs=(ospec, ospec),
        compiler_params=_params("parallel", "parallel", "arbitrary"),
        name="hyena_fwd",
    )(fh, fl, vg, kf)


def _hy_inv_body(gh_ref, gl_ref, yh_ref, yl_ref, vg_ref, x0_ref, bias_ref, o_ref):
    y = _dot3(gh_ref[...], gl_ref[...], yh_ref[...], yl_ref[...])
    o_ref[...] = (x0_ref[...] * (y + vg_ref[...] * bias_ref[...])).astype(BF16)


def hyena_inv(gh, gl, yh, yl, vg, x0u, bias, lay, Lq, row0):
    W = vg.shape[1]
    n = 2 * Lq
    B = lay.B
    tt = lay.gr
    ntt = Lq // tt
    tn = _tile(W, 256)
    g0 = row0 // tt
    return pl.pallas_call(
        _hy_inv_body,
        out_shape=jax.ShapeDtypeStruct((B * Lq, W), BF16),
        grid=(B, W // tn, ntt),
        in_specs=[pl.BlockSpec((tt, n), lambda b, j, i: (i, 0)),
                  pl.BlockSpec((tt, n), lambda b, j, i: (i, 0)),
                  pl.BlockSpec((n, tn), lambda b, j, i: (b, j)),
                  pl.BlockSpec((n, tn), lambda b, j, i: (b, j)),
                  pl.BlockSpec((tt, tn), lambda b, j, i: (b * ntt + i, j)),
                  pl.BlockSpec((tt, tn), lambda b, j, i: (b * lay.NG + g0 + i, j)),
                  pl.BlockSpec((1, tn), lambda b, j, i: (0, j))],
        out_specs=pl.BlockSpec((tt, tn), lambda b, j, i: (b * ntt + i, j)),
        compiler_params=_params("parallel", "parallel", "arbitrary"),
        name="hyena_inv",
    )(gh, gl, yh, yl, vg, x0u, bias.reshape(1, W))


def _hyena_features(L):
    bands = (HY_EMB - 1) // 2
    t = jnp.linspace(0.0, 1.0, L, dtype=F32)[:, None]
    wpos = 2.0 * math.pi * jnp.arange(L, dtype=F32)[:, None] / L
    fb = jnp.linspace(1e-4, bands - 1, bands, dtype=F32)[None]
    z = jnp.concatenate([t, jnp.cos(fb * wpos), -jnp.sin(fb * wpos)], axis=-1)
    z2 = jnp.concatenate([z, z[::-1]], axis=0)
    return jnp.pad(z2, ((0, 0), (0, LANES - HY_EMB)))


def hyena_stream(vg, x0u, bias, wts, lay, Lq, row0):
    w1p, b1, w2, b2, w3, freq, deltas2 = wts
    fwd_full, fwd_half, inv, FT = _dft_tables(Lq)
    kh, kl = _split_bf16(fwd_full)
    fh, fl = _split_bf16(fwd_half)
    gh, gl = _split_bf16(inv)
    kern = hyena_filter(_hyena_features(Lq), w1p, b1, w2, b2, w3, freq, deltas2, Lq)
    kf = hyena_spectrum(kh, kl, kern, FT)
    yh, yl = hyena_fwd(fh, fl, vg, kf, lay.B, Lq, FT)
    return hyena_inv(gh, gl, yh, yl, vg, x0u, bias, lay, Lq, row0)


def _rope_tables(lay):
    n_freq = DIFF_DIM // 4
    inv = ROPE_THETA ** (-jnp.arange(n_freq, dtype=F32) / n_freq)
    rows = lay.L // GRID_W
    r = jnp.repeat(jnp.arange(rows, dtype=F32), GRID_W)
    col = jnp.tile(jnp.arange(GRID_W, dtype=F32), rows)
    ang = jnp.concatenate([r[:, None] * inv, col[:, None] * inv], axis=-1)
    cos, sin = jnp.cos(ang), jnp.sin(ang)
    cos = jnp.concatenate([jnp.ones((lay.Lc, DIFF_DIM // 2), F32), cos], axis=0)
    sin = jnp.concatenate([jnp.zeros((lay.Lc, DIFF_DIM // 2), F32), sin], axis=0)
    cos_t = jnp.tile(cos, (1, 4))
    sin_t = jnp.tile(jnp.concatenate([-sin, sin], axis=-1), (1, 2))
    return cos_t, sin_t


def _main_cols(W, H):
    small0 = 7 * W
    small1 = small0 + 4 * H
    return small0, small1


def mixers(px1, px2, ps, lay, W, layer, need_ctx, p):
    H = W // HEAD_DIM
    lam_init = 0.8 - 0.6 * math.exp(-0.3 * layer)
    q_rot, k_rot = attn_prep(px1, p["cos_t"], p["sin_t"], p["qk_g2"], lay, W)
    att = attention(q_rot, k_rot, px1, p["attn_lambda"], p["attn_subln_g"], lay, W, lam_init)
    gdn = gated_deltanet(px1, ps, p["gdn_conv_w"], p["alog"], p["dtb"], p["gdn_norm_g"], lay, W, 3 * H)
    hg = hgrn2(px2, p["hg_lb_logits"], p["hg_norm_g"], lay, W, 0, layer)
    vgx, vge, x0u = hyena_prep(px2, p["hy_conv_w"], p["hy_conv_b"], lay, W, 5 * H)
    hy_x = hyena_stream(vgx, x0u, p["hy_bias"], p["hy_wts"], lay, lay.L, lay.Lc)
    if need_ctx:
        hy_e = hyena_stream(vge, x0u, p["hy_bias"], p["hy_wts"], lay, lay.Lc, 0)
    else:
        hy_e = jnp.zeros((lay.B, lay.Lc, W), BF16)
    hy = jnp.concatenate([hy_e.reshape(lay.B, lay.Lc, W), hy_x.reshape(lay.B, lay.L, W)], axis=1)
    return att, gdn, hg, hy.reshape(lay.T, W)


def _layer_params(l, lay, W, attn_qk_g, attn_lambda, attn_subln_g, gdn_conv_w, gdn_a_log, gdn_dt_bias, gdn_norm_g,
                  hg_lb_logits, hg_norm_g, hy_conv_w, hy_conv_b, hy_w1, hy_b1, hy_w2, hy_b2, hy_w3, hy_freq,
                  hy_bias, cos_t, sin_t):
    H = W // HEAD_DIM
    pad16 = lambda a: jnp.pad(a.astype(F32).reshape(1, 2 * H), ((0, 0), (0, LANES - 2 * H)))
    deltas = jnp.abs(jnp.linspace(math.log(HY_TARGET) / HY_SLOW_DECAY, math.log(HY_TARGET) / HY_FAST_DECAY, W, dtype=F32))
    return dict(
        cos_t=cos_t, sin_t=sin_t,
        qk_g2=jnp.tile(attn_qk_g[l].astype(F32), (1, 2)),
        attn_lambda=attn_lambda[l].astype(F32), attn_subln_g=attn_subln_g[l].astype(F32),
        gdn_conv_w=gdn_conv_w[l], alog=pad16(gdn_a_log[l]), dtb=pad16(gdn_dt_bias[l]), gdn_norm_g=gdn_norm_g[l],
        hg_lb_logits=hg_lb_logits.astype(F32).reshape(hg_lb_logits.shape[0], 2 * W), hg_norm_g=hg_norm_g[l],
        hy_conv_w=hy_conv_w[l], hy_conv_b=hy_conv_b[l].reshape(1, 3 * W), hy_bias=hy_bias[l].astype(F32),
        hy_wts=(jnp.pad(hy_w1[l].astype(F32), ((0, LANES - HY_EMB), (0, 0))), hy_b1[l].astype(F32),
                hy_w2[l].astype(F32), hy_b2[l].astype(F32), hy_w3[l].astype(F32), hy_freq[l].astype(F32),
                jnp.tile(deltas, 2).reshape(1, 2 * W)),
    )


def kernel(x, c, ctx, c_ctx, norm_g, w_mod, b_mod, ffn_w_in, ffn_w_out, w_in, attn_qk_g, attn_lambda, attn_subln_g, gdn_conv_w, gdn_a_log, gdn_dt_bias, gdn_norm_g, hg_lb_logits, hg_norm_g, hy_conv_w, hy_conv_b, hy_w1, hy_b1, hy_w2, hy_b2, hy_w3, hy_freq, hy_bias, w_gate, w_up, w_out):
    B, L, D = x.shape
    Lc = ctx.shape[1]
    depth = w_mod.shape[0]
    W = D // N_BRANCH
    H = W // HEAD_DIM
    lay = Layout(B, L, Lc)
    h = jnp.concatenate([ctx, x], axis=1).reshape(lay.T, D)
    G = -(-(B + 1) // 8) * 8
    c_all = jnp.concatenate([c, c_ctx[None], jnp.zeros((G - B - 1, D), F32)], axis=0)
    cos_t, sin_t = _rope_tables(lay)
    small0, small1 = _main_cols(W, H)
    for l in range(depth):
        last = l == depth - 1
        p = _layer_params(l, lay, W, attn_qk_g, attn_lambda, attn_subln_g, gdn_conv_w, gdn_a_log, gdn_dt_bias,
                          gdn_norm_g, hg_lb_logits, hg_norm_g, hy_conv_w, hy_conv_b, hy_w1, hy_b1, hy_w2, hy_b2,
                          hy_w3, hy_freq, hy_bias, cos_t, sin_t)
        mod = mod_table(c_all, w_mod, b_mod, l).reshape(G, MOD_CHUNKS, D)
        act = mm_swiglu(norm_mod(h, norm_g[l, 0], mod, lay, 0), ffn_w_in, (l, 0))
        h = mm_resid(act, ffn_w_out, (l, 0), h, mod, lay, 2, 0.5)
        xn = norm_mod(h, norm_g[l, 1], mod, lay, 1)
        px1 = mm_plain(xn, w_in, (l,), n_cols=small0)
        px2 = mm_plain(xn, w_in[l][:, small1:])
        ps = mm_plain(xn, jnp.pad(w_in[l][:, small0:small1], ((0, 0), (0, LANES - (small1 - small0)))), tn_pref=LANES)
        branches = mixers(px1, px2, ps, lay, W, l, not last, p)
        acc = merge_branches(xn, branches, w_gate[l].astype(BF16), w_up[l].astype(BF16))
        h = mm_resid(acc, w_out, (l,), h, mod, lay, 5, 1.0)
        act = mm_swiglu(norm_mod(h, norm_g[l, 2], mod, lay, 2), ffn_w_in, (l, 1))
        h = mm_resid(act, ffn_w_out, (l, 1), h, mod, lay, 8, 0.5)
    return h.reshape(B, lay.S, D)[:, Lc:, :]
```

```python
import functools
import math

import jax
import jax.numpy as jnp
from jax import lax
from jax.experimental import pallas as pl
from jax.experimental.pallas import tpu as pltpu

F32 = jnp.float32
BF16 = jnp.bfloat16

GRID_W = 64
N_BRANCH = 4
HEAD_DIM = 128
DIFF_DIM = HEAD_DIM // 2
MOD_CHUNKS = 9
ROPE_THETA = 10000.0
CHUNK = 64
HY_EMB = 33
HY_FAST_DECAY = 0.3
HY_SLOW_DECAY = 1.5
HY_TARGET = 1e-2
EPS = 1e-6
MASK_NEG = -1e30
LB_FLOOR = 1e-20

LANES = 128
V7X_VMEM_LIMIT = 56 * 1024 * 1024
HIGHEST = lax.Precision.HIGHEST

_NT = (((1,), (1,)), ((), ()))
_TN = (((0,), (0,)), ((), ()))


def _params(*sem):
    return pltpu.CompilerParams(dimension_semantics=sem, vmem_limit_bytes=V7X_VMEM_LIMIT)


def _tile(n, pref):
    t = min(n, pref)
    while n % t:
        t //= 2
    return t


def _wspec(at, K, tn, col):
    return pl.BlockSpec((None,) * len(at) + (K, tn), lambda *ids: tuple(at) + (0, col(*ids)))


def _sigmoid(x):
    return 1.0 / (1.0 + jnp.exp(-x))


def _silu(x):
    return x * _sigmoid(x)


def _softplus(x):
    return jnp.maximum(x, 0.0) + jnp.log1p(jnp.exp(-jnp.abs(x)))


class Layout:
    def __init__(self, B, L, Lc):
        self.B, self.L, self.Lc = B, L, Lc
        self.S = L + Lc
        self.T = B * self.S
        self.gr = math.gcd(L, Lc)
        self.NG = self.S // self.gr
        self.NGc = Lc // self.gr


def _mod_row(mod_ref, lay, r, k):
    b = r // lay.NG
    g = jnp.where(r - b * lay.NG < lay.NGc, lay.B, b)
    return mod_ref[pl.ds(g, 1), k:k + 1, :][0]


def _mod_body(c_ref, w_ref, b_ref, o_ref):
    a = _silu(c_ref[...]).astype(BF16)
    o_ref[...] = jnp.dot(a, w_ref[...].astype(BF16), preferred_element_type=F32) + b_ref[...]


def mod_table(c16, w_mod, b_mod, l):
    R, D = c16.shape
    N = w_mod.shape[-1]
    tn = _tile(N, 512)
    return pl.pallas_call(
        _mod_body,
        out_shape=jax.ShapeDtypeStruct((R, N), F32),
        grid=(N // tn,),
        in_specs=[pl.BlockSpec((R, D), lambda j: (0, 0)),
                  _wspec((l,), D, tn, lambda j: j),
                  _wspec((l,), 1, tn, lambda j: j)],
        out_specs=pl.BlockSpec((R, tn), lambda j: (0, j)),
        compiler_params=_params("parallel"),
        name="mod_table",
    )(c16, w_mod, b_mod.reshape(b_mod.shape[0], 1, N))


def _norm_mod_body(x_ref, g_ref, mod_ref, o_ref, *, lay, k):
    i = pl.program_id(0)
    gr = lay.gr
    n_sub = x_ref.shape[0] // gr
    for s in range(n_sub):
        x = x_ref[s * gr:(s + 1) * gr, :]
        y = x * lax.rsqrt(jnp.mean(x * x, axis=-1, keepdims=True) + EPS) * g_ref[...]
        r = i * n_sub + s
        shift = _mod_row(mod_ref, lay, r, 3 * k)
        scale = _mod_row(mod_ref, lay, r, 3 * k + 1)
        o_ref[s * gr:(s + 1) * gr, :] = (y * (1.0 + scale) + shift).astype(BF16)


def norm_mod(h, g, mod, lay, k):
    T, D = h.shape
    tm = lay.gr
    G = mod.shape[0]
    return pl.pallas_call(
        functools.partial(_norm_mod_body, lay=lay, k=k),
        out_shape=jax.ShapeDtypeStruct((T, D), BF16),
        grid=(T // tm,),
        in_specs=[pl.BlockSpec((tm, D), lambda i: (i, 0)),
                  pl.BlockSpec((1, D), lambda i: (0, 0)),
                  pl.BlockSpec((G, MOD_CHUNKS, D), lambda i: (0, 0, 0))],
        out_specs=pl.BlockSpec((tm, D), lambda i: (i, 0)),
        compiler_params=_params("parallel"),
        name="norm_mod",
    )(h, g.reshape(1, D), mod)


def _mm_swiglu_body(a_ref, wg_ref, wu_ref, o_ref, wgb, wub):
    @pl.when(pl.program_id(1) == 0)
    def _():
        wgb[...] = wg_ref[...].astype(BF16)
        wub[...] = wu_ref[...].astype(BF16)

    a = a_ref[...]
    g = jnp.dot(a, wgb[...], preferred_element_type=F32)
    u = jnp.dot(a, wub[...], preferred_element_type=F32)
    o_ref[...] = (_silu(g) * u).astype(BF16)


def mm_swiglu(a, w, at):
    T, K = a.shape
    F = w.shape[-1] // 2
    tm, tn = _tile(T, 1024), _tile(F, 256)
    nj = F // tn
    return pl.pallas_call(
        _mm_swiglu_body,
        out_shape=jax.ShapeDtypeStruct((T, F), BF16),
        grid=(nj, T // tm),
        in_specs=[pl.BlockSpec((tm, K), lambda j, i: (i, 0)),
                  _wspec(at, K, tn, lambda j, i: j),
                  _wspec(at, K, tn, lambda j, i: j + nj)],
        out_specs=pl.BlockSpec((tm, tn), lambda j, i: (i, j)),
        scratch_shapes=[pltpu.VMEM((K, tn), BF16), pltpu.VMEM((K, tn), BF16)],
        compiler_params=_params("parallel", "arbitrary"),
        name="mm_swiglu",
    )(a, w, w)


def _mm_resid_body(a_ref, w_ref, h_ref, mod_ref, o_ref, wb, *, lay, k, scale):
    i = pl.program_id(1)

    @pl.when(i == 0)
    def _():
        wb[...] = w_ref[...].astype(BF16)

    y = jnp.dot(a_ref[...], wb[...], preferred_element_type=F32)
    gr = lay.gr
    n_sub = a_ref.shape[0] // gr
    for s in range(n_sub):
        gate = _mod_row(mod_ref, lay, i * n_sub + s, k)
        sl = slice(s * gr, (s + 1) * gr)
        o_ref[sl, :] = h_ref[sl, :] + (scale * gate) * y[sl, :]


def mm_resid(a, w, at, h, mod, lay, k, scale):
    T, K = a.shape
    N = w.shape[-1]
    G = mod.shape[0]
    tm, tn = _tile(T, 512), _tile(N, 512)
    tm = max(tm, lay.gr)
    return pl.pallas_call(
        functools.partial(_mm_resid_body, lay=lay, k=k, scale=scale),
        out_shape=jax.ShapeDtypeStruct((T, N), F32),
        grid=(N // tn, T // tm),
        in_specs=[pl.BlockSpec((tm, K), lambda j, i: (i, 0)),
                  _wspec(at, K, tn, lambda j, i: j),
                  pl.BlockSpec((tm, tn), lambda j, i: (i, j)),
                  pl.BlockSpec((G, MOD_CHUNKS, tn), lambda j, i: (0, 0, j))],
        out_specs=pl.BlockSpec((tm, tn), lambda j, i: (i, j)),
        scratch_shapes=[pltpu.VMEM((K, tn), BF16)],
        compiler_params=_params("parallel", "arbitrary"),
        name="mm_resid",
    )(a, w, h, mod)


def _mm_plain_body(a_ref, w_ref, o_ref, wb):
    @pl.when(pl.program_id(1) == 0)
    def _():
        wb[...] = w_ref[...].astype(BF16)

    o_ref[...] = jnp.dot(a_ref[...], wb[...], preferred_element_type=F32)


def mm_plain(a, w, at=(), n_cols=None, tn_pref=512):
    T, K = a.shape
    N = n_cols or w.shape[-1]
    tm, tn = _tile(T, 1024), _tile(N, tn_pref)
    return pl.pallas_call(
        _mm_plain_body,
        out_shape=jax.ShapeDtypeStruct((T, N), F32),
        grid=(N // tn, T // tm),
        in_specs=[pl.BlockSpec((tm, K), lambda j, i: (i, 0)),
                  _wspec(at, K, tn, lambda j, i: j)],
        out_specs=pl.BlockSpec((tm, tn), lambda j, i: (i, j)),
        scratch_shapes=[pltpu.VMEM((K, tn), BF16)],
        compiler_params=_params("parallel", "arbitrary"),
        name="mm_plain",
    )(a, w)


def _merge_body(xn_ref, o0_ref, o1_ref, o2_ref, o3_ref, wg_ref, wu_ref, out_ref):
    xn = xn_ref[...]
    acc = None
    for br, o_ref in enumerate((o0_ref, o1_ref, o2_ref, o3_ref)):
        g = jnp.dot(xn, wg_ref[br], preferred_element_type=F32)
        u = jnp.dot(o_ref[...], wu_ref[br], preferred_element_type=F32)
        term = _sigmoid(g) * u
        acc = term if acc is None else acc + term
    out_ref[...] = acc.astype(BF16)


def merge_branches(xn, branches, w_gate_b, w_up_b, l):
    T, D = xn.shape
    W = branches[0].shape[1]
    tm, tn = _tile(T, 512), _tile(D, 256)
    bspec = pl.BlockSpec((tm, W), lambda i, j: (i, 0))
    return pl.pallas_call(
        _merge_body,
        out_shape=jax.ShapeDtypeStruct((T, D), BF16),
        grid=(T // tm, D // tn),
        in_specs=[pl.BlockSpec((tm, D), lambda i, j: (i, 0)), bspec, bspec, bspec, bspec,
                  pl.BlockSpec((None, N_BRANCH, D, tn), lambda i, j: (l, 0, 0, j)),
                  pl.BlockSpec((None, N_BRANCH, W, tn), lambda i, j: (l, 0, 0, j))],
        out_specs=pl.BlockSpec((tm, tn), lambda i, j: (i, j)),
        compiler_params=_params("parallel", "arbitrary"),
        name="merge_branches",
    )(xn, *branches, w_gate_b, w_up_b)


def _attn_prep_body(q_ref, k_ref, cos_ref, sin_ref, g_ref, qo_ref, ko_ref):
    cos = cos_ref[...]
    sin = sin_ref[...]
    lane = lax.broadcasted_iota(jnp.int32, cos.shape, 1)
    lo = lane < DIFF_DIM
    first = (lane % DIFF_DIM) < DIFF_DIM // 2
    n_heads = q_ref.shape[1] // HEAD_DIM
    for src, dst, gi, sc in ((q_ref, qo_ref, 0, DIFF_DIM ** -0.5), (k_ref, ko_ref, 1, 1.0)):
        g = g_ref[gi:gi + 1, :]
        for h in range(n_heads):
            cs = slice(h * HEAD_DIM, (h + 1) * HEAD_DIM)
            t = src[:, cs]
            ss = t * t
            s_lo = jnp.sum(jnp.where(lo, ss, 0.0), axis=-1, keepdims=True)
            s_hi = jnp.sum(jnp.where(lo, 0.0, ss), axis=-1, keepdims=True)
            inv = jnp.where(lo, lax.rsqrt(s_lo / DIFF_DIM + EPS), lax.rsqrt(s_hi / DIFF_DIM + EPS))
            y = t * inv * g
            partner = jnp.where(first, pltpu.roll(y, HEAD_DIM - DIFF_DIM // 2, 1), pltpu.roll(y, DIFF_DIM // 2, 1))
            dst[:, cs] = ((y * cos + partner * sin) * sc).astype(BF16)


def attn_prep(px, cos_t, sin_t, qk_g2, lay, W):
    T = px.shape[0]
    tr = lay.gr
    return pl.pallas_call(
        _attn_prep_body,
        out_shape=(jax.ShapeDtypeStruct((T, W), BF16), jax.ShapeDtypeStruct((T, W), BF16)),
        grid=(T // tr,),
        in_specs=[pl.BlockSpec((tr, W), lambda i: (i, 0)),
                  pl.BlockSpec((tr, W), lambda i: (i, 1)),
                  pl.BlockSpec((tr, LANES), lambda i: (i % lay.NG, 0)),
                  pl.BlockSpec((tr, LANES), lambda i: (i % lay.NG, 0)),
                  pl.BlockSpec((2, LANES), lambda i: (0, 0))],
        out_specs=(pl.BlockSpec((tr, W), lambda i: (i, 0)), pl.BlockSpec((tr, W), lambda i: (i, 0))),
        compiler_params=_params("parallel"),
        name="attn_prep",
    )(px, px, cos_t, sin_t, qk_g2)


def _attn_body(q_ref, k_ref, v_ref, lam_ref, g_ref, o_ref, *, lay, lam_init):
    qi = pl.program_id(2)
    lp = lam_ref[...]
    lam = (jnp.exp(jnp.sum(lp[0:1] * lp[1:2], axis=-1, keepdims=True))
           - jnp.exp(jnp.sum(lp[2:3] * lp[3:4], axis=-1, keepdims=True)) + lam_init)

    def run(nk):
        q = q_ref[...]
        k = k_ref[0:nk, :]
        v = v_ref[0:nk, :].astype(BF16)
        lane = lax.broadcasted_iota(jnp.int32, q.shape, 1)
        zero = jnp.zeros_like(q)
        outs = []
        for qq in (jnp.where(lane < DIFF_DIM, q, zero), jnp.where(lane < DIFF_DIM, zero, q)):
            s = lax.dot_general(qq, k, _NT, preferred_element_type=F32)
            p = jnp.exp(s - jnp.max(s, axis=-1, keepdims=True))
            l = jnp.sum(p, axis=-1, keepdims=True)
            outs.append(jnp.dot(p.astype(BF16), v, preferred_element_type=F32) / l)
        o = outs[0] - lam * outs[1]
        y = o * lax.rsqrt(jnp.mean(o * o, axis=-1, keepdims=True) + EPS) * g_ref[...]
        o_ref[...] = (y * (1.0 - lam_init)).astype(BF16)

    @pl.when(qi < lay.NGc)
    def _():
        run(lay.Lc)

    @pl.when(qi >= lay.NGc)
    def _():
        run(lay.S)


def attention(q_rot, k_rot, px, lam_p, subln_g, lay, W, lam_init):
    T = px.shape[0]
    H = W // HEAD_DIM
    tq = lay.gr
    S = lay.S
    vb = W // HEAD_DIM * 2
    return pl.pallas_call(
        functools.partial(_attn_body, lay=lay, lam_init=lam_init),
        out_shape=jax.ShapeDtypeStruct((T, W), BF16),
        grid=(lay.B, H, lay.NG),
        in_specs=[pl.BlockSpec((tq, HEAD_DIM), lambda b, h, i: (b * lay.NG + i, h)),
                  pl.BlockSpec((S, HEAD_DIM), lambda b, h, i: (b, h)),
                  pl.BlockSpec((S, HEAD_DIM), lambda b, h, i: (b, vb + h)),
                  pl.BlockSpec((4, DIFF_DIM), lambda b, h, i: (0, 0)),
                  pl.BlockSpec((1, HEAD_DIM), lambda b, h, i: (0, 0))],
        out_specs=pl.BlockSpec((tq, HEAD_DIM), lambda b, h, i: (b * lay.NG + i, h)),
        compiler_params=_params("parallel", "parallel", "arbitrary"),
        name="diff_attention",
    )(q_rot, k_rot, px, lam_p, subln_g.reshape(1, HEAD_DIM))


def _chunk_scan(y, reverse):
    S = y.shape[0]
    pos = lax.broadcasted_iota(jnp.int32, y.shape, 0) % CHUNK
    sh = 1
    while sh < CHUNK:
        if reverse:
            y = y + jnp.where(pos < CHUNK - sh, pltpu.roll(y, S - sh, 0), 0.0)
        else:
            y = y + jnp.where(pos >= sh, pltpu.roll(y, sh, 0), 0.0)
        sh *= 2
    return y


def _chunk_cumsum(y, reverse):
    S, C = y.shape
    nc = S // CHUNK
    ii = lax.broadcasted_iota(jnp.int32, (nc, CHUNK, CHUNK), 1)
    jj = lax.broadcasted_iota(jnp.int32, (nc, CHUNK, CHUNK), 2)
    tri = jnp.where((ii <= jj) if reverse else (ii >= jj), 1.0, 0.0).astype(BF16)
    y3 = y.reshape(nc, CHUNK, C)
    p1 = y3.astype(BF16)
    r1 = y3 - p1.astype(F32)
    p2 = r1.astype(BF16)
    p3 = (r1 - p2.astype(F32)).astype(BF16)
    out = _bmm(tri, p1) + _bmm(tri, p2) + _bmm(tri, p3)
    return out.reshape(S, C)


def _chunk_of_step(n, d, lay):
    ncc = lay.Lc // CHUNK
    nc = lay.S // CHUNK
    if d == 0:
        return n
    return jnp.where(n < ncc, ncc - 1 - n, nc - 1 - (n - ncc))


def _gated_head_norm(o, z, g):
    y = o * lax.rsqrt(jnp.mean(o * o, axis=-1, keepdims=True) + EPS) * g
    return (y * _silu(z)).astype(BF16)


def _centred_conv(x, w, prev_ok, next_ok):
    S = x.shape[0]
    xp = jnp.where(prev_ok, pltpu.roll(x, 1, 0), 0.0)
    xn = jnp.where(next_ok, pltpu.roll(x, S - 1, 0), 0.0)
    return xp * w[0:1] + x * w[1:2] + xn * w[2:3]


def _conv_masks(shape, lay):
    row = lax.broadcasted_iota(jnp.int32, shape, 0)
    prev_ok = (row * (row - lay.Lc)) != 0
    next_ok = ((row - (lay.Lc - 1)) * (row - (lay.S - 1))) != 0
    return prev_ok, next_ok


TRI_BLOCK = 8


def _bmm(x, y):
    return jnp.einsum('gij,gjk->gik', x, y, preferred_element_type=F32)


def _bmm_nt(x, y):
    return jnp.einsum('gik,gjk->gij', x, y, preferred_element_type=F32)


def _bmm_tn(x, y):
    return jnp.einsum('gjk,gjl->gkl', x, y, preferred_element_type=F32)


def _column_spread_matrix():
    m = jnp.arange(CHUNK)[:, None]
    c = jnp.arange(TRI_BLOCK * LANES)[None, :]
    j, lane = c // LANES, c % LANES
    hit = (lane < CHUNK) & (m % TRI_BLOCK == j) & (m // TRI_BLOCK == lane // TRI_BLOCK)
    return hit.astype(BF16)


def _unit_tri_inverse(a, spread, ii, jj, d):
    G = a.shape[0]
    nb = TRI_BLOCK
    a_d = jnp.where(ii // nb == jj // nb, a, 0.0).reshape(G * CHUNK, CHUNK)
    cols = jnp.dot(a_d.astype(BF16), spread, preferred_element_type=F32)
    row = lax.broadcasted_iota(jnp.int32, (G, CHUNK, LANES), 1)
    lane = lax.broadcasted_iota(jnp.int32, (G, CHUNK, LANES), 2)
    x = jnp.where(row == lane, 1.0, 0.0)
    for j in (range(nb - 1) if d == 0 else range(nb - 1, 0, -1)):
        col_j = cols[:, j * LANES:(j + 1) * LANES].reshape(G, CHUNK, LANES)
        x4 = x.reshape(G, CHUNK // nb, nb, LANES)
        row_j = jnp.broadcast_to(x4[:, :, j:j + 1, :], x4.shape).reshape(G, CHUNK, LANES)
        x = x - col_j * row_j
    t = x[:, :, :CHUNK]
    width = nb
    while width < CHUNK:
        e = jnp.where(((ii // width) ^ (jj // width)) == 1, a, 0.0).astype(BF16)
        tb = t.astype(BF16)
        t = t - _bmm(tb, _bmm(e, tb).astype(BF16))
        width *= 2
    return t


def _gdn_gates_body(ps_ref, alog_ref, dtb_ref, o_ref, *, n_heads):
    p = ps_ref[...]
    lane = lax.broadcasted_iota(jnp.int32, p.shape, 1)
    g_all = -jnp.exp(alog_ref[...]) * _softplus(p + dtb_ref[...])
    o_ref[...] = jnp.where(lane < n_heads, _chunk_scan(g_all, False),
                           jnp.where(lane < 2 * n_heads, _chunk_scan(g_all, True), _sigmoid(p)))


def gdn_gates(ps, alog_l, dtb_l, lay, H):
    T = ps.shape[0]
    S = lay.S
    one = pl.BlockSpec((1, LANES), lambda b: (0, 0))
    return pl.pallas_call(
        functools.partial(_gdn_gates_body, n_heads=H),
        out_shape=jax.ShapeDtypeStruct((T, LANES), F32),
        grid=(lay.B,),
        in_specs=[pl.BlockSpec((S, LANES), lambda b: (b, 0)), one, one],
        out_specs=pl.BlockSpec((S, LANES), lambda b: (b, 0)),
        compiler_params=_params("parallel"),
        name="gdn_gates",
    )(ps, alog_l, dtb_l)


def _gdn_body(q_ref, k_ref, v_ref, z_ref, ps_ref, cq_ref, ck_ref, cv_ref, g_ref, sel_ref, o_ref,
              qs, ks, vs, gc, bb, p_s, r_s, k_s, n_s, dl_s, o_s, *, lay, n_heads, group):
    h = pl.program_id(1)
    S = lay.S
    nc = S // CHUNK
    shape = (S, HEAD_DIM)
    prev_ok, next_ok = _conv_masks(shape, lay)
    lane = lax.broadcasted_iota(jnp.int32, shape, 1)

    q = _silu(_centred_conv(q_ref[...], cq_ref[...], prev_ok, next_ok))
    qs[...] = q * lax.rsqrt(jnp.sum(q * q, axis=-1, keepdims=True) + EPS) * HEAD_DIM ** -0.5
    k = _silu(_centred_conv(k_ref[...], ck_ref[...], prev_ok, next_ok))
    ks[...] = k * lax.rsqrt(jnp.sum(k * k, axis=-1, keepdims=True) + EPS)
    vs[...] = _silu(_centred_conv(v_ref[...], cv_ref[...], prev_ok, next_ok))

    p = ps_ref[...]
    for d in range(2):
        gsel = jnp.sum(jnp.where(lane == d * n_heads + h, p, 0.0), axis=-1, keepdims=True)
        gc[d] = jnp.broadcast_to(gsel, shape)
        bsel = jnp.sum(jnp.where(lane == (2 + d) * n_heads + h, p, 0.0), axis=-1, keepdims=True)
        bb[d] = jnp.broadcast_to(bsel, shape)

    G = group
    ii = lax.broadcasted_iota(jnp.int32, (G, CHUNK, CHUNK), 1)
    jj = lax.broadcasted_iota(jnp.int32, (G, CHUNK, CHUNK), 2)
    bmm, bmm_nt, bmm_tn = _bmm, _bmm_nt, _bmm_tn

    def prep(gi, d):
        rows = G * CHUNK
        sl = pl.ds(pl.multiple_of(gi * rows, rows), rows)
        cs = pl.ds(gi * G, G)
        r3 = lambda x: x.reshape(G, CHUNK, HEAD_DIM)
        q3, k3, v3, gcb, b3 = r3(qs[sl, :]), r3(ks[sl, :]), r3(vs[sl, :]), r3(gc[d, sl, :]), r3(bb[d, sl, :])
        glast = gcb[:, CHUNK - 1:CHUNK, :] if d == 0 else gcb[:, 0:1, :]
        eg = jnp.exp(gcb)
        kb = k3 * b3
        diff = gcb[:, :, :CHUNK] - jnp.swapaxes(gcb, 1, 2)[:, :CHUNK, :]
        incl = (ii >= jj) if d == 0 else (ii <= jj)
        strict = (ii > jj) if d == 0 else (ii < jj)
        decay = jnp.exp(jnp.where(incl, diff, MASK_NEG))
        kbf = k3.astype(BF16)
        a = jnp.where(strict, bmm_nt(kb.astype(BF16), kbf) * decay, 0.0)
        tb = _unit_tri_inverse(a, sel_ref[...], ii, jj, d).astype(BF16)
        ub = bmm(tb, (v3 * b3).astype(BF16)).astype(BF16)
        wb = bmm(tb, (kb * eg).astype(BF16)).astype(BF16)
        attb = (bmm_nt(q3.astype(BF16), kbf) * decay).astype(BF16)
        kgb = (k3 * jnp.exp(glast - gcb)).astype(BF16)
        p_s[d, sl, :] = (q3 * eg - bmm(attb, wb)).astype(BF16).reshape(rows, HEAD_DIM)
        r_s[d, sl, :] = bmm(attb, ub).reshape(rows, HEAD_DIM)
        k_s[d, cs, :, :] = bmm_tn(kgb, wb).astype(BF16)
        n_s[d, cs, :, :] = bmm_tn(kgb, ub)
        dl_s[d, cs, :, :] = jnp.broadcast_to(jnp.exp(glast), (G, 8, HEAD_DIM))

    def prep_loop(gi, carry):
        prep(gi, 0)
        prep(gi, 1)
        return carry

    lax.fori_loop(0, nc // G, prep_loop, 0)

    def step_d(d, n, s):
        c = _chunk_of_step(n, d, lay)
        sl = pl.ds(pl.multiple_of(c * CHUNK, CHUNK), CHUNK)
        sb = s.astype(BF16)
        o_s[d, sl, :] = r_s[d, sl, :] + jnp.dot(p_s[d, sl, :], sb, preferred_element_type=F32)
        dl = dl_s[d, pl.ds(c, 1), :, :][0][0:1, :]
        return (s * dl + n_s[d, pl.ds(c, 1), :, :][0]
                - jnp.dot(k_s[d, pl.ds(c, 1), :, :][0], sb, preferred_element_type=F32))

    def both(n, carry):
        return (step_d(0, n, carry[0]), step_d(1, n, carry[1]))

    zero = jnp.zeros((HEAD_DIM, HEAD_DIM), F32)
    lax.fori_loop(0, nc, both, (zero, zero))
    o_ref[...] = _gated_head_norm(o_s[0] + o_s[1], z_ref[...], g_ref[...])


def gated_deltanet(px, gates, conv_w, norm_g, lay, W, col0):
    T = px.shape[0]
    H = W // HEAD_DIM
    S = lay.S
    nc = S // CHUNK

    def col(o):
        return pl.BlockSpec((S, HEAD_DIM), lambda b, h: (b, col0 + o * H + h))

    def cw(o):
        return pl.BlockSpec((3, HEAD_DIM), lambda b, h: (0, o * H + h))

    one = pl.BlockSpec((1, HEAD_DIM), lambda b, h: (0, 0))
    sh = (S, HEAD_DIM)
    st = (2, nc, HEAD_DIM, HEAD_DIM)
    group = max(g for g in range(1, 10) if nc % g == 0)
    return pl.pallas_call(
        functools.partial(_gdn_body, lay=lay, n_heads=H, group=group),
        out_shape=jax.ShapeDtypeStruct((T, W), BF16),
        grid=(lay.B, H),
        in_specs=[col(0), col(1), col(2), col(3),
                  pl.BlockSpec((S, LANES), lambda b, h: (b, 0)),
                  cw(0), cw(1), cw(2), one,
                  pl.BlockSpec((CHUNK, TRI_BLOCK * LANES), lambda b, h: (0, 0))],
        out_specs=pl.BlockSpec((S, HEAD_DIM), lambda b, h: (b, h)),
        scratch_shapes=[pltpu.VMEM(sh, F32), pltpu.VMEM(sh, F32), pltpu.VMEM(sh, F32),
                        pltpu.VMEM((2,) + sh, F32), pltpu.VMEM((2,) + sh, F32),
                        pltpu.VMEM((2,) + sh, BF16), pltpu.VMEM((2,) + sh, F32),
                        pltpu.VMEM(st, BF16), pltpu.VMEM(st, F32),
                        pltpu.VMEM((2, nc, 8, HEAD_DIM), F32), pltpu.VMEM((2,) + sh, F32)],
        compiler_params=_params("parallel", "arbitrary"),
        name="gated_deltanet",
    )(px, px, px, px, gates, conv_w, conv_w, conv_w, norm_g.reshape(1, HEAD_DIM), _column_spread_matrix())


def _hgrn_body(q_ref, f0_ref, f1_ref, i_ref, z_ref, lb0_ref, lb1_ref, g_ref, o_ref,
               gc, kk, o_s, *, lay, layer):
    S = lay.S
    nc = S // CHUNK
    for d, (f_ref, lb_ref) in enumerate(((f0_ref, lb0_ref), (f1_ref, lb1_ref))):
        lg = lb_ref[...]
        depth = lg.shape[0]
        mx = lg[0:1]
        for r in range(1, depth):
            mx = jnp.maximum(mx, lg[r:r + 1])
        e = [jnp.exp(lg[r:r + 1] - mx) for r in range(depth)]
        tot = e[0]
        for r in range(1, depth):
            tot = tot + e[r]
        lb = jnp.zeros_like(mx)
        for r in range(1, layer + 1):
            lb = lb + e[r] / tot
        f = f_ref[...]
        e_neg = jnp.exp(-jnp.abs(f))
        big = 1.0 / (1.0 + e_neg)
        small = e_neg * big
        pos = f >= 0.0
        logf = jnp.log(jnp.maximum(lb, LB_FLOOR) + (1.0 - lb) * jnp.where(pos, big, small))
        gc[d] = _chunk_cumsum(logf, reverse=(d == 1))
        kk[d] = (1.0 - lb) * jnp.where(pos, small, big)

    SB = 16
    nsb = CHUNK // SB
    jrow = lax.broadcasted_iota(jnp.int32, (CHUNK, HEAD_DIM), 0)
    jsub = lax.broadcasted_iota(jnp.int32, (SB, HEAD_DIM), 0)
    alane = lax.broadcasted_iota(jnp.int32, (SB, CHUNK), 1)

    def intra_t(d, qc, kc, g):
        if d == 0:
            refs = [g[0:1, :]] + [g[SB * I - 1:SB * I, :] for I in range(1, nsb)]
            far = range(1, nsb)
        else:
            refs = [g[SB * (I + 1):SB * (I + 1) + 1, :] for I in range(nsb - 1)] + [g[CHUNK - 1:CHUNK, :]]
            far = range(nsb - 1)
        rvec = jnp.concatenate([jnp.broadcast_to(r, (SB, HEAD_DIM)) for r in refs], axis=0)
        qt = qc * jnp.exp(g - rvec)
        off = None
        for I in far:
            seen = (jrow < SB * I) if d == 0 else (jrow >= SB * (I + 1))
            kt = (kc * jnp.exp(jnp.where(seen, refs[I] - g, MASK_NEG))).astype(BF16)
            qi = jnp.where((jrow // SB) == I, qt, 0.0).astype(BF16)
            part = lax.dot_general(kt, qi, _NT, preferred_element_type=F32)
            off = part if off is None else off + part
        blocks = []
        for I in range(nsb):
            rs = slice(SB * I, SB * (I + 1))
            g_i, k_i = g[rs, :], kc[rs, :]
            acc = jnp.zeros((SB, CHUNK), F32)
            for r in range(SB):
                i = SB * I + r
                keep = (jsub <= r) if d == 0 else (jsub >= r)
                rel = jnp.exp(jnp.where(keep, g[i:i + 1, :] - g_i, MASK_NEG))
                colv = jnp.sum(rel * k_i * qc[i:i + 1, :], axis=-1, keepdims=True)
                acc = jnp.where(alane == i, colv, acc)
            blocks.append(acc)
        return jnp.concatenate(blocks, axis=0) + off

    def step_d(d, n, st):
        c = _chunk_of_step(n, d, lay)
        sl = pl.ds(pl.multiple_of(c * CHUNK, CHUNK), CHUNK)
        qc, kc, vc, g = q_ref[sl, :], kk[d, sl, :], i_ref[sl, :], gc[d, sl, :]
        glast = g[CHUNK - 1:CHUNK, :] if d == 0 else g[0:1, :]
        a_t = intra_t(d, qc, kc, g)
        vb = vc.astype(BF16)
        o = (lax.dot_general((qc * jnp.exp(g)).astype(BF16), st.astype(BF16), _NT, preferred_element_type=F32)
             + lax.dot_general(a_t.astype(BF16), vb, _TN, preferred_element_type=F32))
        o_s[d, sl, :] = o
        kg = (kc * jnp.exp(glast - g)).astype(BF16)
        return st * jnp.exp(glast) + lax.dot_general(vb, kg, _TN, preferred_element_type=F32)

    def both(n, carry):
        return (step_d(0, n, carry[0]), step_d(1, n, carry[1]))

    zero = jnp.zeros((HEAD_DIM, HEAD_DIM), F32)
    lax.fori_loop(0, nc, both, (zero, zero), unroll=4 if nc % 4 == 0 else 1)
    o_ref[...] = _gated_head_norm(o_s[0] + o_s[1], z_ref[...], g_ref[...])


def hgrn2(px, lb_logits, norm_g, lay, W, col0, layer):
    T = px.shape[0]
    H = W // HEAD_DIM
    S = lay.S
    depth = lb_logits.shape[0]

    def col(o):
        return pl.BlockSpec((S, HEAD_DIM), lambda b, h: (b, col0 + o * H + h))

    def lbs(d):
        return pl.BlockSpec((depth, HEAD_DIM), lambda b, h: (0, d * H + h))

    sh = (S, HEAD_DIM)
    return pl.pallas_call(
        functools.partial(_hgrn_body, lay=lay, layer=layer),
        out_shape=jax.ShapeDtypeStruct((T, W), BF16),
        grid=(lay.B, H),
        in_specs=[col(0), col(1), col(2), col(3), col(4), lbs(0), lbs(1),
                  pl.BlockSpec((1, HEAD_DIM), lambda b, h: (0, 0))],
        out_specs=pl.BlockSpec((S, HEAD_DIM), lambda b, h: (b, h)),
        scratch_shapes=[pltpu.VMEM((2,) + sh, F32), pltpu.VMEM((2,) + sh, F32), pltpu.VMEM((2,) + sh, F32)],
        compiler_params=_params("parallel", "arbitrary"),
        name="hgrn2",
    )(px, px, px, px, px, lb_logits, lb_logits, norm_g.reshape(1, HEAD_DIM))


def _split_bf16(x):
    hi = x.astype(BF16)
    return hi, (x - hi.astype(F32)).astype(BF16)


def _dot3(a_hi, a_lo, b_hi, b_lo):
    return (jnp.dot(a_hi, b_hi, preferred_element_type=F32)
            + jnp.dot(a_hi, b_lo, preferred_element_type=F32)
            + jnp.dot(a_lo, b_hi, preferred_element_type=F32))


def _hy_filter_body(z_ref, w1_ref, b1_ref, w2_ref, b2_ref, w3_ref, fr_ref, dl_ref, o_ref, *, L):
    i = pl.program_id(0)
    z = z_ref[...]
    tr = z.shape[0]
    W = o_ref.shape[1]
    h = jnp.sin(fr_ref[0:1, :] * (jnp.dot(z, w1_ref[...], precision=HIGHEST, preferred_element_type=F32) + b1_ref[...]))
    h = jnp.sin(fr_ref[1:2, :] * (jnp.dot(h, w2_ref[...], precision=HIGHEST, preferred_element_type=F32) + b2_ref[...]))
    h = jnp.dot(h, w3_ref[...], precision=HIGHEST, preferred_element_type=F32)
    h = h * jnp.exp(-z[:, 0:1] * dl_ref[...])
    row = lax.broadcasted_iota(jnp.int32, (tr, W), 0) + i * tr
    o_ref[...] = jnp.where(row < L, h[:, :W], jnp.where(row == L, 0.0, h[:, W:]))


def hyena_filter(z2, w1p, b1, w2, b2, w3, freq, deltas2, L):
    n = z2.shape[0]
    W = w3.shape[1] // 2
    O = w2.shape[0]
    tr = _tile(n, 512)
    full = lambda a: pl.BlockSpec(a.shape, lambda i: (0,) * a.ndim)
    args = (w1p, b1.reshape(1, O), w2, b2.reshape(1, O), w3, freq, deltas2)
    return pl.pallas_call(
        functools.partial(_hy_filter_body, L=L),
        out_shape=jax.ShapeDtypeStruct((n, W), F32),
        grid=(n // tr,),
        in_specs=[pl.BlockSpec((tr, LANES), lambda i: (i, 0))] + [full(a) for a in args],
        out_specs=pl.BlockSpec((tr, W), lambda i: (i, 0)),
        compiler_params=_params("parallel"),
        name="hyena_filter",
    )(z2, *args)


def _hy_prep_body(v_ref, x0_ref, x1_ref, cv_ref, c0_ref, c1_ref, bv_ref, b0_ref, b1_ref,
                  vgx_ref, vge_ref, x0u_ref, *, lay):
    shape = v_ref.shape
    prev_ok, next_ok = _conv_masks(shape, lay)
    v = _centred_conv(v_ref[...], cv_ref[...], prev_ok, next_ok) + bv_ref[...]
    x0 = _centred_conv(x0_ref[...], c0_ref[...], prev_ok, next_ok) + b0_ref[...]
    x1 = _centred_conv(x1_ref[...], c1_ref[...], prev_ok, next_ok) + b1_ref[...]
    vg = v * x1
    vge_ref[...] = vg[:lay.Lc, :]
    vgx_ref[...] = vg[lay.Lc:, :]
    x0u_ref[...] = x0


def hyena_prep(px, conv_w, conv_b, lay, W, col0):
    T = px.shape[0]
    H = W // HEAD_DIM
    S, L, Lc, B = lay.S, lay.L, lay.Lc, lay.B

    def col(o):
        return pl.BlockSpec((S, HEAD_DIM), lambda b, j: (b, col0 + o * H + j))

    def cw(o, rows):
        return pl.BlockSpec((rows, HEAD_DIM), lambda b, j: (0, o * H + j))

    return pl.pallas_call(
        functools.partial(_hy_prep_body, lay=lay),
        out_shape=(jax.ShapeDtypeStruct((B * L, W), F32), jax.ShapeDtypeStruct((B * Lc, W), F32),
                   jax.ShapeDtypeStruct((T, W), F32)),
        grid=(B, H),
        in_specs=[col(0), col(1), col(2), cw(0, 3), cw(1, 3), cw(2, 3), cw(0, 1), cw(1, 1), cw(2, 1)],
        out_specs=(pl.BlockSpec((L, HEAD_DIM), lambda b, j: (b, j)),
                   pl.BlockSpec((Lc, HEAD_DIM), lambda b, j: (b, j)),
                   pl.BlockSpec((S, HEAD_DIM), lambda b, j: (b, j))),
        compiler_params=_params("parallel", "parallel"),
        name="hyena_prep",
    )(px, px, px, conv_w, conv_w, conv_w, conv_b, conv_b, conv_b)


def _dft_tables(L):
    n = 2 * L
    FT = min(256, L)
    sub = 64
    idx = jnp.arange(n, dtype=jnp.int32)
    r = idx % (2 * FT)
    is_im = r >= FT
    f = (idx // (2 * FT)) * FT + jnp.where(is_im, r - FT, r)
    nyq = is_im & (f == 0)
    f = jnp.where(nyq, L, f)
    step = jnp.arange(sub, dtype=jnp.int32)
    ang_hi = (2.0 * math.pi / n) * ((f[:, None] * (step * sub)[None, :]) % n).astype(F32)
    ang_lo = (2.0 * math.pi / n) * ((f[:, None] * step[None, :]) % n).astype(F32)
    ch, sh, cl, sl = jnp.cos(ang_hi), jnp.sin(ang_hi), jnp.cos(ang_lo), jnp.sin(ang_lo)
    wgt = jnp.where((f == 0) | (f == L), 1.0 / n, 2.0 / n)
    minus_sin = is_im & ~nyq

    def table(hi_steps, transpose):
        if transpose:
            c_h, s_h, c_l, s_l = (t.T for t in (ch, sh, cl, sl))
            cos = c_h[:hi_steps, None, :] * c_l[None, :, :] - s_h[:hi_steps, None, :] * s_l[None, :, :]
            sin = s_h[:hi_steps, None, :] * c_l[None, :, :] + c_h[:hi_steps, None, :] * s_l[None, :, :]
            out = jnp.where(minus_sin[None, None, :], -sin, cos) * wgt[None, None, :]
            return out.reshape(hi_steps * sub, n)
        cos = ch[:, :hi_steps, None] * cl[:, None, :] - sh[:, :hi_steps, None] * sl[:, None, :]
        sin = sh[:, :hi_steps, None] * cl[:, None, :] + ch[:, :hi_steps, None] * sl[:, None, :]
        return jnp.where(minus_sin[:, None, None], -sin, cos).reshape(n, hi_steps * sub)

    return table(n // sub, False), table(L // sub, False), table(L // sub, True), FT


def _hy_fwd_body(ah_ref, al_ref, u_ref, kf_ref, yh_ref, yl_ref, *, FT):
    t = pl.program_id(2)
    uh, ul = _split_bf16(u_ref[...])
    x = _dot3(ah_ref[...], al_ref[...], uh, ul)
    kf = kf_ref[...]
    xr, xi, kr, ki = x[:FT], x[FT:], kf[:FT], kf[FT:]
    row = lax.broadcasted_iota(jnp.int32, xr.shape, 0)
    packed = (row + t) == 0
    yr = jnp.where(packed, xr * kr, xr * kr - xi * ki)
    yi = jnp.where(packed, xi * ki, xr * ki + xi * kr)
    y = jnp.concatenate([yr, yi], axis=0)
    yh, yl = _split_bf16(y)
    yh_ref[...] = yh
    yl_ref[...] = yl


def _hy_spec_body(ah_ref, al_ref, u_ref, o_ref):
    uh, ul = _split_bf16(u_ref[...])
    o_ref[...] = _dot3(ah_ref[...], al_ref[...], uh, ul)


def hyena_spectrum(fh, fl, kern, FT):
    n, W = kern.shape
    tn = _tile(W, 256)
    return pl.pallas_call(
        _hy_spec_body,
        out_shape=jax.ShapeDtypeStruct((n, W), F32),
        grid=(W // tn, n // (2 * FT)),
        in_specs=[pl.BlockSpec((2 * FT, n), lambda j, t: (t, 0)),
                  pl.BlockSpec((2 * FT, n), lambda j, t: (t, 0)),
                  pl.BlockSpec((n, tn), lambda j, t: (0, j))],
        out_specs=pl.BlockSpec((2 * FT, tn), lambda j, t: (t, j)),
        compiler_params=_params("parallel", "arbitrary"),
        name="hyena_spectrum",
    )(fh, fl, kern)


def hyena_fwd(fh, fl, vg, kf, B, Lq, FT):
    W = vg.shape[1]
    n = 2 * Lq
    nt = n // (2 * FT)
    tn = _tile(W, 256)
    out = jax.ShapeDtypeStruct((B * n, W), BF16)
    ospec = pl.BlockSpec((2 * FT, tn), lambda b, j, t: (b * nt + t, j))
    return pl.pallas_call(
        functools.partial(_hy_fwd_body, FT=FT),
        out_shape=(out, out),
        grid=(B, W // tn, nt),
        in_specs=[pl.BlockSpec((2 * FT, Lq), lambda b, j, t: (t, 0)),
                  pl.BlockSpec((2 * FT, Lq), lambda b, j, t: (t, 0)),
                  pl.BlockSpec((Lq, tn), lambda b, j, t: (b, j)),
                  pl.BlockSpec((2 * FT, tn), lambda b, j, t: (t, j))],
        out_specs=(ospec, ospec),
        compiler_params=_params("parallel", "parallel", "arbitrary"),
        name="hyena_fwd",
    )(fh, fl, vg, kf)


def _hy_inv_body(gh_ref, gl_ref, yh_ref, yl_ref, vg_ref, x0_ref, bias_ref, o_ref):
    y = _dot3(gh_ref[...], gl_ref[...], yh_ref[...], yl_ref[...])
    o_ref[...] = (x0_ref[...] * (y + vg_ref[...] * bias_ref[...])).astype(BF16)


def hyena_inv(gh, gl, yh, yl, vg, x0u, bias, lay, Lq, row0):
    W = vg.shape[1]
    n = 2 * Lq
    B = lay.B
    tt = lay.gr
    ntt = Lq // tt
    tn = _tile(W, 256)
    g0 = row0 // tt
    return pl.pallas_call(
        _hy_inv_body,
        out_shape=jax.ShapeDtypeStruct((B * Lq, W), BF16),
        grid=(B, W // tn, ntt),
        in_specs=[pl.BlockSpec((tt, n), lambda b, j, i: (i, 0)),
                  pl.BlockSpec((tt, n), lambda b, j, i: (i, 0)),
                  pl.BlockSpec((n, tn), lambda b, j, i: (b, j)),
                  pl.BlockSpec((n, tn), lambda b, j, i: (b, j)),
                  pl.BlockSpec((tt, tn), lambda b, j, i: (b * ntt + i, j)),
                  pl.BlockSpec((tt, tn), lambda b, j, i: (b * lay.NG + g0 + i, j)),
                  pl.BlockSpec((1, tn), lambda b, j, i: (0, j))],
        out_specs=pl.BlockSpec((tt, tn), lambda b, j, i: (b * ntt + i, j)),
        compiler_params=_params("parallel", "parallel", "arbitrary"),
        name="hyena_inv",
    )(gh, gl, yh, yl, vg, x0u, bias.reshape(1, W))


def _hyena_features(L):
    bands = (HY_EMB - 1) // 2
    t = jnp.linspace(0.0, 1.0, L, dtype=F32)[:, None]
    wpos = 2.0 * math.pi * jnp.arange(L, dtype=F32)[:, None] / L
    fb = jnp.linspace(1e-4, bands - 1, bands, dtype=F32)[None]
    z = jnp.concatenate([t, jnp.cos(fb * wpos), -jnp.sin(fb * wpos)], axis=-1)
    z2 = jnp.concatenate([z, z[::-1]], axis=0)
    return jnp.pad(z2, ((0, 0), (0, LANES - HY_EMB)))


def hyena_stream(vg, x0u, bias, wts, lay, Lq, row0):
    w1p, b1, w2, b2, w3, freq, deltas2 = wts
    fwd_full, fwd_half, inv, FT = _dft_tables(Lq)
    kh, kl = _split_bf16(fwd_full)
    fh, fl = _split_bf16(fwd_half)
    gh, gl = _split_bf16(inv)
    kern = hyena_filter(_hyena_features(Lq), w1p, b1, w2, b2, w3, freq, deltas2, Lq)
    kf = hyena_spectrum(kh, kl, kern, FT)
    yh, yl = hyena_fwd(fh, fl, vg, kf, lay.B, Lq, FT)
    return hyena_inv(gh, gl, yh, yl, vg, x0u, bias, lay, Lq, row0)


def _rope_tables(lay):
    n_freq = DIFF_DIM // 4
    inv = ROPE_THETA ** (-jnp.arange(n_freq, dtype=F32) / n_freq)
    rows = lay.L // GRID_W
    r = jnp.repeat(jnp.arange(rows, dtype=F32), GRID_W)
    col = jnp.tile(jnp.arange(GRID_W, dtype=F32), rows)
    ang = jnp.concatenate([r[:, None] * inv, col[:, None] * inv], axis=-1)
    cos, sin = jnp.cos(ang), jnp.sin(ang)
    cos = jnp.concatenate([jnp.ones((lay.Lc, DIFF_DIM // 2), F32), cos], axis=0)
    sin = jnp.concatenate([jnp.zeros((lay.Lc, DIFF_DIM // 2), F32), sin], axis=0)
    cos_t = jnp.tile(cos, (1, 4))
    sin_t = jnp.tile(jnp.concatenate([-sin, sin], axis=-1), (1, 2))
    return cos_t, sin_t


def _main_cols(W, H):
    small0 = 7 * W
    small1 = small0 + 4 * H
    return small0, small1


def mixers(px1, px2, ps, lay, W, layer, need_ctx, p):
    H = W // HEAD_DIM
    lam_init = 0.8 - 0.6 * math.exp(-0.3 * layer)
    q_rot, k_rot = attn_prep(px1, p["cos_t"], p["sin_t"], p["qk_g2"], lay, W)
    att = attention(q_rot, k_rot, px1, p["attn_lambda"], p["attn_subln_g"], lay, W, lam_init)
    gates = gdn_gates(ps, p["alog"], p["dtb"], lay, H)
    gdn = gated_deltanet(px1, gates, p["gdn_conv_w"], p["gdn_norm_g"], lay, W, 3 * H)
    hg = hgrn2(px2, p["hg_lb_logits"], p["hg_norm_g"], lay, W, 0, layer)
    vgx, vge, x0u = hyena_prep(px2, p["hy_conv_w"], p["hy_conv_b"], lay, W, 5 * H)
    hy_x = hyena_stream(vgx, x0u, p["hy_bias"], p["hy_wts"], lay, lay.L, lay.Lc)
    if need_ctx:
        hy_e = hyena_stream(vge, x0u, p["hy_bias"], p["hy_wts"], lay, lay.Lc, 0)
    else:
        hy_e = jnp.zeros((lay.B, lay.Lc, W), BF16)
    hy = jnp.concatenate([hy_e.reshape(lay.B, lay.Lc, W), hy_x.reshape(lay.B, lay.L, W)], axis=1)
    return att, gdn, hg, hy.reshape(lay.T, W)


def _layer_params(l, lay, W, attn_qk_g, attn_lambda, attn_subln_g, gdn_conv_w, gdn_a_log, gdn_dt_bias, gdn_norm_g,
                  hg_lb_logits, hg_norm_g, hy_conv_w, hy_conv_b, hy_w1, hy_b1, hy_w2, hy_b2, hy_w3, hy_freq,
                  hy_bias, cos_t, sin_t):
    H = W // HEAD_DIM
    pad16 = lambda a: jnp.pad(a.astype(F32).reshape(1, 2 * H), ((0, 0), (0, LANES - 2 * H)))
    deltas = jnp.abs(jnp.linspace(math.log(HY_TARGET) / HY_SLOW_DECAY, math.log(HY_TARGET) / HY_FAST_DECAY, W, dtype=F32))
    return dict(
        cos_t=cos_t, sin_t=sin_t,
        qk_g2=jnp.tile(attn_qk_g[l].astype(F32), (1, 2)),
        attn_lambda=attn_lambda[l].astype(F32), attn_subln_g=attn_subln_g[l].astype(F32),
        gdn_conv_w=gdn_conv_w[l], alog=pad16(gdn_a_log[l]), dtb=pad16(gdn_dt_bias[l]), gdn_norm_g=gdn_norm_g[l],
        hg_lb_logits=hg_lb_logits.astype(F32).reshape(hg_lb_logits.shape[0], 2 * W), hg_norm_g=hg_norm_g[l],
        hy_conv_w=hy_conv_w[l], hy_conv_b=hy_conv_b[l].reshape(1, 3 * W), hy_bias=hy_bias[l].astype(F32),
        hy_wts=(jnp.pad(hy_w1[l].astype(F32), ((0, LANES - HY_EMB), (0, 0))), hy_b1[l].astype(F32),
                hy_w2[l].astype(F32), hy_b2[l].astype(F32), hy_w3[l].astype(F32), hy_freq[l].astype(F32),
                jnp.tile(deltas, 2).reshape(1, 2 * W)),
    )


def kernel(x, c, ctx, c_ctx, norm_g, w_mod, b_mod, ffn_w_in, ffn_w_out, w_in, attn_qk_g, attn_lambda, attn_subln_g, gdn_conv_w, gdn_a_log, gdn_dt_bias, gdn_norm_g, hg_lb_logits, hg_norm_g, hy_conv_w, hy_conv_b, hy_w1, hy_b1, hy_w2, hy_b2, hy_w3, hy_freq, hy_bias, w_gate, w_up, w_out):
    B, L, D = x.shape
    Lc = ctx.shape[1]
    depth = w_mod.shape[0]
    W = D // N_BRANCH
    H = W // HEAD_DIM
    lay = Layout(B, L, Lc)
    h = jnp.concatenate([ctx, x], axis=1).reshape(lay.T, D)
    G = -(-(B + 1) // 8) * 8
    c_all = jnp.concatenate([c, c_ctx[None], jnp.zeros((G - B - 1, D), F32)], axis=0)
    cos_t, sin_t = _rope_tables(lay)
    small0, small1 = _main_cols(W, H)
    w_gate_b, w_up_b = w_gate.astype(BF16), w_up.astype(BF16)
    for l in range(depth):
        last = l == depth - 1
        p = _layer_params(l, lay, W, attn_qk_g, attn_lambda, attn_subln_g, gdn_conv_w, gdn_a_log, gdn_dt_bias,
                          gdn_norm_g, hg_lb_logits, hg_norm_g, hy_conv_w, hy_conv_b, hy_w1, hy_b1, hy_w2, hy_b2,
                          hy_w3, hy_freq, hy_bias, cos_t, sin_t)
        mod = mod_table(c_all, w_mod, b_mod, l).reshape(G, MOD_CHUNKS, D)
        act = mm_swiglu(norm_mod(h, norm_g[l, 0], mod, lay, 0), ffn_w_in, (l, 0))
        h = mm_resid(act, ffn_w_out, (l, 0), h, mod, lay, 2, 0.5)
        xn = norm_mod(h, norm_g[l, 1], mod, lay, 1)
        cols = lambda lo, hi: lax.slice(w_in, (l, 0, lo), (l + 1, D, hi))
        px1 = mm_plain(xn, cols(0, small0), (0,))
        px2 = mm_plain(xn, cols(small1, w_in.shape[-1]), (0,))
        ps = mm_plain(xn, jnp.pad(cols(small0, small1), ((0, 0), (0, 0), (0, LANES - (small1 - small0)))), (0,),
                      tn_pref=LANES)
        branches = mixers(px1, px2, ps, lay, W, l, not last, p)
        acc = merge_branches(xn, branches, w_gate_b, w_up_b, l)
        h = mm_resid(acc, w_out, (l,), h, mod, lay, 5, 1.0)
        act = mm_swiglu(norm_mod(h, norm_g[l, 2], mod, lay, 2), ffn_w_in, (l, 1))
        h = mm_resid(act, ffn_w_out, (l, 1), h, mod, lay, 8, 0.5)
    return h.reshape(B, lay.S, D)[:, Lc:, :]
```

```python
import functools
import math

import jax
import jax.numpy as jnp
from jax import lax
from jax.experimental import pallas as pl
from jax.experimental.pallas import tpu as pltpu

F32 = jnp.float32
BF16 = jnp.bfloat16

GRID_W = 64
N_BRANCH = 4
HEAD_DIM = 128
DIFF_DIM = HEAD_DIM // 2
MOD_CHUNKS = 9
ROPE_THETA = 10000.0
CHUNK = 64
HY_EMB = 33
HY_FAST_DECAY = 0.3
HY_SLOW_DECAY = 1.5
HY_TARGET = 1e-2
EPS = 1e-6
MASK_NEG = -1e30
LB_FLOOR = 1e-20

LANES = 128
V7X_VMEM_LIMIT = 56 * 1024 * 1024
HIGHEST = lax.Precision.HIGHEST

_NT = (((1,), (1,)), ((), ()))
_TN = (((0,), (0,)), ((), ()))


def _params(*sem):
    return pltpu.CompilerParams(dimension_semantics=sem, vmem_limit_bytes=V7X_VMEM_LIMIT)


def _tile(n, pref):
    t = min(n, pref)
    while n % t:
        t //= 2
    return t


def _wspec(at, K, tn, col):
    return pl.BlockSpec((None,) * len(at) + (K, tn), lambda *ids: tuple(at) + (0, col(*ids)))


def _sigmoid(x):
    return 1.0 / (1.0 + jnp.exp(-x))


def _silu(x):
    return x * _sigmoid(x)


def _softplus(x):
    return jnp.maximum(x, 0.0) + jnp.log1p(jnp.exp(-jnp.abs(x)))


class Layout:
    def __init__(self, B, L, Lc, gr=None):
        self.B, self.L, self.Lc = B, L, Lc
        self.S = L + Lc
        self.T = B * self.S
        self.gr = gr or math.gcd(L, Lc)
        self.NG = self.S // self.gr
        self.NGc = Lc // self.gr


def _mod_row(mod_ref, lay, r, k):
    b = r // lay.NG
    g = jnp.where(r - b * lay.NG < lay.NGc, lay.B, b)
    return mod_ref[pl.ds(g, 1), k:k + 1, :][0]


def _mod_body(c_ref, w_ref, b_ref, o_ref):
    a = _silu(c_ref[...]).astype(BF16)
    o_ref[...] = jnp.dot(a, w_ref[...].astype(BF16), preferred_element_type=F32) + b_ref[...]


def mod_table(c16, w_mod, b_mod, l):
    R, D = c16.shape
    N = w_mod.shape[-1]
    tn = _tile(N, 512)
    return pl.pallas_call(
        _mod_body,
        out_shape=jax.ShapeDtypeStruct((R, N), F32),
        grid=(N // tn,),
        in_specs=[pl.BlockSpec((R, D), lambda j: (0, 0)),
                  _wspec((l,), D, tn, lambda j: j),
                  _wspec((l,), 1, tn, lambda j: j)],
        out_specs=pl.BlockSpec((R, tn), lambda j: (0, j)),
        compiler_params=_params("parallel"),
        name="mod_table",
    )(c16, w_mod, b_mod.reshape(b_mod.shape[0], 1, N))


def _norm_mod_body(x_ref, g_ref, mod_ref, o_ref, *, lay, k):
    i = pl.program_id(0)
    gr = lay.gr
    n_sub = x_ref.shape[0] // gr
    for s in range(n_sub):
        x = x_ref[s * gr:(s + 1) * gr, :]
        y = x * lax.rsqrt(jnp.mean(x * x, axis=-1, keepdims=True) + EPS) * g_ref[...]
        r = i * n_sub + s
        shift = _mod_row(mod_ref, lay, r, 3 * k)
        scale = _mod_row(mod_ref, lay, r, 3 * k + 1)
        o_ref[s * gr:(s + 1) * gr, :] = (y * (1.0 + scale) + shift).astype(BF16)


def norm_mod(h, g, mod, lay, k):
    T, D = h.shape
    tm = lay.gr
    G = mod.shape[0]
    return pl.pallas_call(
        functools.partial(_norm_mod_body, lay=lay, k=k),
        out_shape=jax.ShapeDtypeStruct((T, D), BF16),
        grid=(T // tm,),
        in_specs=[pl.BlockSpec((tm, D), lambda i: (i, 0)),
                  pl.BlockSpec((1, D), lambda i: (0, 0)),
                  pl.BlockSpec((G, MOD_CHUNKS, D), lambda i: (0, 0, 0))],
        out_specs=pl.BlockSpec((tm, D), lambda i: (i, 0)),
        compiler_params=_params("parallel"),
        name="norm_mod",
    )(h, g.reshape(1, D), mod)


def _mm_swiglu_body(a_ref, wg_ref, wu_ref, o_ref, wgb, wub):
    @pl.when(pl.program_id(1) == 0)
    def _():
        wgb[...] = wg_ref[...].astype(BF16)
        wub[...] = wu_ref[...].astype(BF16)

    a = a_ref[...]
    g = jnp.dot(a, wgb[...], preferred_element_type=F32)
    u = jnp.dot(a, wub[...], preferred_element_type=F32)
    o_ref[...] = (_silu(g) * u).astype(BF16)


def mm_swiglu(a, w, at):
    T, K = a.shape
    F = w.shape[-1] // 2
    tm, tn = _tile(T, 1024), _tile(F, 256)
    nj = F // tn
    return pl.pallas_call(
        _mm_swiglu_body,
        out_shape=jax.ShapeDtypeStruct((T, F), BF16),
        grid=(nj, T // tm),
        in_specs=[pl.BlockSpec((tm, K), lambda j, i: (i, 0)),
                  _wspec(at, K, tn, lambda j, i: j),
                  _wspec(at, K, tn, lambda j, i: j + nj)],
        out_specs=pl.BlockSpec((tm, tn), lambda j, i: (i, j)),
        scratch_shapes=[pltpu.VMEM((K, tn), BF16), pltpu.VMEM((K, tn), BF16)],
        compiler_params=_params("parallel", "arbitrary"),
        name="mm_swiglu",
    )(a, w, w)


def _mm_resid_body(a_ref, w_ref, h_ref, mod_ref, o_ref, wb, *, lay, k, scale):
    i = pl.program_id(1)

    @pl.when(i == 0)
    def _():
        wb[...] = w_ref[...].astype(BF16)

    y = jnp.dot(a_ref[...], wb[...], preferred_element_type=F32)
    gr = lay.gr
    n_sub = a_ref.shape[0] // gr
    for s in range(n_sub):
        gate = _mod_row(mod_ref, lay, i * n_sub + s, k)
        sl = slice(s * gr, (s + 1) * gr)
        o_ref[sl, :] = h_ref[sl, :] + (scale * gate) * y[sl, :]


def mm_resid(a, w, at, h, mod, lay, k, scale):
    T, K = a.shape
    N = w.shape[-1]
    G = mod.shape[0]
    tn = _tile(N, 512)

    def vmem_bytes(tm):
        return 2 * tm * K * 2 + 2 * K * tn * 4 + K * tn * 2 + 4 * tm * tn * 4

    tm = _tile(T, 1024)
    if vmem_bytes(tm) > (V7X_VMEM_LIMIT * 7) // 8:
        tm = _tile(T, 512)
    tm = max(tm, lay.gr)
    return pl.pallas_call(
        functools.partial(_mm_resid_body, lay=lay, k=k, scale=scale),
        out_shape=jax.ShapeDtypeStruct((T, N), F32),
        grid=(N // tn, T // tm),
        in_specs=[pl.BlockSpec((tm, K), lambda j, i: (i, 0)),
                  _wspec(at, K, tn, lambda j, i: j),
                  pl.BlockSpec((tm, tn), lambda j, i: (i, j)),
                  pl.BlockSpec((G, MOD_CHUNKS, tn), lambda j, i: (0, 0, j))],
        out_specs=pl.BlockSpec((tm, tn), lambda j, i: (i, j)),
        scratch_shapes=[pltpu.VMEM((K, tn), BF16)],
        compiler_params=_params("parallel", "arbitrary"),
        name="mm_resid",
    )(a, w, h, mod)


def _mm_plain_body(a_ref, w_ref, o_ref, wb):
    @pl.when(pl.program_id(1) == 0)
    def _():
        wb[...] = w_ref[...].astype(BF16)

    o_ref[...] = jnp.dot(a_ref[...], wb[...], preferred_element_type=F32)


def mm_plain(a, w, at=(), n_cols=None, tn_pref=512):
    T, K = a.shape
    N = n_cols or w.shape[-1]
    tm, tn = _tile(T, 1024), _tile(N, tn_pref)
    return pl.pallas_call(
        _mm_plain_body,
        out_shape=jax.ShapeDtypeStruct((T, N), F32),
        grid=(N // tn, T // tm),
        in_specs=[pl.BlockSpec((tm, K), lambda j, i: (i, 0)),
                  _wspec(at, K, tn, lambda j, i: j)],
        out_specs=pl.BlockSpec((tm, tn), lambda j, i: (i, j)),
        scratch_shapes=[pltpu.VMEM((K, tn), BF16)],
        compiler_params=_params("parallel", "arbitrary"),
        name="mm_plain",
    )(a, w)


def _merge_body(xn_ref, o0_ref, o1_ref, o2_ref, o3_ref, wg_ref, wu_ref, out_ref):
    xn = xn_ref[...]
    acc = None
    for br, o_ref in enumerate((o0_ref, o1_ref, o2_ref, o3_ref)):
        g = jnp.dot(xn, wg_ref[br], preferred_element_type=F32)
        u = jnp.dot(o_ref[...], wu_ref[br], preferred_element_type=F32)
        term = _sigmoid(g) * u
        acc = term if acc is None else acc + term
    out_ref[...] = acc.astype(BF16)


def merge_branches(xn, branches, w_gate_b, w_up_b, l):
    T, D = xn.shape
    W = branches[0].shape[1]
    tm, tn = _tile(T, 512), _tile(D, 256)
    bspec = pl.BlockSpec((tm, W), lambda i, j: (i, 0))
    return pl.pallas_call(
        _merge_body,
        out_shape=jax.ShapeDtypeStruct((T, D), BF16),
        grid=(T // tm, D // tn),
        in_specs=[pl.BlockSpec((tm, D), lambda i, j: (i, 0)), bspec, bspec, bspec, bspec,
                  pl.BlockSpec((None, N_BRANCH, D, tn), lambda i, j: (l, 0, 0, j)),
                  pl.BlockSpec((None, N_BRANCH, W, tn), lambda i, j: (l, 0, 0, j))],
        out_specs=pl.BlockSpec((tm, tn), lambda i, j: (i, j)),
        compiler_params=_params("parallel", "arbitrary"),
        name="merge_branches",
    )(xn, *branches, w_gate_b, w_up_b)


def _attn_prep_body(q_ref, k_ref, cos_ref, sin_ref, g_ref, qo_ref, ko_ref):
    cos = cos_ref[...]
    sin = sin_ref[...]
    lane = lax.broadcasted_iota(jnp.int32, cos.shape, 1)
    lo = lane < DIFF_DIM
    first = (lane % DIFF_DIM) < DIFF_DIM // 2
    n_heads = q_ref.shape[1] // HEAD_DIM
    for src, dst, gi, sc in ((q_ref, qo_ref, 0, DIFF_DIM ** -0.5), (k_ref, ko_ref, 1, 1.0)):
        g = g_ref[gi:gi + 1, :]
        for h in range(n_heads):
            cs = slice(h * HEAD_DIM, (h + 1) * HEAD_DIM)
            t = src[:, cs]
            ss = t * t
            s_lo = jnp.sum(jnp.where(lo, ss, 0.0), axis=-1, keepdims=True)
            s_hi = jnp.sum(jnp.where(lo, 0.0, ss), axis=-1, keepdims=True)
            inv = jnp.where(lo, lax.rsqrt(s_lo / DIFF_DIM + EPS), lax.rsqrt(s_hi / DIFF_DIM + EPS))
            y = t * inv * g
            partner = jnp.where(first, pltpu.roll(y, HEAD_DIM - DIFF_DIM // 2, 1), pltpu.roll(y, DIFF_DIM // 2, 1))
            dst[:, cs] = ((y * cos + partner * sin) * sc).astype(BF16)


def attn_prep(px, cos_t, sin_t, qk_g2, lay, W):
    T = px.shape[0]
    tr = lay.gr
    return pl.pallas_call(
        _attn_prep_body,
        out_shape=(jax.ShapeDtypeStruct((T, W), BF16), jax.ShapeDtypeStruct((T, W), BF16)),
        grid=(T // tr,),
        in_specs=[pl.BlockSpec((tr, W), lambda i: (i, 0)),
                  pl.BlockSpec((tr, W), lambda i: (i, 1)),
                  pl.BlockSpec((tr, LANES), lambda i: (i % lay.NG, 0)),
                  pl.BlockSpec((tr, LANES), lambda i: (i % lay.NG, 0)),
                  pl.BlockSpec((2, LANES), lambda i: (0, 0))],
        out_specs=(pl.BlockSpec((tr, W), lambda i: (i, 0)), pl.BlockSpec((tr, W), lambda i: (i, 0))),
        compiler_params=_params("parallel"),
        name="attn_prep",
    )(px, px, cos_t, sin_t, qk_g2)


def _attn_body(q_ref, k_ref, v_ref, lam_ref, g_ref, o_ref, *, lay, lam_init):
    qi = pl.program_id(2)
    lp = lam_ref[...]
    lam = (jnp.exp(jnp.sum(lp[0:1] * lp[1:2], axis=-1, keepdims=True))
           - jnp.exp(jnp.sum(lp[2:3] * lp[3:4], axis=-1, keepdims=True)) + lam_init)

    def run(nk):
        q = q_ref[...]
        k = k_ref[0:nk, :]
        v = v_ref[0:nk, :].astype(BF16)
        lane = lax.broadcasted_iota(jnp.int32, q.shape, 1)
        zero = jnp.zeros_like(q)
        outs = []
        for qq in (jnp.where(lane < DIFF_DIM, q, zero), jnp.where(lane < DIFF_DIM, zero, q)):
            s = lax.dot_general(qq, k, _NT, preferred_element_type=F32)
            p = jnp.exp(s - jnp.max(s, axis=-1, keepdims=True))
            l = jnp.sum(p, axis=-1, keepdims=True)
            outs.append(jnp.dot(p.astype(BF16), v, preferred_element_type=F32) / l)
        o = outs[0] - lam * outs[1]
        y = o * lax.rsqrt(jnp.mean(o * o, axis=-1, keepdims=True) + EPS) * g_ref[...]
        o_ref[...] = (y * (1.0 - lam_init)).astype(BF16)

    @pl.when(qi < lay.NGc)
    def _():
        run(lay.Lc)

    @pl.when(qi >= lay.NGc)
    def _():
        run(lay.S)


def attention(q_rot, k_rot, px, lam_p, subln_g, lay, W, lam_init):
    T = px.shape[0]
    H = W // HEAD_DIM
    tq = lay.gr
    S = lay.S
    vb = W // HEAD_DIM * 2
    return pl.pallas_call(
        functools.partial(_attn_body, lay=lay, lam_init=lam_init),
        out_shape=jax.ShapeDtypeStruct((T, W), BF16),
        grid=(lay.B, H, lay.NG),
        in_specs=[pl.BlockSpec((tq, HEAD_DIM), lambda b, h, i: (b * lay.NG + i, h)),
                  pl.BlockSpec((S, HEAD_DIM), lambda b, h, i: (b, h)),
                  pl.BlockSpec((S, HEAD_DIM), lambda b, h, i: (b, vb + h)),
                  pl.BlockSpec((4, DIFF_DIM), lambda b, h, i: (0, 0)),
                  pl.BlockSpec((1, HEAD_DIM), lambda b, h, i: (0, 0))],
        out_specs=pl.BlockSpec((tq, HEAD_DIM), lambda b, h, i: (b * lay.NG + i, h)),
        compiler_params=_params("parallel", "parallel", "arbitrary"),
        name="diff_attention",
    )(q_rot, k_rot, px, lam_p, subln_g.reshape(1, HEAD_DIM))


def _chunk_scan(y, reverse):
    S = y.shape[0]
    pos = lax.broadcasted_iota(jnp.int32, y.shape, 0) % CHUNK
    sh = 1
    while sh < CHUNK:
        if reverse:
            y = y + jnp.where(pos < CHUNK - sh, pltpu.roll(y, S - sh, 0), 0.0)
        else:
            y = y + jnp.where(pos >= sh, pltpu.roll(y, sh, 0), 0.0)
        sh *= 2
    return y


def _chunk_cumsum(y, reverse):
    S, C = y.shape
    nc = S // CHUNK
    ii = lax.broadcasted_iota(jnp.int32, (nc, CHUNK, CHUNK), 1)
    jj = lax.broadcasted_iota(jnp.int32, (nc, CHUNK, CHUNK), 2)
    tri = jnp.where((ii <= jj) if reverse else (ii >= jj), 1.0, 0.0).astype(BF16)
    y3 = y.reshape(nc, CHUNK, C)
    p1 = y3.astype(BF16)
    r1 = y3 - p1.astype(F32)
    p2 = r1.astype(BF16)
    p3 = (r1 - p2.astype(F32)).astype(BF16)
    out = _bmm(tri, p1) + _bmm(tri, p2) + _bmm(tri, p3)
    return out.reshape(S, C)


def _chunk_of_step(n, d, lay):
    ncc = lay.Lc // CHUNK
    nc = lay.S // CHUNK
    if d == 0:
        return n
    return jnp.where(n < ncc, ncc - 1 - n, nc - 1 - (n - ncc))


def _gated_head_norm(o, z, g):
    y = o * lax.rsqrt(jnp.mean(o * o, axis=-1, keepdims=True) + EPS) * g
    return (y * _silu(z)).astype(BF16)


def _centred_conv(x, w, prev_ok, next_ok):
    S = x.shape[0]
    xp = jnp.where(prev_ok, pltpu.roll(x, 1, 0), 0.0)
    xn = jnp.where(next_ok, pltpu.roll(x, S - 1, 0), 0.0)
    return xp * w[0:1] + x * w[1:2] + xn * w[2:3]


def _conv_masks(shape, lay):
    row = lax.broadcasted_iota(jnp.int32, shape, 0)
    prev_ok = (row * (row - lay.Lc)) != 0
    next_ok = ((row - (lay.Lc - 1)) * (row - (lay.S - 1))) != 0
    return prev_ok, next_ok


TRI_BLOCK = 8


def _bmm(x, y):
    return jnp.einsum('gij,gjk->gik', x, y, preferred_element_type=F32)


def _bmm_nt(x, y):
    return jnp.einsum('gik,gjk->gij', x, y, preferred_element_type=F32)


def _bmm_tn(x, y):
    return jnp.einsum('gjk,gjl->gkl', x, y, preferred_element_type=F32)


def _column_spread_matrix():
    m = jnp.arange(CHUNK)[:, None]
    c = jnp.arange(TRI_BLOCK * LANES)[None, :]
    j, lane = c // LANES, c % LANES
    hit = (lane < CHUNK) & (m % TRI_BLOCK == j) & (m // TRI_BLOCK == lane // TRI_BLOCK)
    return hit.astype(BF16)


def _unit_tri_inverse(a, spread, ii, jj, d):
    G = a.shape[0]
    nb = TRI_BLOCK
    a_d = jnp.where(ii // nb == jj // nb, a, 0.0).reshape(G * CHUNK, CHUNK)
    cols = jnp.dot(a_d.astype(BF16), spread, preferred_element_type=F32)
    row = lax.broadcasted_iota(jnp.int32, (G, CHUNK, LANES), 1)
    lane = lax.broadcasted_iota(jnp.int32, (G, CHUNK, LANES), 2)
    x = jnp.where(row == lane, 1.0, 0.0)
    for j in (range(nb - 1) if d == 0 else range(nb - 1, 0, -1)):
        col_j = cols[:, j * LANES:(j + 1) * LANES].reshape(G, CHUNK, LANES)
        x4 = x.reshape(G, CHUNK // nb, nb, LANES)
        row_j = jnp.broadcast_to(x4[:, :, j:j + 1, :], x4.shape).reshape(G, CHUNK, LANES)
        x = x - col_j * row_j
    t = x[:, :, :CHUNK]
    width = nb
    while width < CHUNK:
        e = jnp.where(((ii // width) ^ (jj // width)) == 1, a, 0.0).astype(BF16)
        tb = t.astype(BF16)
        t = t - _bmm(tb, _bmm(e, tb).astype(BF16))
        width *= 2
    return t


def _gdn_gates_body(ps_ref, alog_ref, dtb_ref, o_ref, *, n_heads):
    p = ps_ref[...]
    lane = lax.broadcasted_iota(jnp.int32, p.shape, 1)
    g_all = -jnp.exp(alog_ref[...]) * _softplus(p + dtb_ref[...])
    o_ref[...] = jnp.where(lane < n_heads, _chunk_scan(g_all, False),
                           jnp.where(lane < 2 * n_heads, _chunk_scan(g_all, True), _sigmoid(p)))


def gdn_gates(ps, alog_l, dtb_l, lay, H):
    T = ps.shape[0]
    S = lay.S
    one = pl.BlockSpec((1, LANES), lambda b: (0, 0))
    return pl.pallas_call(
        functools.partial(_gdn_gates_body, n_heads=H),
        out_shape=jax.ShapeDtypeStruct((T, LANES), F32),
        grid=(lay.B,),
        in_specs=[pl.BlockSpec((S, LANES), lambda b: (b, 0)), one, one],
        out_specs=pl.BlockSpec((S, LANES), lambda b: (b, 0)),
        compiler_params=_params("parallel"),
        name="gdn_gates",
    )(ps, alog_l, dtb_l)


def _gdn_body(q_ref, k_ref, v_ref, z_ref, ps_ref, cq_ref, ck_ref, cv_ref, g_ref, sel_ref, o_ref,
              qs, ks, vs, gc, bb, p_s, r_s, k_s, n_s, dl_s, o_s, *, lay, n_heads, group):
    h = pl.program_id(1)
    S = lay.S
    nc = S // CHUNK
    shape = (S, HEAD_DIM)
    prev_ok, next_ok = _conv_masks(shape, lay)
    lane = lax.broadcasted_iota(jnp.int32, shape, 1)

    q = _silu(_centred_conv(q_ref[...], cq_ref[...], prev_ok, next_ok))
    qs[...] = q * lax.rsqrt(jnp.sum(q * q, axis=-1, keepdims=True) + EPS) * HEAD_DIM ** -0.5
    k = _silu(_centred_conv(k_ref[...], ck_ref[...], prev_ok, next_ok))
    ks[...] = k * lax.rsqrt(jnp.sum(k * k, axis=-1, keepdims=True) + EPS)
    vs[...] = _silu(_centred_conv(v_ref[...], cv_ref[...], prev_ok, next_ok))

    p = ps_ref[...]
    for d in range(2):
        gsel = jnp.sum(jnp.where(lane == d * n_heads + h, p, 0.0), axis=-1, keepdims=True)
        gc[d] = jnp.broadcast_to(gsel, shape)
        bsel = jnp.sum(jnp.where(lane == (2 + d) * n_heads + h, p, 0.0), axis=-1, keepdims=True)
        bb[d] = jnp.broadcast_to(bsel, shape)

    G = group
    ii = lax.broadcasted_iota(jnp.int32, (G, CHUNK, CHUNK), 1)
    jj = lax.broadcasted_iota(jnp.int32, (G, CHUNK, CHUNK), 2)
    bmm, bmm_nt, bmm_tn = _bmm, _bmm_nt, _bmm_tn

    def prep(gi, d):
        rows = G * CHUNK
        sl = pl.ds(pl.multiple_of(gi * rows, rows), rows)
        cs = pl.ds(gi * G, G)
        r3 = lambda x: x.reshape(G, CHUNK, HEAD_DIM)
        q3, k3, v3, gcb, b3 = r3(qs[sl, :]), r3(ks[sl, :]), r3(vs[sl, :]), r3(gc[d, sl, :]), r3(bb[d, sl, :])
        glast = gcb[:, CHUNK - 1:CHUNK, :] if d == 0 else gcb[:, 0:1, :]
        eg = jnp.exp(gcb)
        kb = k3 * b3
        diff = gcb[:, :, :CHUNK] - jnp.swapaxes(gcb, 1, 2)[:, :CHUNK, :]
        incl = (ii >= jj) if d == 0 else (ii <= jj)
        strict = (ii > jj) if d == 0 else (ii < jj)
        decay = jnp.exp(jnp.where(incl, diff, MASK_NEG))
        kbf = k3.astype(BF16)
        a = jnp.where(strict, bmm_nt(kb.astype(BF16), kbf) * decay, 0.0)
        tb = _unit_tri_inverse(a, sel_ref[...], ii, jj, d).astype(BF16)
        ub = bmm(tb, (v3 * b3).astype(BF16)).astype(BF16)
        wb = bmm(tb, (kb * eg).astype(BF16)).astype(BF16)
        attb = (bmm_nt(q3.astype(BF16), kbf) * decay).astype(BF16)
        kgb = (k3 * jnp.exp(glast - gcb)).astype(BF16)
        p_s[d, sl, :] = (q3 * eg - bmm(attb, wb)).astype(BF16).reshape(rows, HEAD_DIM)
        r_s[d, sl, :] = bmm(attb, ub).reshape(rows, HEAD_DIM)
        k_s[d, cs, :, :] = bmm_tn(kgb, wb).astype(BF16)
        n_s[d, cs, :, :] = bmm_tn(kgb, ub)
        dl_s[d, cs, :, :] = jnp.broadcast_to(jnp.exp(glast), (G, 8, HEAD_DIM))

    def prep_loop(gi, carry):
        prep(gi, 0)
        prep(gi, 1)
        return carry

    lax.fori_loop(0, nc // G, prep_loop, 0)

    def step_d(d, n, s):
        c = _chunk_of_step(n, d, lay)
        sl = pl.ds(pl.multiple_of(c * CHUNK, CHUNK), CHUNK)
        sb = s.astype(BF16)
        o_s[d, sl, :] = r_s[d, sl, :] + jnp.dot(p_s[d, sl, :], sb, preferred_element_type=F32)
        dl = dl_s[d, pl.ds(c, 1), :, :][0][0:1, :]
        return (s * dl + n_s[d, pl.ds(c, 1), :, :][0]
                - jnp.dot(k_s[d, pl.ds(c, 1), :, :][0], sb, preferred_element_type=F32))

    def both(n, carry):
        return (step_d(0, n, carry[0]), step_d(1, n, carry[1]))

    zero = jnp.zeros((HEAD_DIM, HEAD_DIM), F32)
    lax.fori_loop(0, nc, both, (zero, zero))
    o_ref[...] = _gated_head_norm(o_s[0] + o_s[1], z_ref[...], g_ref[...])


def gated_deltanet(px, gates, conv_w, norm_g, lay, W, col0):
    T = px.shape[0]
    H = W // HEAD_DIM
    S = lay.S
    nc = S // CHUNK

    def col(o):
        return pl.BlockSpec((S, HEAD_DIM), lambda b, h: (b, col0 + o * H + h))

    def cw(o):
        return pl.BlockSpec((3, HEAD_DIM), lambda b, h: (0, o * H + h))

    one = pl.BlockSpec((1, HEAD_DIM), lambda b, h: (0, 0))
    sh = (S, HEAD_DIM)
    st = (2, nc, HEAD_DIM, HEAD_DIM)
    group = max(g for g in range(1, 10) if nc % g == 0)
    return pl.pallas_call(
        functools.partial(_gdn_body, lay=lay, n_heads=H, group=group),
        out_shape=jax.ShapeDtypeStruct((T, W), BF16),
        grid=(lay.B, H),
        in_specs=[col(0), col(1), col(2), col(3),
                  pl.BlockSpec((S, LANES), lambda b, h: (b, 0)),
                  cw(0), cw(1), cw(2), one,
                  pl.BlockSpec((CHUNK, TRI_BLOCK * LANES), lambda b, h: (0, 0))],
        out_specs=pl.BlockSpec((S, HEAD_DIM), lambda b, h: (b, h)),
        scratch_shapes=[pltpu.VMEM(sh, F32), pltpu.VMEM(sh, F32), pltpu.VMEM(sh, F32),
                        pltpu.VMEM((2,) + sh, F32), pltpu.VMEM((2,) + sh, F32),
                        pltpu.VMEM((2,) + sh, BF16), pltpu.VMEM((2,) + sh, F32),
                        pltpu.VMEM(st, BF16), pltpu.VMEM(st, F32),
                        pltpu.VMEM((2, nc, 8, HEAD_DIM), F32), pltpu.VMEM((2,) + sh, F32)],
        compiler_params=_params("parallel", "arbitrary"),
        name="gated_deltanet",
    )(px, px, px, px, gates, conv_w, conv_w, conv_w, norm_g.reshape(1, HEAD_DIM), _column_spread_matrix())


def _hgrn_body(q_ref, f0_ref, f1_ref, i_ref, z_ref, lb0_ref, lb1_ref, g_ref, o_ref,
               gc, kk, o_s, *, lay, layer):
    S = lay.S
    nc = S // CHUNK
    for d, (f_ref, lb_ref) in enumerate(((f0_ref, lb0_ref), (f1_ref, lb1_ref))):
        lg = lb_ref[...]
        depth = lg.shape[0]
        mx = lg[0:1]
        for r in range(1, depth):
            mx = jnp.maximum(mx, lg[r:r + 1])
        e = [jnp.exp(lg[r:r + 1] - mx) for r in range(depth)]
        tot = e[0]
        for r in range(1, depth):
            tot = tot + e[r]
        lb = jnp.zeros_like(mx)
        for r in range(1, layer + 1):
            lb = lb + e[r] / tot
        f = f_ref[...]
        e_neg = jnp.exp(-jnp.abs(f))
        big = 1.0 / (1.0 + e_neg)
        small = e_neg * big
        pos = f >= 0.0
        logf = jnp.log(jnp.maximum(lb, LB_FLOOR) + (1.0 - lb) * jnp.where(pos, big, small))
        gc[d] = _chunk_cumsum(logf, reverse=(d == 1))
        kk[d] = (1.0 - lb) * jnp.where(pos, small, big)

    SB = 16
    nsb = CHUNK // SB
    jrow = lax.broadcasted_iota(jnp.int32, (CHUNK, HEAD_DIM), 0)
    jsub = lax.broadcasted_iota(jnp.int32, (SB, HEAD_DIM), 0)
    alane = lax.broadcasted_iota(jnp.int32, (SB, CHUNK), 1)

    def intra_t(d, qc, kc, g):
        if d == 0:
            refs = [g[0:1, :]] + [g[SB * I - 1:SB * I, :] for I in range(1, nsb)]
            far = range(1, nsb)
        else:
            refs = [g[SB * (I + 1):SB * (I + 1) + 1, :] for I in range(nsb - 1)] + [g[CHUNK - 1:CHUNK, :]]
            far = range(nsb - 1)
        rvec = jnp.concatenate([jnp.broadcast_to(r, (SB, HEAD_DIM)) for r in refs], axis=0)
        qt = qc * jnp.exp(g - rvec)
        off = None
        for I in far:
            seen = (jrow < SB * I) if d == 0 else (jrow >= SB * (I + 1))
            kt = (kc * jnp.exp(jnp.where(seen, refs[I] - g, MASK_NEG))).astype(BF16)
            qi = jnp.where((jrow // SB) == I, qt, 0.0).astype(BF16)
            part = lax.dot_general(kt, qi, _NT, preferred_element_type=F32)
            off = part if off is None else off + part
        blocks = []
        for I in range(nsb):
            rs = slice(SB * I, SB * (I + 1))
            g_i, k_i = g[rs, :], kc[rs, :]
            acc = jnp.zeros((SB, CHUNK), F32)
            for r in range(SB):
                i = SB * I + r
                keep = (jsub <= r) if d == 0 else (jsub >= r)
                rel = jnp.exp(jnp.where(keep, g[i:i + 1, :] - g_i, MASK_NEG))
                colv = jnp.sum(rel * k_i * qc[i:i + 1, :], axis=-1, keepdims=True)
                acc = jnp.where(alane == i, colv, acc)
            blocks.append(acc)
        return jnp.concatenate(blocks, axis=0) + off

    def step_d(d, n, st):
        c = _chunk_of_step(n, d, lay)
        sl = pl.ds(pl.multiple_of(c * CHUNK, CHUNK), CHUNK)
        qc, kc, vc, g = q_ref[sl, :], kk[d, sl, :], i_ref[sl, :], gc[d, sl, :]
        glast = g[CHUNK - 1:CHUNK, :] if d == 0 else g[0:1, :]
        a_t = intra_t(d, qc, kc, g)
        vb = vc.astype(BF16)
        o = (lax.dot_general((qc * jnp.exp(g)).astype(BF16), st.astype(BF16), _NT, preferred_element_type=F32)
             + lax.dot_general(a_t.astype(BF16), vb, _TN, preferred_element_type=F32))
        o_s[d, sl, :] = o
        kg = (kc * jnp.exp(glast - g)).astype(BF16)
        return st * jnp.exp(glast) + lax.dot_general(vb, kg, _TN, preferred_element_type=F32)

    def both(n, carry):
        return (step_d(0, n, carry[0]), step_d(1, n, carry[1]))

    zero = jnp.zeros((HEAD_DIM, HEAD_DIM), F32)
    lax.fori_loop(0, nc, both, (zero, zero), unroll=4 if nc % 4 == 0 else 1)
    o_ref[...] = _gated_head_norm(o_s[0] + o_s[1], z_ref[...], g_ref[...])


def hgrn2(px, lb_logits, norm_g, lay, W, col0, layer):
    T = px.shape[0]
    H = W // HEAD_DIM
    S = lay.S
    depth = lb_logits.shape[0]

    def col(o):
        return pl.BlockSpec((S, HEAD_DIM), lambda b, h: (b, col0 + o * H + h))

    def lbs(d):
        return pl.BlockSpec((depth, HEAD_DIM), lambda b, h: (0, d * H + h))

    sh = (S, HEAD_DIM)
    return pl.pallas_call(
        functools.partial(_hgrn_body, lay=lay, layer=layer),
        out_shape=jax.ShapeDtypeStruct((T, W), BF16),
        grid=(lay.B, H),
        in_specs=[col(0), col(1), col(2), col(3), col(4), lbs(0), lbs(1),
                  pl.BlockSpec((1, HEAD_DIM), lambda b, h: (0, 0))],
        out_specs=pl.BlockSpec((S, HEAD_DIM), lambda b, h: (b, h)),
        scratch_shapes=[pltpu.VMEM((2,) + sh, F32), pltpu.VMEM((2,) + sh, F32), pltpu.VMEM((2,) + sh, F32)],
        compiler_params=_params("parallel", "arbitrary"),
        name="hgrn2",
    )(px, px, px, px, px, lb_logits, lb_logits, norm_g.reshape(1, HEAD_DIM))


def _split_bf16(x):
    hi = x.astype(BF16)
    return hi, (x - hi.astype(F32)).astype(BF16)


def _dot3(a_hi, a_lo, b_hi, b_lo):
    return (jnp.dot(a_hi, b_hi, preferred_element_type=F32)
            + jnp.dot(a_hi, b_lo, preferred_element_type=F32)
            + jnp.dot(a_lo, b_hi, preferred_element_type=F32))


def _hy_filter_body(z_ref, w1_ref, b1_ref, w2_ref, b2_ref, w3_ref, fr_ref, dl_ref, o_ref, *, L):
    i = pl.program_id(0)
    z = z_ref[...]
    tr = z.shape[0]
    W = o_ref.shape[1]
    h = jnp.sin(fr_ref[0:1, :] * (jnp.dot(z, w1_ref[...], precision=HIGHEST, preferred_element_type=F32) + b1_ref[...]))
    h = jnp.sin(fr_ref[1:2, :] * (jnp.dot(h, w2_ref[...], precision=HIGHEST, preferred_element_type=F32) + b2_ref[...]))
    h = jnp.dot(h, w3_ref[...], precision=HIGHEST, preferred_element_type=F32)
    h = h * jnp.exp(-z[:, 0:1] * dl_ref[...])
    row = lax.broadcasted_iota(jnp.int32, (tr, W), 0) + i * tr
    o_ref[...] = jnp.where(row < L, h[:, :W], jnp.where(row == L, 0.0, h[:, W:]))


def hyena_filter(z2, w1p, b1, w2, b2, w3, freq, deltas2, L):
    n = z2.shape[0]
    W = w3.shape[1] // 2
    O = w2.shape[0]
    tr = _tile(n, 512)
    full = lambda a: pl.BlockSpec(a.shape, lambda i: (0,) * a.ndim)
    args = (w1p, b1.reshape(1, O), w2, b2.reshape(1, O), w3, freq, deltas2)
    return pl.pallas_call(
        functools.partial(_hy_filter_body, L=L),
        out_shape=jax.ShapeDtypeStruct((n, W), F32),
        grid=(n // tr,),
        in_specs=[pl.BlockSpec((tr, LANES), lambda i: (i, 0))] + [full(a) for a in args],
        out_specs=pl.BlockSpec((tr, W), lambda i: (i, 0)),
        compiler_params=_params("parallel"),
        name="hyena_filter",
    )(z2, *args)


def _hy_prep_body(v_ref, x0_ref, x1_ref, cv_ref, c0_ref, c1_ref, bv_ref, b0_ref, b1_ref,
                  vgx_ref, vge_ref, x0u_ref, *, lay):
    shape = v_ref.shape
    prev_ok, next_ok = _conv_masks(shape, lay)
    v = _centred_conv(v_ref[...], cv_ref[...], prev_ok, next_ok) + bv_ref[...]
    x0 = _centred_conv(x0_ref[...], c0_ref[...], prev_ok, next_ok) + b0_ref[...]
    x1 = _centred_conv(x1_ref[...], c1_ref[...], prev_ok, next_ok) + b1_ref[...]
    vg = v * x1
    vge_ref[...] = vg[:lay.Lc, :]
    vgx_ref[...] = vg[lay.Lc:, :]
    x0u_ref[...] = x0


def hyena_prep(px, conv_w, conv_b, lay, W, col0):
    T = px.shape[0]
    H = W // HEAD_DIM
    S, L, Lc, B = lay.S, lay.L, lay.Lc, lay.B

    def col(o):
        return pl.BlockSpec((S, HEAD_DIM), lambda b, j: (b, col0 + o * H + j))

    def cw(o, rows):
        return pl.BlockSpec((rows, HEAD_DIM), lambda b, j: (0, o * H + j))

    return pl.pallas_call(
        functools.partial(_hy_prep_body, lay=lay),
        out_shape=(jax.ShapeDtypeStruct((B * L, W), F32), jax.ShapeDtypeStruct((B * Lc, W), F32),
                   jax.ShapeDtypeStruct((T, W), F32)),
        grid=(B, H),
        in_specs=[col(0), col(1), col(2), cw(0, 3), cw(1, 3), cw(2, 3), cw(0, 1), cw(1, 1), cw(2, 1)],
        out_specs=(pl.BlockSpec((L, HEAD_DIM), lambda b, j: (b, j)),
                   pl.BlockSpec((Lc, HEAD_DIM), lambda b, j: (b, j)),
                   pl.BlockSpec((S, HEAD_DIM), lambda b, j: (b, j))),
        compiler_params=_params("parallel", "parallel"),
        name="hyena_prep",
    )(px, px, px, conv_w, conv_w, conv_w, conv_b, conv_b, conv_b)


def _dft_tables(L):
    n = 2 * L
    FT = min(256, L)
    sub = 64
    idx = jnp.arange(n, dtype=jnp.int32)
    r = idx % (2 * FT)
    is_im = r >= FT
    f = (idx // (2 * FT)) * FT + jnp.where(is_im, r - FT, r)
    nyq = is_im & (f == 0)
    f = jnp.where(nyq, L, f)
    step = jnp.arange(sub, dtype=jnp.int32)
    ang_hi = (2.0 * math.pi / n) * ((f[:, None] * (step * sub)[None, :]) % n).astype(F32)
    ang_lo = (2.0 * math.pi / n) * ((f[:, None] * step[None, :]) % n).astype(F32)
    ch, sh, cl, sl = jnp.cos(ang_hi), jnp.sin(ang_hi), jnp.cos(ang_lo), jnp.sin(ang_lo)
    wgt = jnp.where((f == 0) | (f == L), 1.0 / n, 2.0 / n)
    minus_sin = is_im & ~nyq

    def table(hi_steps, transpose):
        if transpose:
            c_h, s_h, c_l, s_l = (t.T for t in (ch, sh, cl, sl))
            cos = c_h[:hi_steps, None, :] * c_l[None, :, :] - s_h[:hi_steps, None, :] * s_l[None, :, :]
            sin = s_h[:hi_steps, None, :] * c_l[None, :, :] + c_h[:hi_steps, None, :] * s_l[None, :, :]
            out = jnp.where(minus_sin[None, None, :], -sin, cos) * wgt[None, None, :]
            return out.reshape(hi_steps * sub, n)
        cos = ch[:, :hi_steps, None] * cl[:, None, :] - sh[:, :hi_steps, None] * sl[:, None, :]
        sin = sh[:, :hi_steps, None] * cl[:, None, :] + ch[:, :hi_steps, None] * sl[:, None, :]
        return jnp.where(minus_sin[:, None, None], -sin, cos).reshape(n, hi_steps * sub)

    return table(n // sub, False), table(L // sub, False), table(L // sub, True), FT


def _hy_fwd_body(a_ref, u_ref, kf_ref, y_ref, *, FT):
    t = pl.program_id(2)
    x = jnp.dot(a_ref[...], u_ref[...].astype(BF16), preferred_element_type=F32)
    kf = kf_ref[...]
    xr, xi, kr, ki = x[:FT], x[FT:], kf[:FT], kf[FT:]
    row = lax.broadcasted_iota(jnp.int32, xr.shape, 0)
    packed = (row + t) == 0
    yr = jnp.where(packed, xr * kr, xr * kr - xi * ki)
    yi = jnp.where(packed, xi * ki, xr * ki + xi * kr)
    y_ref[...] = jnp.concatenate([yr, yi], axis=0).astype(BF16)


def _hy_spec_body(ah_ref, al_ref, u_ref, o_ref):
    uh, ul = _split_bf16(u_ref[...])
    o_ref[...] = _dot3(ah_ref[...], al_ref[...], uh, ul)


def hyena_spectrum(fh, fl, kern, FT):
    n, W = kern.shape
    tn = _tile(W, 256)
    return pl.pallas_call(
        _hy_spec_body,
        out_shape=jax.ShapeDtypeStruct((n, W), F32),
        grid=(W // tn, n // (2 * FT)),
        in_specs=[pl.BlockSpec((2 * FT, n), lambda j, t: (t, 0)),
                  pl.BlockSpec((2 * FT, n), lambda j, t: (t, 0)),
                  pl.BlockSpec((n, tn), lambda j, t: (0, j))],
        out_specs=pl.BlockSpec((2 * FT, tn), lambda j, t: (t, j)),
        compiler_params=_params("parallel", "arbitrary"),
        name="hyena_spectrum",
    )(fh, fl, kern)


def hyena_fwd(f_b, vg, kf, B, Lq, FT):
    W = vg.shape[1]
    n = 2 * Lq
    nt = n // (2 * FT)
    tn = _tile(W, 256)
    return pl.pallas_call(
        functools.partial(_hy_fwd_body, FT=FT),
        out_shape=jax.ShapeDtypeStruct((B * n, W), BF16),
        grid=(B, W // tn, nt),
        in_specs=[pl.BlockSpec((2 * FT, Lq), lambda b, j, t: (t, 0)),
                  pl.BlockSpec((Lq, tn), lambda b, j, t: (b, j)),
                  pl.BlockSpec((2 * FT, tn), lambda b, j, t: (t, j))],
        out_specs=pl.BlockSpec((2 * FT, tn), lambda b, j, t: (b * nt + t, j)),
        compiler_params=_params("parallel", "parallel", "arbitrary"),
        name="hyena_fwd",
    )(f_b, vg, kf)


def _hy_inv_body(g_ref, y_ref, vg_ref, x0_ref, bias_ref, o_ref):
    y = jnp.dot(g_ref[...], y_ref[...], preferred_element_type=F32)
    o_ref[...] = (x0_ref[...] * (y + vg_ref[...] * bias_ref[...])).astype(BF16)


def hyena_inv(g_b, y_b, vg, x0u, bias, lay, Lq, row0):
    W = vg.shape[1]
    n = 2 * Lq
    B = lay.B
    tt = lay.gr
    ntt = Lq // tt
    tn = _tile(W, 256)
    g0 = row0 // tt
    return pl.pallas_call(
        _hy_inv_body,
        out_shape=jax.ShapeDtypeStruct((B * Lq, W), BF16),
        grid=(B, W // tn, ntt),
        in_specs=[pl.BlockSpec((tt, n), lambda b, j, i: (i, 0)),
                  pl.BlockSpec((n, tn), lambda b, j, i: (b, j)),
                  pl.BlockSpec((tt, tn), lambda b, j, i: (b * ntt + i, j)),
                  pl.BlockSpec((tt, tn), lambda b, j, i: (b * lay.NG + g0 + i, j)),
                  pl.BlockSpec((1, tn), lambda b, j, i: (0, j))],
        out_specs=pl.BlockSpec((tt, tn), lambda b, j, i: (b * ntt + i, j)),
        compiler_params=_params("parallel", "parallel", "arbitrary"),
        name="hyena_inv",
    )(g_b, y_b, vg, x0u, bias.reshape(1, W))


def _hyena_features(L):
    bands = (HY_EMB - 1) // 2
    t = jnp.linspace(0.0, 1.0, L, dtype=F32)[:, None]
    wpos = 2.0 * math.pi * jnp.arange(L, dtype=F32)[:, None] / L
    fb = jnp.linspace(1e-4, bands - 1, bands, dtype=F32)[None]
    z = jnp.concatenate([t, jnp.cos(fb * wpos), -jnp.sin(fb * wpos)], axis=-1)
    z2 = jnp.concatenate([z, z[::-1]], axis=0)
    return jnp.pad(z2, ((0, 0), (0, LANES - HY_EMB)))


def hyena_stream(vg, x0u, bias, wts, lay, Lq, row0):
    w1p, b1, w2, b2, w3, freq, deltas2 = wts
    fwd_full, fwd_half, inv, FT = _dft_tables(Lq)
    kh, kl = _split_bf16(fwd_full)
    kern = hyena_filter(_hyena_features(Lq), w1p, b1, w2, b2, w3, freq, deltas2, Lq)
    kf = hyena_spectrum(kh, kl, kern, FT)
    y_b = hyena_fwd(fwd_half.astype(BF16), vg, kf, lay.B, Lq, FT)
    return hyena_inv(inv.astype(BF16), y_b, vg, x0u, bias, lay, Lq, row0)


def _rope_tables(lay):
    n_freq = DIFF_DIM // 4
    inv = ROPE_THETA ** (-jnp.arange(n_freq, dtype=F32) / n_freq)
    rows = lay.L // GRID_W
    r = jnp.repeat(jnp.arange(rows, dtype=F32), GRID_W)
    col = jnp.tile(jnp.arange(GRID_W, dtype=F32), rows)
    ang = jnp.concatenate([r[:, None] * inv, col[:, None] * inv], axis=-1)
    cos, sin = jnp.cos(ang), jnp.sin(ang)
    cos = jnp.concatenate([jnp.ones((lay.Lc, DIFF_DIM // 2), F32), cos], axis=0)
    sin = jnp.concatenate([jnp.zeros((lay.Lc, DIFF_DIM // 2), F32), sin], axis=0)
    cos_t = jnp.tile(cos, (1, 4))
    sin_t = jnp.tile(jnp.concatenate([-sin, sin], axis=-1), (1, 2))
    return cos_t, sin_t


def _main_cols(W, H):
    small0 = 7 * W
    small1 = small0 + 4 * H
    return small0, small1


def mixers(px1, px2, ps, lay, W, layer, need_ctx, p):
    H = W // HEAD_DIM
    lam_init = 0.8 - 0.6 * math.exp(-0.3 * layer)
    q_rot, k_rot = attn_prep(px1, p["cos_t"], p["sin_t"], p["qk_g2"], lay, W)
    att = attention(q_rot, k_rot, px1, p["attn_lambda"], p["attn_subln_g"], lay, W, lam_init)
    gates = gdn_gates(ps, p["alog"], p["dtb"], lay, H)
    gdn = gated_deltanet(px1, gates, p["gdn_conv_w"], p["gdn_norm_g"], lay, W, 3 * H)
    hg = hgrn2(px2, p["hg_lb_logits"], p["hg_norm_g"], lay, W, 0, layer)
    vgx, vge, x0u = hyena_prep(px2, p["hy_conv_w"], p["hy_conv_b"], lay, W, 5 * H)
    hy_x = hyena_stream(vgx, x0u, p["hy_bias"], p["hy_wts"], lay, lay.L, lay.Lc)
    if need_ctx:
        hy_e = hyena_stream(vge, x0u, p["hy_bias"], p["hy_wts"], lay, lay.Lc, 0)
    else:
        hy_e = jnp.zeros((lay.B, lay.Lc, W), BF16)
    hy = jnp.concatenate([hy_e.reshape(lay.B, lay.Lc, W), hy_x.reshape(lay.B, lay.L, W)], axis=1)
    return att, gdn, hg, hy.reshape(lay.T, W)


def _layer_params(l, lay, W, attn_qk_g, attn_lambda, attn_subln_g, gdn_conv_w, gdn_a_log, gdn_dt_bias, gdn_norm_g,
                  hg_lb_logits, hg_norm_g, hy_conv_w, hy_conv_b, hy_w1, hy_b1, hy_w2, hy_b2, hy_w3, hy_freq,
                  hy_bias, cos_t, sin_t):
    H = W // HEAD_DIM
    pad16 = lambda a: jnp.pad(a.astype(F32).reshape(1, 2 * H), ((0, 0), (0, LANES - 2 * H)))
    deltas = jnp.abs(jnp.linspace(math.log(HY_TARGET) / HY_SLOW_DECAY, math.log(HY_TARGET) / HY_FAST_DECAY, W, dtype=F32))
    return dict(
        cos_t=cos_t, sin_t=sin_t,
        qk_g2=jnp.tile(attn_qk_g[l].astype(F32), (1, 2)),
        attn_lambda=attn_lambda[l].astype(F32), attn_subln_g=attn_subln_g[l].astype(F32),
        gdn_conv_w=gdn_conv_w[l], alog=pad16(gdn_a_log[l]), dtb=pad16(gdn_dt_bias[l]), gdn_norm_g=gdn_norm_g[l],
        hg_lb_logits=hg_lb_logits.astype(F32).reshape(hg_lb_logits.shape[0], 2 * W), hg_norm_g=hg_norm_g[l],
        hy_conv_w=hy_conv_w[l], hy_conv_b=hy_conv_b[l].reshape(1, 3 * W), hy_bias=hy_bias[l].astype(F32),
        hy_wts=(jnp.pad(hy_w1[l].astype(F32), ((0, LANES - HY_EMB), (0, 0))), hy_b1[l].astype(F32),
                hy_w2[l].astype(F32), hy_b2[l].astype(F32), hy_w3[l].astype(F32), hy_freq[l].astype(F32),
                jnp.tile(deltas, 2).reshape(1, 2 * W)),
    )


def kernel(x, c, ctx, c_ctx, norm_g, w_mod, b_mod, ffn_w_in, ffn_w_out, w_in, attn_qk_g, attn_lambda, attn_subln_g, gdn_conv_w, gdn_a_log, gdn_dt_bias, gdn_norm_g, hg_lb_logits, hg_norm_g, hy_conv_w, hy_conv_b, hy_w1, hy_b1, hy_w2, hy_b2, hy_w3, hy_freq, hy_bias, w_gate, w_up, w_out):
    B, L, D = x.shape
    Lc = ctx.shape[1]
    depth = w_mod.shape[0]
    W = D // N_BRANCH
    H = W // HEAD_DIM
    lay = Layout(B, L, Lc)
    h = jnp.concatenate([ctx, x], axis=1).reshape(lay.T, D)
    G = -(-(B + 1) // 8) * 8
    c_all = jnp.concatenate([c, c_ctx[None], jnp.zeros((G - B - 1, D), F32)], axis=0)
    cos_t, sin_t = _rope_tables(lay)
    small0, small1 = _main_cols(W, H)
    w_gate_b, w_up_b = w_gate.astype(BF16), w_up.astype(BF16)
    for l in range(depth):
        last = l == depth - 1
        p = _layer_params(l, lay, W, attn_qk_g, attn_lambda, attn_subln_g, gdn_conv_w, gdn_a_log, gdn_dt_bias,
                          gdn_norm_g, hg_lb_logits, hg_norm_g, hy_conv_w, hy_conv_b, hy_w1, hy_b1, hy_w2, hy_b2,
                          hy_w3, hy_freq, hy_bias, cos_t, sin_t)
        mod = mod_table(c_all, w_mod, b_mod, l).reshape(G, MOD_CHUNKS, D)
        act = mm_swiglu(norm_mod(h, norm_g[l, 0], mod, lay, 0), ffn_w_in, (l, 0))
        h = mm_resid(act, ffn_w_out, (l, 0), h, mod, lay, 2, 0.5)
        xn = norm_mod(h, norm_g[l, 1], mod, lay, 1)
        cols = lambda lo, hi: lax.slice(w_in, (l, 0, lo), (l + 1, D, hi))
        px1 = mm_plain(xn, cols(0, small0), (0,))
        px2 = mm_plain(xn, cols(small1, w_in.shape[-1]), (0,))
        ps = mm_plain(xn, jnp.pad(cols(small0, small1), ((0, 0), (0, 0), (0, LANES - (small1 - small0)))), (0,),
                      tn_pref=LANES)
        branches = mixers(px1, px2, ps, lay, W, l, not last, p)
        if last:
            latent = lambda a: a.reshape(B, lay.S, a.shape[-1])[:, Lc:, :].reshape(B * L, a.shape[-1])
            xn, h, branches = latent(xn), latent(h), tuple(latent(a) for a in branches)
            lay = Layout(B, L, 0, gr=lay.gr)
        acc = merge_branches(xn, branches, w_gate_b, w_up_b, l)
        h = mm_resid(acc, w_out, (l,), h, mod, lay, 5, 1.0)
        act = mm_swiglu(norm_mod(h, norm_g[l, 2], mod, lay, 2), ffn_w_in, (l, 1))
        h = mm_resid(act, ffn_w_out, (l, 1), h, mod, lay, 8, 0.5)
    return h.reshape(B, L, D)
```

```python
import functools
import math

import jax
import jax.numpy as jnp
from jax import lax
from jax.experimental import pallas as pl
from jax.experimental.pallas import tpu as pltpu

F32 = jnp.float32
BF16 = jnp.bfloat16

GRID_W = 64
N_BRANCH = 4
HEAD_DIM = 128
DIFF_DIM = HEAD_DIM // 2
MOD_CHUNKS = 9
ROPE_THETA = 10000.0
CHUNK = 64
HY_EMB = 33
HY_FAST_DECAY = 0.3
HY_SLOW_DECAY = 1.5
HY_TARGET = 1e-2
EPS = 1e-6
MASK_NEG = -1e30
LB_FLOOR = 1e-20

LANES = 128
V7X_VMEM_LIMIT = 56 * 1024 * 1024
HIGHEST = lax.Precision.HIGHEST

_NT = (((1,), (1,)), ((), ()))
_TN = (((0,), (0,)), ((), ()))


def _params(*sem):
    return pltpu.CompilerParams(dimension_semantics=sem, vmem_limit_bytes=V7X_VMEM_LIMIT)


def _tile(n, pref):
    t = min(n, pref)
    while n % t:
        t //= 2
    return t


def _wspec(at, K, tn, col):
    return pl.BlockSpec((None,) * len(at) + (K, tn), lambda *ids: tuple(at) + (0, col(*ids)))


def _sigmoid(x):
    return 1.0 / (1.0 + jnp.exp(-x))


def _silu(x):
    return x * _sigmoid(x)


def _softplus(x):
    return jnp.maximum(x, 0.0) + jnp.log1p(jnp.exp(-jnp.abs(x)))


class Layout:
    def __init__(self, B, L, Lc, gr=None):
        self.B, self.L, self.Lc = B, L, Lc
        self.S = L + Lc
        self.T = B * self.S
        self.gr = gr or math.gcd(L, Lc)
        self.NG = self.S // self.gr
        self.NGc = Lc // self.gr


def _mod_row(mod_ref, lay, r, k):
    b = r // lay.NG
    g = jnp.where(r - b * lay.NG < lay.NGc, lay.B, b)
    return mod_ref[pl.ds(g, 1), k:k + 1, :][0]


def _mod_body(c_ref, w_ref, b_ref, o_ref):
    a = _silu(c_ref[...]).astype(BF16)
    o_ref[...] = jnp.dot(a, w_ref[...].astype(BF16), preferred_element_type=F32) + b_ref[...]


def mod_table(c16, w_mod, b_mod, l):
    R, D = c16.shape
    N = w_mod.shape[-1]
    tn = _tile(N, 512)
    return pl.pallas_call(
        _mod_body,
        out_shape=jax.ShapeDtypeStruct((R, N), F32),
        grid=(N // tn,),
        in_specs=[pl.BlockSpec((R, D), lambda j: (0, 0)),
                  _wspec((l,), D, tn, lambda j: j),
                  _wspec((l,), 1, tn, lambda j: j)],
        out_specs=pl.BlockSpec((R, tn), lambda j: (0, j)),
        compiler_params=_params("parallel"),
        name="mod_table",
    )(c16, w_mod, b_mod.reshape(b_mod.shape[0], 1, N))


def _norm_mod_body(x_ref, g_ref, mod_ref, o_ref, *, lay, k):
    i = pl.program_id(0)
    gr = lay.gr
    n_sub = x_ref.shape[0] // gr
    for s in range(n_sub):
        x = x_ref[s * gr:(s + 1) * gr, :]
        y = x * lax.rsqrt(jnp.mean(x * x, axis=-1, keepdims=True) + EPS) * g_ref[...]
        r = i * n_sub + s
        shift = _mod_row(mod_ref, lay, r, 3 * k)
        scale = _mod_row(mod_ref, lay, r, 3 * k + 1)
        o_ref[s * gr:(s + 1) * gr, :] = (y * (1.0 + scale) + shift).astype(BF16)


def norm_mod(h, g, mod, lay, k):
    T, D = h.shape
    tm = lay.gr
    G = mod.shape[0]
    return pl.pallas_call(
        functools.partial(_norm_mod_body, lay=lay, k=k),
        out_shape=jax.ShapeDtypeStruct((T, D), BF16),
        grid=(T // tm,),
        in_specs=[pl.BlockSpec((tm, D), lambda i: (i, 0)),
                  pl.BlockSpec((1, D), lambda i: (0, 0)),
                  pl.BlockSpec((G, MOD_CHUNKS, D), lambda i: (0, 0, 0))],
        out_specs=pl.BlockSpec((tm, D), lambda i: (i, 0)),
        compiler_params=_params("parallel"),
        name="norm_mod",
    )(h, g.reshape(1, D), mod)


def _mm_swiglu_body(a_ref, wg_ref, wu_ref, o_ref, wgb, wub):
    @pl.when(pl.program_id(1) == 0)
    def _():
        wgb[...] = wg_ref[...].astype(BF16)
        wub[...] = wu_ref[...].astype(BF16)

    a = a_ref[...]
    g = jnp.dot(a, wgb[...], preferred_element_type=F32)
    u = jnp.dot(a, wub[...], preferred_element_type=F32)
    o_ref[...] = (_silu(g) * u).astype(BF16)


def mm_swiglu(a, w, at):
    T, K = a.shape
    F = w.shape[-1] // 2
    tm, tn = _tile(T, 1024), _tile(F, 256)
    nj = F // tn
    return pl.pallas_call(
        _mm_swiglu_body,
        out_shape=jax.ShapeDtypeStruct((T, F), BF16),
        grid=(nj, T // tm),
        in_specs=[pl.BlockSpec((tm, K), lambda j, i: (i, 0)),
                  _wspec(at, K, tn, lambda j, i: j),
                  _wspec(at, K, tn, lambda j, i: j + nj)],
        out_specs=pl.BlockSpec((tm, tn), lambda j, i: (i, j)),
        scratch_shapes=[pltpu.VMEM((K, tn), BF16), pltpu.VMEM((K, tn), BF16)],
        compiler_params=_params("parallel", "arbitrary"),
        name="mm_swiglu",
    )(a, w, w)


def _mm_resid_body(a_ref, w_ref, h_ref, mod_ref, o_ref, wb, *, lay, k, scale):
    i = pl.program_id(1)

    @pl.when(i == 0)
    def _():
        wb[...] = w_ref[...].astype(BF16)

    y = jnp.dot(a_ref[...], wb[...], preferred_element_type=F32)
    gr = lay.gr
    n_sub = a_ref.shape[0] // gr
    for s in range(n_sub):
        gate = _mod_row(mod_ref, lay, i * n_sub + s, k)
        sl = slice(s * gr, (s + 1) * gr)
        o_ref[sl, :] = h_ref[sl, :] + (scale * gate) * y[sl, :]


def mm_resid(a, w, at, h, mod, lay, k, scale):
    T, K = a.shape
    N = w.shape[-1]
    G = mod.shape[0]
    def vmem_bytes(tm, tn):
        return 2 * tm * K * 2 + 2 * K * tn * 4 + K * tn * 2 + 4 * tm * tn * 4

    tm, tn = max(_tile(T, 1024), lay.gr), _tile(N, 512)
    if vmem_bytes(tm, tn) > (V7X_VMEM_LIMIT * 7) // 8:
        tn = _tile(N, 256)
    return pl.pallas_call(
        functools.partial(_mm_resid_body, lay=lay, k=k, scale=scale),
        out_shape=jax.ShapeDtypeStruct((T, N), F32),
        grid=(N // tn, T // tm),
        in_specs=[pl.BlockSpec((tm, K), lambda j, i: (i, 0)),
                  _wspec(at, K, tn, lambda j, i: j),
                  pl.BlockSpec((tm, tn), lambda j, i: (i, j)),
                  pl.BlockSpec((G, MOD_CHUNKS, tn), lambda j, i: (0, 0, j))],
        out_specs=pl.BlockSpec((tm, tn), lambda j, i: (i, j)),
        scratch_shapes=[pltpu.VMEM((K, tn), BF16)],
        compiler_params=_params("parallel", "arbitrary"),
        name="mm_resid",
    )(a, w, h, mod)


def _mm_plain_body(a_ref, wt_ref, o_ref, wb):
    @pl.when(pl.program_id(1) == 0)
    def _():
        wb[...] = wt_ref[...].T.astype(BF16)

    o_ref[...] = jnp.dot(a_ref[...], wb[...], preferred_element_type=F32)


def mm_plain(a, w_t, tn_pref=512):
    T, K = a.shape
    N = w_t.shape[0]
    tm, tn = _tile(T, 1024), _tile(N, tn_pref)
    return pl.pallas_call(
        _mm_plain_body,
        out_shape=jax.ShapeDtypeStruct((T, N), F32),
        grid=(N // tn, T // tm),
        in_specs=[pl.BlockSpec((tm, K), lambda j, i: (i, 0)),
                  pl.BlockSpec((tn, K), lambda j, i: (j, 0))],
        out_specs=pl.BlockSpec((tm, tn), lambda j, i: (i, j)),
        scratch_shapes=[pltpu.VMEM((K, tn), BF16)],
        compiler_params=_params("parallel", "arbitrary"),
        name="mm_plain",
    )(a, w_t)


def _merge_body(xn_ref, o0_ref, o1_ref, o2_ref, o3_ref, wg_ref, wu_ref, out_ref):
    xn = xn_ref[...]
    acc = None
    for br, o_ref in enumerate((o0_ref, o1_ref, o2_ref, o3_ref)):
        g = jnp.dot(xn, wg_ref[br], preferred_element_type=F32)
        u = jnp.dot(o_ref[...], wu_ref[br], preferred_element_type=F32)
        term = _sigmoid(g) * u
        acc = term if acc is None else acc + term
    out_ref[...] = acc.astype(BF16)


def merge_branches(xn, branches, w_gate_b, w_up_b, l):
    T, D = xn.shape
    W = branches[0].shape[1]
    tm, tn = _tile(T, 512), _tile(D, 256)
    bspec = pl.BlockSpec((tm, W), lambda i, j: (i, 0))
    return pl.pallas_call(
        _merge_body,
        out_shape=jax.ShapeDtypeStruct((T, D), BF16),
        grid=(T // tm, D // tn),
        in_specs=[pl.BlockSpec((tm, D), lambda i, j: (i, 0)), bspec, bspec, bspec, bspec,
                  pl.BlockSpec((None, N_BRANCH, D, tn), lambda i, j: (l, 0, 0, j)),
                  pl.BlockSpec((None, N_BRANCH, W, tn), lambda i, j: (l, 0, 0, j))],
        out_specs=pl.BlockSpec((tm, tn), lambda i, j: (i, j)),
        compiler_params=_params("parallel", "arbitrary"),
        name="merge_branches",
    )(xn, *branches, w_gate_b, w_up_b)


def _attn_prep_body(q_ref, k_ref, cos_ref, sin_ref, g_ref, qo_ref, ko_ref):
    cos = cos_ref[...]
    sin = sin_ref[...]
    lane = lax.broadcasted_iota(jnp.int32, cos.shape, 1)
    lo = lane < DIFF_DIM
    first = (lane % DIFF_DIM) < DIFF_DIM // 2
    n_heads = q_ref.shape[1] // HEAD_DIM
    for src, dst, gi, sc in ((q_ref, qo_ref, 0, DIFF_DIM ** -0.5), (k_ref, ko_ref, 1, 1.0)):
        g = g_ref[gi:gi + 1, :]
        for h in range(n_heads):
            cs = slice(h * HEAD_DIM, (h + 1) * HEAD_DIM)
            t = src[:, cs]
            ss = t * t
            s_lo = jnp.sum(jnp.where(lo, ss, 0.0), axis=-1, keepdims=True)
            s_hi = jnp.sum(jnp.where(lo, 0.0, ss), axis=-1, keepdims=True)
            inv = jnp.where(lo, lax.rsqrt(s_lo / DIFF_DIM + EPS), lax.rsqrt(s_hi / DIFF_DIM + EPS))
            y = t * inv * g
            partner = jnp.where(first, pltpu.roll(y, HEAD_DIM - DIFF_DIM // 2, 1), pltpu.roll(y, DIFF_DIM // 2, 1))
            dst[:, cs] = ((y * cos + partner * sin) * sc).astype(BF16)


def attn_prep(px, cos_t, sin_t, qk_g2, lay, W):
    T = px.shape[0]
    tr = lay.gr
    return pl.pallas_call(
        _attn_prep_body,
        out_shape=(jax.ShapeDtypeStruct((T, W), BF16), jax.ShapeDtypeStruct((T, W), BF16)),
        grid=(T // tr,),
        in_specs=[pl.BlockSpec((tr, W), lambda i: (i, 0)),
                  pl.BlockSpec((tr, W), lambda i: (i, 1)),
                  pl.BlockSpec((tr, LANES), lambda i: (i % lay.NG, 0)),
                  pl.BlockSpec((tr, LANES), lambda i: (i % lay.NG, 0)),
                  pl.BlockSpec((2, LANES), lambda i: (0, 0))],
        out_specs=(pl.BlockSpec((tr, W), lambda i: (i, 0)), pl.BlockSpec((tr, W), lambda i: (i, 0))),
        compiler_params=_params("parallel"),
        name="attn_prep",
    )(px, px, cos_t, sin_t, qk_g2)


def _attn_body(q_ref, k_ref, v_ref, lam_ref, g_ref, o_ref, *, lay, lam_init):
    qi = pl.program_id(2)
    lp = lam_ref[...]
    lam = (jnp.exp(jnp.sum(lp[0:1] * lp[1:2], axis=-1, keepdims=True))
           - jnp.exp(jnp.sum(lp[2:3] * lp[3:4], axis=-1, keepdims=True)) + lam_init)

    def run(nk):
        q = q_ref[...]
        k = k_ref[0:nk, :]
        v = v_ref[0:nk, :].astype(BF16)
        lane = lax.broadcasted_iota(jnp.int32, q.shape, 1)
        zero = jnp.zeros_like(q)
        outs = []
        for qq in (jnp.where(lane < DIFF_DIM, q, zero), jnp.where(lane < DIFF_DIM, zero, q)):
            s = lax.dot_general(qq, k, _NT, preferred_element_type=F32)
            p = jnp.exp(s - jnp.max(s, axis=-1, keepdims=True))
            l = jnp.sum(p, axis=-1, keepdims=True)
            outs.append(jnp.dot(p.astype(BF16), v, preferred_element_type=F32) / l)
        o = outs[0] - lam * outs[1]
        y = o * lax.rsqrt(jnp.mean(o * o, axis=-1, keepdims=True) + EPS) * g_ref[...]
        o_ref[...] = (y * (1.0 - lam_init)).astype(BF16)

    @pl.when(qi < lay.NGc)
    def _():
        run(lay.Lc)

    @pl.when(qi >= lay.NGc)
    def _():
        run(lay.S)


def attention(q_rot, k_rot, px, lam_p, subln_g, lay, W, lam_init):
    T = px.shape[0]
    H = W // HEAD_DIM
    tq = lay.gr
    S = lay.S
    vb = W // HEAD_DIM * 2
    return pl.pallas_call(
        functools.partial(_attn_body, lay=lay, lam_init=lam_init),
        out_shape=jax.ShapeDtypeStruct((T, W), BF16),
        grid=(lay.B, H, lay.NG),
        in_specs=[pl.BlockSpec((tq, HEAD_DIM), lambda b, h, i: (b * lay.NG + i, h)),
                  pl.BlockSpec((S, HEAD_DIM), lambda b, h, i: (b, h)),
                  pl.BlockSpec((S, HEAD_DIM), lambda b, h, i: (b, vb + h)),
                  pl.BlockSpec((4, DIFF_DIM), lambda b, h, i: (0, 0)),
                  pl.BlockSpec((1, HEAD_DIM), lambda b, h, i: (0, 0))],
        out_specs=pl.BlockSpec((tq, HEAD_DIM), lambda b, h, i: (b * lay.NG + i, h)),
        compiler_params=_params("parallel", "parallel", "arbitrary"),
        name="diff_attention",
    )(q_rot, k_rot, px, lam_p, subln_g.reshape(1, HEAD_DIM))


def _chunk_scan(y, reverse):
    S = y.shape[0]
    pos = lax.broadcasted_iota(jnp.int32, y.shape, 0) % CHUNK
    sh = 1
    while sh < CHUNK:
        if reverse:
            y = y + jnp.where(pos < CHUNK - sh, pltpu.roll(y, S - sh, 0), 0.0)
        else:
            y = y + jnp.where(pos >= sh, pltpu.roll(y, sh, 0), 0.0)
        sh *= 2
    return y


def _chunk_cumsum(y, reverse):
    S, C = y.shape
    nc = S // CHUNK
    ii = lax.broadcasted_iota(jnp.int32, (nc, CHUNK, CHUNK), 1)
    jj = lax.broadcasted_iota(jnp.int32, (nc, CHUNK, CHUNK), 2)
    tri = jnp.where((ii <= jj) if reverse else (ii >= jj), 1.0, 0.0).astype(BF16)
    y3 = y.reshape(nc, CHUNK, C)
    p1 = y3.astype(BF16)
    r1 = y3 - p1.astype(F32)
    p2 = r1.astype(BF16)
    p3 = (r1 - p2.astype(F32)).astype(BF16)
    out = _bmm(tri, p1) + _bmm(tri, p2) + _bmm(tri, p3)
    return out.reshape(S, C)


def _chunk_of_step(n, d, lay):
    ncc = lay.Lc // CHUNK
    nc = lay.S // CHUNK
    if d == 0:
        return n
    return jnp.where(n < ncc, ncc - 1 - n, nc - 1 - (n - ncc))


def _gated_head_norm(o, z, g):
    y = o * lax.rsqrt(jnp.mean(o * o, axis=-1, keepdims=True) + EPS) * g
    return (y * _silu(z)).astype(BF16)


def _centred_conv(x, w, prev_ok, next_ok):
    S = x.shape[0]
    xp = jnp.where(prev_ok, pltpu.roll(x, 1, 0), 0.0)
    xn = jnp.where(next_ok, pltpu.roll(x, S - 1, 0), 0.0)
    return xp * w[0:1] + x * w[1:2] + xn * w[2:3]


def _conv_masks(shape, lay):
    row = lax.broadcasted_iota(jnp.int32, shape, 0)
    prev_ok = (row * (row - lay.Lc)) != 0
    next_ok = ((row - (lay.Lc - 1)) * (row - (lay.S - 1))) != 0
    return prev_ok, next_ok


TRI_BLOCK = 8


def _bmm(x, y):
    return jnp.einsum('gij,gjk->gik', x, y, preferred_element_type=F32)


def _bmm_nt(x, y):
    return jnp.einsum('gik,gjk->gij', x, y, preferred_element_type=F32)


def _bmm_tn(x, y):
    return jnp.einsum('gjk,gjl->gkl', x, y, preferred_element_type=F32)


def _column_spread_matrix():
    m = jnp.arange(CHUNK)[:, None]
    c = jnp.arange(TRI_BLOCK * LANES)[None, :]
    j, lane = c // LANES, c % LANES
    hit = (lane < CHUNK) & (m % TRI_BLOCK == j) & (m // TRI_BLOCK == lane // TRI_BLOCK)
    return hit.astype(BF16)


def _unit_tri_inverse(a, spread, ii, jj, d):
    G = a.shape[0]
    nb = TRI_BLOCK
    a_d = jnp.where(ii // nb == jj // nb, a, 0.0).reshape(G * CHUNK, CHUNK)
    cols = jnp.dot(a_d.astype(BF16), spread, preferred_element_type=F32)
    row = lax.broadcasted_iota(jnp.int32, (G, CHUNK, LANES), 1)
    lane = lax.broadcasted_iota(jnp.int32, (G, CHUNK, LANES), 2)
    x = jnp.where(row == lane, 1.0, 0.0)
    for j in (range(nb - 1) if d == 0 else range(nb - 1, 0, -1)):
        col_j = cols[:, j * LANES:(j + 1) * LANES].reshape(G, CHUNK, LANES)
        x4 = x.reshape(G, CHUNK // nb, nb, LANES)
        row_j = jnp.broadcast_to(x4[:, :, j:j + 1, :], x4.shape).reshape(G, CHUNK, LANES)
        x = x - col_j * row_j
    t = x[:, :, :CHUNK]
    width = nb
    while width < CHUNK:
        e = jnp.where(((ii // width) ^ (jj // width)) == 1, a, 0.0).astype(BF16)
        tb = t.astype(BF16)
        t = t - _bmm(tb, _bmm(e, tb).astype(BF16))
        width *= 2
    return t


def _gdn_gates_body(ps_ref, alog_ref, dtb_ref, o_ref, *, n_heads):
    p = ps_ref[...]
    lane = lax.broadcasted_iota(jnp.int32, p.shape, 1)
    g_all = -jnp.exp(alog_ref[...]) * _softplus(p + dtb_ref[...])
    o_ref[...] = jnp.where(lane < n_heads, _chunk_scan(g_all, False),
                           jnp.where(lane < 2 * n_heads, _chunk_scan(g_all, True), _sigmoid(p)))


def gdn_gates(ps, alog_l, dtb_l, lay, H):
    T = ps.shape[0]
    S = lay.S
    one = pl.BlockSpec((1, LANES), lambda b: (0, 0))
    return pl.pallas_call(
        functools.partial(_gdn_gates_body, n_heads=H),
        out_shape=jax.ShapeDtypeStruct((T, LANES), F32),
        grid=(lay.B,),
        in_specs=[pl.BlockSpec((S, LANES), lambda b: (b, 0)), one, one],
        out_specs=pl.BlockSpec((S, LANES), lambda b: (b, 0)),
        compiler_params=_params("parallel"),
        name="gdn_gates",
    )(ps, alog_l, dtb_l)


def _gdn_body(q_ref, k_ref, v_ref, z_ref, ps_ref, cq_ref, ck_ref, cv_ref, g_ref, sel_ref, o_ref,
              qs, ks, vs, gc, bb, p_s, r_s, k_s, n_s, dl_s, o_s, *, lay, n_heads, group):
    h = pl.program_id(1)
    S = lay.S
    nc = S // CHUNK
    shape = (S, HEAD_DIM)
    prev_ok, next_ok = _conv_masks(shape, lay)
    lane = lax.broadcasted_iota(jnp.int32, shape, 1)

    q = _silu(_centred_conv(q_ref[...], cq_ref[...], prev_ok, next_ok))
    qs[...] = q * lax.rsqrt(jnp.sum(q * q, axis=-1, keepdims=True) + EPS) * HEAD_DIM ** -0.5
    k = _silu(_centred_conv(k_ref[...], ck_ref[...], prev_ok, next_ok))
    ks[...] = k * lax.rsqrt(jnp.sum(k * k, axis=-1, keepdims=True) + EPS)
    vs[...] = _silu(_centred_conv(v_ref[...], cv_ref[...], prev_ok, next_ok))

    p = ps_ref[...]
    for d in range(2):
        gsel = jnp.sum(jnp.where(lane == d * n_heads + h, p, 0.0), axis=-1, keepdims=True)
        gc[d] = jnp.broadcast_to(gsel, shape)
        bsel = jnp.sum(jnp.where(lane == (2 + d) * n_heads + h, p, 0.0), axis=-1, keepdims=True)
        bb[d] = jnp.broadcast_to(bsel, shape)

    G = group
    ii = lax.broadcasted_iota(jnp.int32, (G, CHUNK, CHUNK), 1)
    jj = lax.broadcasted_iota(jnp.int32, (G, CHUNK, CHUNK), 2)
    bmm, bmm_nt, bmm_tn = _bmm, _bmm_nt, _bmm_tn

    def prep(gi, d):
        rows = G * CHUNK
        sl = pl.ds(pl.multiple_of(gi * rows, rows), rows)
        cs = pl.ds(gi * G, G)
        r3 = lambda x: x.reshape(G, CHUNK, HEAD_DIM)
        q3, k3, v3, gcb, b3 = r3(qs[sl, :]), r3(ks[sl, :]), r3(vs[sl, :]), r3(gc[d, sl, :]), r3(bb[d, sl, :])
        glast = gcb[:, CHUNK - 1:CHUNK, :] if d == 0 else gcb[:, 0:1, :]
        eg = jnp.exp(gcb)
        kb = k3 * b3
        diff = gcb[:, :, :CHUNK] - jnp.swapaxes(gcb, 1, 2)[:, :CHUNK, :]
        incl = (ii >= jj) if d == 0 else (ii <= jj)
        strict = (ii > jj) if d == 0 else (ii < jj)
        decay = jnp.exp(jnp.where(incl, diff, MASK_NEG))
        kbf = k3.astype(BF16)
        a = jnp.where(strict, bmm_nt(kb.astype(BF16), kbf) * decay, 0.0)
        tb = _unit_tri_inverse(a, sel_ref[...], ii, jj, d).astype(BF16)
        ub = bmm(tb, (v3 * b3).astype(BF16)).astype(BF16)
        wb = bmm(tb, (kb * eg).astype(BF16)).astype(BF16)
        attb = (bmm_nt(q3.astype(BF16), kbf) * decay).astype(BF16)
        kgb = (k3 * jnp.exp(glast - gcb)).astype(BF16)
        p_s[d, sl, :] = (q3 * eg - bmm(attb, wb)).astype(BF16).reshape(rows, HEAD_DIM)
        r_s[d, sl, :] = bmm(attb, ub).reshape(rows, HEAD_DIM)
        k_s[d, cs, :, :] = bmm_tn(kgb, wb).astype(BF16)
        n_s[d, cs, :, :] = bmm_tn(kgb, ub)
        dl_s[d, cs, :, :] = jnp.broadcast_to(jnp.exp(glast), (G, 8, HEAD_DIM))

    def prep_loop(gi, carry):
        prep(gi, 0)
        prep(gi, 1)
        return carry

    lax.fori_loop(0, nc // G, prep_loop, 0)

    def step_d(d, n, s):
        c = _chunk_of_step(n, d, lay)
        sl = pl.ds(pl.multiple_of(c * CHUNK, CHUNK), CHUNK)
        sb = s.astype(BF16)
        o_s[d, sl, :] = r_s[d, sl, :] + jnp.dot(p_s[d, sl, :], sb, preferred_element_type=F32)
        dl = dl_s[d, pl.ds(c, 1), :, :][0][0:1, :]
        return (s * dl + n_s[d, pl.ds(c, 1), :, :][0]
                - jnp.dot(k_s[d, pl.ds(c, 1), :, :][0], sb, preferred_element_type=F32))

    def both(n, carry):
        return (step_d(0, n, carry[0]), step_d(1, n, carry[1]))

    zero = jnp.zeros((HEAD_DIM, HEAD_DIM), F32)
    lax.fori_loop(0, nc, both, (zero, zero))
    o_ref[...] = _gated_head_norm(o_s[0] + o_s[1], z_ref[...], g_ref[...])


def gated_deltanet(px, gates, conv_w, norm_g, lay, W, col0):
    T = px.shape[0]
    H = W // HEAD_DIM
    S = lay.S
    nc = S // CHUNK

    def col(o):
        return pl.BlockSpec((S, HEAD_DIM), lambda b, h: (b, col0 + o * H + h))

    def cw(o):
        return pl.BlockSpec((3, HEAD_DIM), lambda b, h: (0, o * H + h))

    one = pl.BlockSpec((1, HEAD_DIM), lambda b, h: (0, 0))
    sh = (S, HEAD_DIM)
    st = (2, nc, HEAD_DIM, HEAD_DIM)
    group = max(g for g in range(1, 10) if nc % g == 0)
    return pl.pallas_call(
        functools.partial(_gdn_body, lay=lay, n_heads=H, group=group),
        out_shape=jax.ShapeDtypeStruct((T, W), BF16),
        grid=(lay.B, H),
        in_specs=[col(0), col(1), col(2), col(3),
                  pl.BlockSpec((S, LANES), lambda b, h: (b, 0)),
                  cw(0), cw(1), cw(2), one,
                  pl.BlockSpec((CHUNK, TRI_BLOCK * LANES), lambda b, h: (0, 0))],
        out_specs=pl.BlockSpec((S, HEAD_DIM), lambda b, h: (b, h)),
        scratch_shapes=[pltpu.VMEM(sh, F32), pltpu.VMEM(sh, F32), pltpu.VMEM(sh, F32),
                        pltpu.VMEM((2,) + sh, F32), pltpu.VMEM((2,) + sh, F32),
                        pltpu.VMEM((2,) + sh, BF16), pltpu.VMEM((2,) + sh, F32),
                        pltpu.VMEM(st, BF16), pltpu.VMEM(st, F32),
                        pltpu.VMEM((2, nc, 8, HEAD_DIM), F32), pltpu.VMEM((2,) + sh, F32)],
        compiler_params=_params("parallel", "arbitrary"),
        name="gated_deltanet",
    )(px, px, px, px, gates, conv_w, conv_w, conv_w, norm_g.reshape(1, HEAD_DIM), _column_spread_matrix())


def _hgrn_body(q_ref, f0_ref, f1_ref, i_ref, z_ref, lb0_ref, lb1_ref, g_ref, o_ref,
               gc, kk, o_s, *, lay, layer):
    S = lay.S
    nc = S // CHUNK
    for d, (f_ref, lb_ref) in enumerate(((f0_ref, lb0_ref), (f1_ref, lb1_ref))):
        lg = lb_ref[...]
        depth = lg.shape[0]
        mx = lg[0:1]
        for r in range(1, depth):
            mx = jnp.maximum(mx, lg[r:r + 1])
        e = [jnp.exp(lg[r:r + 1] - mx) for r in range(depth)]
        tot = e[0]
        for r in range(1, depth):
            tot = tot + e[r]
        lb = jnp.zeros_like(mx)
        for r in range(1, layer + 1):
            lb = lb + e[r] / tot
        f = f_ref[...]
        e_neg = jnp.exp(-jnp.abs(f))
        big = 1.0 / (1.0 + e_neg)
        small = e_neg * big
        pos = f >= 0.0
        logf = jnp.log(jnp.maximum(lb, LB_FLOOR) + (1.0 - lb) * jnp.where(pos, big, small))
        gc[d] = _chunk_cumsum(logf, reverse=(d == 1))
        kk[d] = (1.0 - lb) * jnp.where(pos, small, big)

    SB = 16
    nsb = CHUNK // SB
    jrow = lax.broadcasted_iota(jnp.int32, (CHUNK, HEAD_DIM), 0)
    jsub = lax.broadcasted_iota(jnp.int32, (SB, HEAD_DIM), 0)
    alane = lax.broadcasted_iota(jnp.int32, (SB, CHUNK), 1)

    def intra_t(d, qc, kc, g):
        if d == 0:
            refs = [g[0:1, :]] + [g[SB * I - 1:SB * I, :] for I in range(1, nsb)]
            far = range(1, nsb)
        else:
            refs = [g[SB * (I + 1):SB * (I + 1) + 1, :] for I in range(nsb - 1)] + [g[CHUNK - 1:CHUNK, :]]
            far = range(nsb - 1)
        rvec = jnp.concatenate([jnp.broadcast_to(r, (SB, HEAD_DIM)) for r in refs], axis=0)
        qt = qc * jnp.exp(g - rvec)
        off = None
        for I in far:
            seen = (jrow < SB * I) if d == 0 else (jrow >= SB * (I + 1))
            kt = (kc * jnp.exp(jnp.where(seen, refs[I] - g, MASK_NEG))).astype(BF16)
            qi = jnp.where((jrow // SB) == I, qt, 0.0).astype(BF16)
            part = lax.dot_general(kt, qi, _NT, preferred_element_type=F32)
            off = part if off is None else off + part
        blocks = []
        for I in range(nsb):
            rs = slice(SB * I, SB * (I + 1))
            g_i, k_i = g[rs, :], kc[rs, :]
            acc = jnp.zeros((SB, CHUNK), F32)
            for r in range(SB):
                i = SB * I + r
                keep = (jsub <= r) if d == 0 else (jsub >= r)
                rel = jnp.exp(jnp.where(keep, g[i:i + 1, :] - g_i, MASK_NEG))
                colv = jnp.sum(rel * k_i * qc[i:i + 1, :], axis=-1, keepdims=True)
                acc = jnp.where(alane == i, colv, acc)
            blocks.append(acc)
        return jnp.concatenate(blocks, axis=0) + off

    def step_d(d, n, st):
        c = _chunk_of_step(n, d, lay)
        sl = pl.ds(pl.multiple_of(c * CHUNK, CHUNK), CHUNK)
        qc, kc, vc, g = q_ref[sl, :], kk[d, sl, :], i_ref[sl, :], gc[d, sl, :]
        glast = g[CHUNK - 1:CHUNK, :] if d == 0 else g[0:1, :]
        a_t = intra_t(d, qc, kc, g)
        vb = vc.astype(BF16)
        o = (lax.dot_general((qc * jnp.exp(g)).astype(BF16), st.astype(BF16), _NT, preferred_element_type=F32)
             + lax.dot_general(a_t.astype(BF16), vb, _TN, preferred_element_type=F32))
        o_s[d, sl, :] = o
        kg = (kc * jnp.exp(glast - g)).astype(BF16)
        return st * jnp.exp(glast) + lax.dot_general(vb, kg, _TN, preferred_element_type=F32)

    def both(n, carry):
        return (step_d(0, n, carry[0]), step_d(1, n, carry[1]))

    zero = jnp.zeros((HEAD_DIM, HEAD_DIM), F32)
    lax.fori_loop(0, nc, both, (zero, zero), unroll=4 if nc % 4 == 0 else 1)
    o_ref[...] = _gated_head_norm(o_s[0] + o_s[1], z_ref[...], g_ref[...])


def hgrn2(px, lb_logits, norm_g, lay, W, col0, layer):
    T = px.shape[0]
    H = W // HEAD_DIM
    S = lay.S
    depth = lb_logits.shape[0]

    def col(o):
        return pl.BlockSpec((S, HEAD_DIM), lambda b, h: (b, col0 + o * H + h))

    def lbs(d):
        return pl.BlockSpec((depth, HEAD_DIM), lambda b, h: (0, d * H + h))

    sh = (S, HEAD_DIM)
    return pl.pallas_call(
        functools.partial(_hgrn_body, lay=lay, layer=layer),
        out_shape=jax.ShapeDtypeStruct((T, W), BF16),
        grid=(lay.B, H),
        in_specs=[col(0), col(1), col(2), col(3), col(4), lbs(0), lbs(1),
                  pl.BlockSpec((1, HEAD_DIM), lambda b, h: (0, 0))],
        out_specs=pl.BlockSpec((S, HEAD_DIM), lambda b, h: (b, h)),
        scratch_shapes=[pltpu.VMEM((2,) + sh, F32), pltpu.VMEM((2,) + sh, F32), pltpu.VMEM((2,) + sh, F32)],
        compiler_params=_params("parallel", "arbitrary"),
        name="hgrn2",
    )(px, px, px, px, px, lb_logits, lb_logits, norm_g.reshape(1, HEAD_DIM))


def _split_bf16(x):
    hi = x.astype(BF16)
    return hi, (x - hi.astype(F32)).astype(BF16)


def _dot3(a_hi, a_lo, b_hi, b_lo):
    return (jnp.dot(a_hi, b_hi, preferred_element_type=F32)
            + jnp.dot(a_hi, b_lo, preferred_element_type=F32)
            + jnp.dot(a_lo, b_hi, preferred_element_type=F32))


def _hy_filter_body(z_ref, w1_ref, b1_ref, w2_ref, b2_ref, w3_ref, fr_ref, dl_ref, o_ref, *, L):
    i = pl.program_id(0)
    z = z_ref[...]
    tr = z.shape[0]
    W = o_ref.shape[1]
    h = jnp.sin(fr_ref[0:1, :] * (jnp.dot(z, w1_ref[...], precision=HIGHEST, preferred_element_type=F32) + b1_ref[...]))
    h = jnp.sin(fr_ref[1:2, :] * (jnp.dot(h, w2_ref[...], precision=HIGHEST, preferred_element_type=F32) + b2_ref[...]))
    h = jnp.dot(h, w3_ref[...], precision=HIGHEST, preferred_element_type=F32)
    h = h * jnp.exp(-z[:, 0:1] * dl_ref[...])
    row = lax.broadcasted_iota(jnp.int32, (tr, W), 0) + i * tr
    o_ref[...] = jnp.where(row < L, h[:, :W], jnp.where(row == L, 0.0, h[:, W:]))


def hyena_filter(z2, w1p, b1, w2, b2, w3, freq, deltas2, L):
    n = z2.shape[0]
    W = w3.shape[1] // 2
    O = w2.shape[0]
    tr = _tile(n, 512)
    full = lambda a: pl.BlockSpec(a.shape, lambda i: (0,) * a.ndim)
    args = (w1p, b1.reshape(1, O), w2, b2.reshape(1, O), w3, freq, deltas2)
    return pl.pallas_call(
        functools.partial(_hy_filter_body, L=L),
        out_shape=jax.ShapeDtypeStruct((n, W), F32),
        grid=(n // tr,),
        in_specs=[pl.BlockSpec((tr, LANES), lambda i: (i, 0))] + [full(a) for a in args],
        out_specs=pl.BlockSpec((tr, W), lambda i: (i, 0)),
        compiler_params=_params("parallel"),
        name="hyena_filter",
    )(z2, *args)


def _hy_prep_body(v_ref, x0_ref, x1_ref, cv_ref, c0_ref, c1_ref, bv_ref, b0_ref, b1_ref,
                  vgx_ref, vge_ref, x0u_ref, *, lay):
    shape = v_ref.shape
    prev_ok, next_ok = _conv_masks(shape, lay)
    v = _centred_conv(v_ref[...], cv_ref[...], prev_ok, next_ok) + bv_ref[...]
    x0 = _centred_conv(x0_ref[...], c0_ref[...], prev_ok, next_ok) + b0_ref[...]
    x1 = _centred_conv(x1_ref[...], c1_ref[...], prev_ok, next_ok) + b1_ref[...]
    vg = v * x1
    vge_ref[...] = vg[:lay.Lc, :]
    vgx_ref[...] = vg[lay.Lc:, :]
    x0u_ref[...] = x0


def hyena_prep(px, conv_w, conv_b, lay, W, col0):
    T = px.shape[0]
    H = W // HEAD_DIM
    S, L, Lc, B = lay.S, lay.L, lay.Lc, lay.B

    def col(o):
        return pl.BlockSpec((S, HEAD_DIM), lambda b, j: (b, col0 + o * H + j))

    def cw(o, rows):
        return pl.BlockSpec((rows, HEAD_DIM), lambda b, j: (0, o * H + j))

    return pl.pallas_call(
        functools.partial(_hy_prep_body, lay=lay),
        out_shape=(jax.ShapeDtypeStruct((B * L, W), F32), jax.ShapeDtypeStruct((B * Lc, W), F32),
                   jax.ShapeDtypeStruct((T, W), F32)),
        grid=(B, H),
        in_specs=[col(0), col(1), col(2), cw(0, 3), cw(1, 3), cw(2, 3), cw(0, 1), cw(1, 1), cw(2, 1)],
        out_specs=(pl.BlockSpec((L, HEAD_DIM), lambda b, j: (b, j)),
                   pl.BlockSpec((Lc, HEAD_DIM), lambda b, j: (b, j)),
                   pl.BlockSpec((S, HEAD_DIM), lambda b, j: (b, j))),
        compiler_params=_params("parallel", "parallel"),
        name="hyena_prep",
    )(px, px, px, conv_w, conv_w, conv_w, conv_b, conv_b, conv_b)


def _dft_tables(L):
    n = 2 * L
    FT = min(256, L)
    sub = 64
    idx = jnp.arange(n, dtype=jnp.int32)
    r = idx % (2 * FT)
    is_im = r >= FT
    f = (idx // (2 * FT)) * FT + jnp.where(is_im, r - FT, r)
    nyq = is_im & (f == 0)
    f = jnp.where(nyq, L, f)
    step = jnp.arange(sub, dtype=jnp.int32)
    ang_hi = (2.0 * math.pi / n) * ((f[:, None] * (step * sub)[None, :]) % n).astype(F32)
    ang_lo = (2.0 * math.pi / n) * ((f[:, None] * step[None, :]) % n).astype(F32)
    ch, sh, cl, sl = jnp.cos(ang_hi), jnp.sin(ang_hi), jnp.cos(ang_lo), jnp.sin(ang_lo)
    wgt = jnp.where((f == 0) | (f == L), 1.0 / n, 2.0 / n)
    minus_sin = is_im & ~nyq

    def table(hi_steps, transpose):
        if transpose:
            c_h, s_h, c_l, s_l = (t.T for t in (ch, sh, cl, sl))
            cos = c_h[:hi_steps, None, :] * c_l[None, :, :] - s_h[:hi_steps, None, :] * s_l[None, :, :]
            sin = s_h[:hi_steps, None, :] * c_l[None, :, :] + c_h[:hi_steps, None, :] * s_l[None, :, :]
            out = jnp.where(minus_sin[None, None, :], -sin, cos) * wgt[None, None, :]
            return out.reshape(hi_steps * sub, n)
        cos = ch[:, :hi_steps, None] * cl[:, None, :] - sh[:, :hi_steps, None] * sl[:, None, :]
        sin = sh[:, :hi_steps, None] * cl[:, None, :] + ch[:, :hi_steps, None] * sl[:, None, :]
        return jnp.where(minus_sin[:, None, None], -sin, cos).reshape(n, hi_steps * sub)

    return table(n // sub, False), table(L // sub, False), table(L // sub, True), FT


def _hy_fwd_body(a_ref, u_ref, kf_ref, y_ref, *, FT):
    t = pl.program_id(2)
    x = jnp.dot(a_ref[...], u_ref[...].astype(BF16), preferred_element_type=F32)
    kf = kf_ref[...]
    xr, xi, kr, ki = x[:FT], x[FT:], kf[:FT], kf[FT:]
    row = lax.broadcasted_iota(jnp.int32, xr.shape, 0)
    packed = (row + t) == 0
    yr = jnp.where(packed, xr * kr, xr * kr - xi * ki)
    yi = jnp.where(packed, xi * ki, xr * ki + xi * kr)
    y_ref[...] = jnp.concatenate([yr, yi], axis=0).astype(BF16)


def _hy_spec_body(ah_ref, al_ref, u_ref, o_ref):
    uh, ul = _split_bf16(u_ref[...])
    o_ref[...] = _dot3(ah_ref[...], al_ref[...], uh, ul)


def hyena_spectrum(fh, fl, kern, FT):
    n, W = kern.shape
    tn = _tile(W, 256)
    return pl.pallas_call(
        _hy_spec_body,
        out_shape=jax.ShapeDtypeStruct((n, W), F32),
        grid=(W // tn, n // (2 * FT)),
        in_specs=[pl.BlockSpec((2 * FT, n), lambda j, t: (t, 0)),
                  pl.BlockSpec((2 * FT, n), lambda j, t: (t, 0)),
                  pl.BlockSpec((n, tn), lambda j, t: (0, j))],
        out_specs=pl.BlockSpec((2 * FT, tn), lambda j, t: (t, j)),
        compiler_params=_params("parallel", "arbitrary"),
        name="hyena_spectrum",
    )(fh, fl, kern)


def hyena_fwd(f_b, vg, kf, B, Lq, FT):
    W = vg.shape[1]
    n = 2 * Lq
    nt = n // (2 * FT)
    tn = _tile(W, 1024)
    return pl.pallas_call(
        functools.partial(_hy_fwd_body, FT=FT),
        out_shape=jax.ShapeDtypeStruct((B * n, W), BF16),
        grid=(B, W // tn, nt),
        in_specs=[pl.BlockSpec((2 * FT, Lq), lambda b, j, t: (t, 0)),
                  pl.BlockSpec((Lq, tn), lambda b, j, t: (b, j)),
                  pl.BlockSpec((2 * FT, tn), lambda b, j, t: (t, j))],
        out_specs=pl.BlockSpec((2 * FT, tn), lambda b, j, t: (b * nt + t, j)),
        compiler_params=_params("parallel", "parallel", "arbitrary"),
        name="hyena_fwd",
    )(f_b, vg, kf)


def _hy_inv_body(g_ref, y_ref, vg_ref, x0_ref, bias_ref, o_ref):
    y = jnp.dot(g_ref[...], y_ref[...], preferred_element_type=F32)
    o_ref[...] = (x0_ref[...] * (y + vg_ref[...] * bias_ref[...])).astype(BF16)


def hyena_inv(g_b, y_b, vg, x0u, bias, lay, Lq, row0):
    W = vg.shape[1]
    n = 2 * Lq
    B = lay.B
    tt = lay.gr
    ntt = Lq // tt
    tn = _tile(W, 1024)
    g0 = row0 // tt
    return pl.pallas_call(
        _hy_inv_body,
        out_shape=jax.ShapeDtypeStruct((B * Lq, W), BF16),
        grid=(B, W // tn, ntt),
        in_specs=[pl.BlockSpec((tt, n), lambda b, j, i: (i, 0)),
                  pl.BlockSpec((n, tn), lambda b, j, i: (b, j)),
                  pl.BlockSpec((tt, tn), lambda b, j, i: (b * ntt + i, j)),
                  pl.BlockSpec((tt, tn), lambda b, j, i: (b * lay.NG + g0 + i, j)),
                  pl.BlockSpec((1, tn), lambda b, j, i: (0, j))],
        out_specs=pl.BlockSpec((tt, tn), lambda b, j, i: (b * ntt + i, j)),
        compiler_params=_params("parallel", "parallel", "arbitrary"),
        name="hyena_inv",
    )(g_b, y_b, vg, x0u, bias.reshape(1, W))


def _hyena_features(L):
    bands = (HY_EMB - 1) // 2
    t = jnp.linspace(0.0, 1.0, L, dtype=F32)[:, None]
    wpos = 2.0 * math.pi * jnp.arange(L, dtype=F32)[:, None] / L
    fb = jnp.linspace(1e-4, bands - 1, bands, dtype=F32)[None]
    z = jnp.concatenate([t, jnp.cos(fb * wpos), -jnp.sin(fb * wpos)], axis=-1)
    z2 = jnp.concatenate([z, z[::-1]], axis=0)
    return jnp.pad(z2, ((0, 0), (0, LANES - HY_EMB)))


def hyena_stream(vg, x0u, bias, wts, lay, Lq, row0):
    w1p, b1, w2, b2, w3, freq, deltas2 = wts
    fwd_full, fwd_half, inv, FT = _dft_tables(Lq)
    kh, kl = _split_bf16(fwd_full)
    kern = hyena_filter(_hyena_features(Lq), w1p, b1, w2, b2, w3, freq, deltas2, Lq)
    kf = hyena_spectrum(kh, kl, kern, FT)
    y_b = hyena_fwd(fwd_half.astype(BF16), vg, kf, lay.B, Lq, FT)
    return hyena_inv(inv.astype(BF16), y_b, vg, x0u, bias, lay, Lq, row0)


def _rope_tables(lay):
    n_freq = DIFF_DIM // 4
    inv = ROPE_THETA ** (-jnp.arange(n_freq, dtype=F32) / n_freq)
    rows = lay.L // GRID_W
    r = jnp.repeat(jnp.arange(rows, dtype=F32), GRID_W)
    col = jnp.tile(jnp.arange(GRID_W, dtype=F32), rows)
    ang = jnp.concatenate([r[:, None] * inv, col[:, None] * inv], axis=-1)
    cos, sin = jnp.cos(ang), jnp.sin(ang)
    cos = jnp.concatenate([jnp.ones((lay.Lc, DIFF_DIM // 2), F32), cos], axis=0)
    sin = jnp.concatenate([jnp.zeros((lay.Lc, DIFF_DIM // 2), F32), sin], axis=0)
    cos_t = jnp.tile(cos, (1, 4))
    sin_t = jnp.tile(jnp.concatenate([-sin, sin], axis=-1), (1, 2))
    return cos_t, sin_t


def _main_cols(W, H):
    small0 = 7 * W
    small1 = small0 + 4 * H
    return small0, small1


def mixers(px1, px2, ps, lay, W, layer, need_ctx, p):
    H = W // HEAD_DIM
    lam_init = 0.8 - 0.6 * math.exp(-0.3 * layer)
    q_rot, k_rot = attn_prep(px1, p["cos_t"], p["sin_t"], p["qk_g2"], lay, W)
    att = attention(q_rot, k_rot, px1, p["attn_lambda"], p["attn_subln_g"], lay, W, lam_init)
    gates = gdn_gates(ps, p["alog"], p["dtb"], lay, H)
    gdn = gated_deltanet(px1, gates, p["gdn_conv_w"], p["gdn_norm_g"], lay, W, 3 * H)
    hg = hgrn2(px2, p["hg_lb_logits"], p["hg_norm_g"], lay, W, 0, layer)
    vgx, vge, x0u = hyena_prep(px2, p["hy_conv_w"], p["hy_conv_b"], lay, W, 5 * H)
    hy_x = hyena_stream(vgx, x0u, p["hy_bias"], p["hy_wts"], lay, lay.L, lay.Lc)
    if need_ctx:
        hy_e = hyena_stream(vge, x0u, p["hy_bias"], p["hy_wts"], lay, lay.Lc, 0)
    else:
        hy_e = jnp.zeros((lay.B, lay.Lc, W), BF16)
    hy = jnp.concatenate([hy_e.reshape(lay.B, lay.Lc, W), hy_x.reshape(lay.B, lay.L, W)], axis=1)
    return att, gdn, hg, hy.reshape(lay.T, W)


def _layer_params(l, lay, W, attn_qk_g, attn_lambda, attn_subln_g, gdn_conv_w, gdn_a_log, gdn_dt_bias, gdn_norm_g,
                  hg_lb_logits, hg_norm_g, hy_conv_w, hy_conv_b, hy_w1, hy_b1, hy_w2, hy_b2, hy_w3, hy_freq,
                  hy_bias, cos_t, sin_t):
    H = W // HEAD_DIM
    pad16 = lambda a: jnp.pad(a.astype(F32).reshape(1, 2 * H), ((0, 0), (0, LANES - 2 * H)))
    deltas = jnp.abs(jnp.linspace(math.log(HY_TARGET) / HY_SLOW_DECAY, math.log(HY_TARGET) / HY_FAST_DECAY, W, dtype=F32))
    return dict(
        cos_t=cos_t, sin_t=sin_t,
        qk_g2=jnp.tile(attn_qk_g[l].astype(F32), (1, 2)),
        attn_lambda=attn_lambda[l].astype(F32), attn_subln_g=attn_subln_g[l].astype(F32),
        gdn_conv_w=gdn_conv_w[l], alog=pad16(gdn_a_log[l]), dtb=pad16(gdn_dt_bias[l]), gdn_norm_g=gdn_norm_g[l],
        hg_lb_logits=hg_lb_logits.astype(F32).reshape(hg_lb_logits.shape[0], 2 * W), hg_norm_g=hg_norm_g[l],
        hy_conv_w=hy_conv_w[l], hy_conv_b=hy_conv_b[l].reshape(1, 3 * W), hy_bias=hy_bias[l].astype(F32),
        hy_wts=(jnp.pad(hy_w1[l].astype(F32), ((0, LANES - HY_EMB), (0, 0))), hy_b1[l].astype(F32),
                hy_w2[l].astype(F32), hy_b2[l].astype(F32), hy_w3[l].astype(F32), hy_freq[l].astype(F32),
                jnp.tile(deltas, 2).reshape(1, 2 * W)),
    )


def kernel(x, c, ctx, c_ctx, norm_g, w_mod, b_mod, ffn_w_in, ffn_w_out, w_in, attn_qk_g, attn_lambda, attn_subln_g, gdn_conv_w, gdn_a_log, gdn_dt_bias, gdn_norm_g, hg_lb_logits, hg_norm_g, hy_conv_w, hy_conv_b, hy_w1, hy_b1, hy_w2, hy_b2, hy_w3, hy_freq, hy_bias, w_gate, w_up, w_out):
    B, L, D = x.shape
    Lc = ctx.shape[1]
    depth = w_mod.shape[0]
    W = D // N_BRANCH
    H = W // HEAD_DIM
    lay = Layout(B, L, Lc)
    h = jnp.concatenate([ctx, x], axis=1).reshape(lay.T, D)
    G = -(-(B + 1) // 8) * 8
    c_all = jnp.concatenate([c, c_ctx[None], jnp.zeros((G - B - 1, D), F32)], axis=0)
    cos_t, sin_t = _rope_tables(lay)
    small0, small1 = _main_cols(W, H)
    w_gate_b, w_up_b = w_gate.astype(BF16), w_up.astype(BF16)
    for l in range(depth):
        last = l == depth - 1
        p = _layer_params(l, lay, W, attn_qk_g, attn_lambda, attn_subln_g, gdn_conv_w, gdn_a_log, gdn_dt_bias,
                          gdn_norm_g, hg_lb_logits, hg_norm_g, hy_conv_w, hy_conv_b, hy_w1, hy_b1, hy_w2, hy_b2,
                          hy_w3, hy_freq, hy_bias, cos_t, sin_t)
        mod = mod_table(c_all, w_mod, b_mod, l).reshape(G, MOD_CHUNKS, D)
        act = mm_swiglu(norm_mod(h, norm_g[l, 0], mod, lay, 0), ffn_w_in, (l, 0))
        h = mm_resid(act, ffn_w_out, (l, 0), h, mod, lay, 2, 0.5)
        xn = norm_mod(h, norm_g[l, 1], mod, lay, 1)
        cols_t = lambda lo, hi: w_in[l, :, lo:hi].T
        px1 = mm_plain(xn, cols_t(0, small0))
        px2 = mm_plain(xn, cols_t(small1, w_in.shape[-1]))
        ps = mm_plain(xn, jnp.pad(cols_t(small0, small1), ((0, LANES - (small1 - small0)), (0, 0))), tn_pref=LANES)
        branches = mixers(px1, px2, ps, lay, W, l, not last, p)
        if last:
            latent = lambda a: a.reshape(B, lay.S, a.shape[-1])[:, Lc:, :].reshape(B * L, a.shape[-1])
            xn, h, branches = latent(xn), latent(h), tuple(latent(a) for a in branches)
            lay = Layout(B, L, 0, gr=lay.gr)
        acc = merge_branches(xn, branches, w_gate_b, w_up_b, l)
        h = mm_resid(acc, w_out, (l,), h, mod, lay, 5, 1.0)
        act = mm_swiglu(norm_mod(h, norm_g[l, 2], mod, lay, 2), ffn_w_in, (l, 1))
        h = mm_resid(act, ffn_w_out, (l, 1), h, mod, lay, 8, 0.5)
    return h.reshape(B, L, D)
```

```python
import functools
import math

import jax
import jax.numpy as jnp
from jax import lax
from jax.experimental import pallas as pl
from jax.experimental.pallas import tpu as pltpu

F32 = jnp.float32
BF16 = jnp.bfloat16

GRID_W = 64
N_BRANCH = 4
HEAD_DIM = 128
DIFF_DIM = HEAD_DIM // 2
MOD_CHUNKS = 9
ROPE_THETA = 10000.0
CHUNK = 64
HY_EMB = 33
HY_FAST_DECAY = 0.3
HY_SLOW_DECAY = 1.5
HY_TARGET = 1e-2
EPS = 1e-6
MASK_NEG = -1e30
LB_FLOOR = 1e-20

LANES = 128
V7X_VMEM_LIMIT = 56 * 1024 * 1024
HIGHEST = lax.Precision.HIGHEST

_NT = (((1,), (1,)), ((), ()))
_TN = (((0,), (0,)), ((), ()))


def _params(*sem):
    return pltpu.CompilerParams(dimension_semantics=sem, vmem_limit_bytes=V7X_VMEM_LIMIT)


def _tile(n, pref):
    t = min(n, pref)
    while n % t:
        t //= 2
    return t


def _wspec(at, K, tn, col):
    return pl.BlockSpec((None,) * len(at) + (K, tn), lambda *ids: tuple(at) + (0, col(*ids)))


def _sigmoid(x):
    return 1.0 / (1.0 + jnp.exp(-x))


def _silu(x):
    return x * _sigmoid(x)


def _softplus(x):
    return jnp.maximum(x, 0.0) + jnp.log1p(jnp.exp(-jnp.abs(x)))


class Layout:
    def __init__(self, B, L, Lc, gr=None):
        self.B, self.L, self.Lc = B, L, Lc
        self.S = L + Lc
        self.T = B * self.S
        self.gr = gr or math.gcd(L, Lc)
        self.NG = self.S // self.gr
        self.NGc = Lc // self.gr


def _mod_row(mod_ref, lay, r, k):
    b = r // lay.NG
    g = jnp.where(r - b * lay.NG < lay.NGc, lay.B, b)
    return mod_ref[pl.ds(g, 1), k:k + 1, :][0]


def _mod_body(c_ref, w_ref, b_ref, o_ref):
    a = _silu(c_ref[...]).astype(BF16)
    o_ref[...] = jnp.dot(a, w_ref[...].astype(BF16), preferred_element_type=F32) + b_ref[...]


def mod_table(c16, w_mod, b_mod, l):
    R, D = c16.shape
    N = w_mod.shape[-1]
    tn = _tile(N, 512)
    return pl.pallas_call(
        _mod_body,
        out_shape=jax.ShapeDtypeStruct((R, N), F32),
        grid=(N // tn,),
        in_specs=[pl.BlockSpec((R, D), lambda j: (0, 0)),
                  _wspec((l,), D, tn, lambda j: j),
                  _wspec((l,), 1, tn, lambda j: j)],
        out_specs=pl.BlockSpec((R, tn), lambda j: (0, j)),
        compiler_params=_params("parallel"),
        name="mod_table",
    )(c16, w_mod, b_mod.reshape(b_mod.shape[0], 1, N))


def _norm_mod_body(x_ref, g_ref, mod_ref, o_ref, *, lay, k):
    i = pl.program_id(0)
    gr = lay.gr
    n_sub = x_ref.shape[0] // gr
    for s in range(n_sub):
        x = x_ref[s * gr:(s + 1) * gr, :]
        y = x * lax.rsqrt(jnp.mean(x * x, axis=-1, keepdims=True) + EPS) * g_ref[...]
        r = i * n_sub + s
        shift = _mod_row(mod_ref, lay, r, 3 * k)
        scale = _mod_row(mod_ref, lay, r, 3 * k + 1)
        o_ref[s * gr:(s + 1) * gr, :] = (y * (1.0 + scale) + shift).astype(BF16)


def norm_mod(h, g, mod, lay, k):
    T, D = h.shape
    tm = lay.gr
    G = mod.shape[0]
    return pl.pallas_call(
        functools.partial(_norm_mod_body, lay=lay, k=k),
        out_shape=jax.ShapeDtypeStruct((T, D), BF16),
        grid=(T // tm,),
        in_specs=[pl.BlockSpec((tm, D), lambda i: (i, 0)),
                  pl.BlockSpec((1, D), lambda i: (0, 0)),
                  pl.BlockSpec((G, MOD_CHUNKS, D), lambda i: (0, 0, 0))],
        out_specs=pl.BlockSpec((tm, D), lambda i: (i, 0)),
        compiler_params=_params("parallel"),
        name="norm_mod",
    )(h, g.reshape(1, D), mod)


def _mm_swiglu_body(a_ref, wg_ref, wu_ref, o_ref, wgb, wub):
    @pl.when(pl.program_id(1) == 0)
    def _():
        wgb[...] = wg_ref[...].astype(BF16)
        wub[...] = wu_ref[...].astype(BF16)

    a = a_ref[...]
    g = jnp.dot(a, wgb[...], preferred_element_type=F32)
    u = jnp.dot(a, wub[...], preferred_element_type=F32)
    o_ref[...] = (_silu(g) * u).astype(BF16)


def mm_swiglu(a, w, at):
    T, K = a.shape
    F = w.shape[-1] // 2
    tm, tn = _tile(T, 1024), _tile(F, 256)
    nj = F // tn
    return pl.pallas_call(
        _mm_swiglu_body,
        out_shape=jax.ShapeDtypeStruct((T, F), BF16),
        grid=(nj, T // tm),
        in_specs=[pl.BlockSpec((tm, K), lambda j, i: (i, 0)),
                  _wspec(at, K, tn, lambda j, i: j),
                  _wspec(at, K, tn, lambda j, i: j + nj)],
        out_specs=pl.BlockSpec((tm, tn), lambda j, i: (i, j)),
        scratch_shapes=[pltpu.VMEM((K, tn), BF16), pltpu.VMEM((K, tn), BF16)],
        compiler_params=_params("parallel", "arbitrary"),
        name="mm_swiglu",
    )(a, w, w)


def _mm_resid_body(a_ref, w_ref, h_ref, mod_ref, o_ref, wb, *, lay, k, scale):
    i = pl.program_id(1)

    @pl.when(i == 0)
    def _():
        wb[...] = w_ref[...].astype(BF16)

    y = jnp.dot(a_ref[...], wb[...], preferred_element_type=F32)
    gr = lay.gr
    n_sub = a_ref.shape[0] // gr
    for s in range(n_sub):
        gate = _mod_row(mod_ref, lay, i * n_sub + s, k)
        sl = slice(s * gr, (s + 1) * gr)
        o_ref[sl, :] = h_ref[sl, :] + (scale * gate) * y[sl, :]


def mm_resid(a, w, at, h, mod, lay, k, scale):
    T, K = a.shape
    N = w.shape[-1]
    G = mod.shape[0]
    def vmem_bytes(tm, tn):
        return 2 * tm * K * 2 + 2 * K * tn * 4 + K * tn * 2 + 4 * tm * tn * 4

    tm, tn = max(_tile(T, 1024), lay.gr), _tile(N, 512)
    if vmem_bytes(tm, tn) > (V7X_VMEM_LIMIT * 7) // 8:
        tm = max(_tile(T, 512), lay.gr)
    return pl.pallas_call(
        functools.partial(_mm_resid_body, lay=lay, k=k, scale=scale),
        out_shape=jax.ShapeDtypeStruct((T, N), F32),
        grid=(N // tn, T // tm),
        in_specs=[pl.BlockSpec((tm, K), lambda j, i: (i, 0)),
                  _wspec(at, K, tn, lambda j, i: j),
                  pl.BlockSpec((tm, tn), lambda j, i: (i, j)),
                  pl.BlockSpec((G, MOD_CHUNKS, tn), lambda j, i: (0, 0, j))],
        out_specs=pl.BlockSpec((tm, tn), lambda j, i: (i, j)),
        scratch_shapes=[pltpu.VMEM((K, tn), BF16)],
        compiler_params=_params("parallel", "arbitrary"),
        name="mm_resid",
    )(a, w, h, mod)


def _mm_plain_body(a_ref, wt_ref, o_ref, wb):
    @pl.when(pl.program_id(1) == 0)
    def _():
        wb[...] = wt_ref[...].T.astype(BF16)

    o_ref[...] = jnp.dot(a_ref[...], wb[...], preferred_element_type=F32)


def mm_plain(a, w_t, tn_pref=512):
    T, K = a.shape
    N = w_t.shape[0]
    tm, tn = _tile(T, 1024), _tile(N, tn_pref)
    return pl.pallas_call(
        _mm_plain_body,
        out_shape=jax.ShapeDtypeStruct((T, N), F32),
        grid=(N // tn, T // tm),
        in_specs=[pl.BlockSpec((tm, K), lambda j, i: (i, 0)),
                  pl.BlockSpec((tn, K), lambda j, i: (j, 0))],
        out_specs=pl.BlockSpec((tm, tn), lambda j, i: (i, j)),
        scratch_shapes=[pltpu.VMEM((K, tn), BF16)],
        compiler_params=_params("parallel", "arbitrary"),
        name="mm_plain",
    )(a, w_t)


def _merge_body(xn_ref, o0_ref, o1_ref, o2_ref, o3_ref, wg_ref, wu_ref, out_ref):
    xn = xn_ref[...]
    acc = None
    for br, o_ref in enumerate((o0_ref, o1_ref, o2_ref, o3_ref)):
        g = jnp.dot(xn, wg_ref[br], preferred_element_type=F32)
        u = jnp.dot(o_ref[...], wu_ref[br], preferred_element_type=F32)
        term = _sigmoid(g) * u
        acc = term if acc is None else acc + term
    out_ref[...] = acc.astype(BF16)


def merge_branches(xn, branches, w_gate_b, w_up_b, l):
    T, D = xn.shape
    W = branches[0].shape[1]
    tm, tn = _tile(T, 512), _tile(D, 256)
    bspec = pl.BlockSpec((tm, W), lambda i, j: (i, 0))
    return pl.pallas_call(
        _merge_body,
        out_shape=jax.ShapeDtypeStruct((T, D), BF16),
        grid=(T // tm, D // tn),
        in_specs=[pl.BlockSpec((tm, D), lambda i, j: (i, 0)), bspec, bspec, bspec, bspec,
                  pl.BlockSpec((None, N_BRANCH, D, tn), lambda i, j: (l, 0, 0, j)),
                  pl.BlockSpec((None, N_BRANCH, W, tn), lambda i, j: (l, 0, 0, j))],
        out_specs=pl.BlockSpec((tm, tn), lambda i, j: (i, j)),
        compiler_params=_params("parallel", "arbitrary"),
        name="merge_branches",
    )(xn, *branches, w_gate_b, w_up_b)


def _attn_prep_body(q_ref, k_ref, cos_ref, sin_ref, g_ref, qo_ref, ko_ref):
    cos = cos_ref[...]
    sin = sin_ref[...]
    lane = lax.broadcasted_iota(jnp.int32, cos.shape, 1)
    lo = lane < DIFF_DIM
    first = (lane % DIFF_DIM) < DIFF_DIM // 2
    n_heads = q_ref.shape[1] // HEAD_DIM
    for src, dst, gi, sc in ((q_ref, qo_ref, 0, DIFF_DIM ** -0.5), (k_ref, ko_ref, 1, 1.0)):
        g = g_ref[gi:gi + 1, :]
        for h in range(n_heads):
            cs = slice(h * HEAD_DIM, (h + 1) * HEAD_DIM)
            t = src[:, cs]
            ss = t * t
            s_lo = jnp.sum(jnp.where(lo, ss, 0.0), axis=-1, keepdims=True)
            s_hi = jnp.sum(jnp.where(lo, 0.0, ss), axis=-1, keepdims=True)
            inv = jnp.where(lo, lax.rsqrt(s_lo / DIFF_DIM + EPS), lax.rsqrt(s_hi / DIFF_DIM + EPS))
            y = t * inv * g
            partner = jnp.where(first, pltpu.roll(y, HEAD_DIM - DIFF_DIM // 2, 1), pltpu.roll(y, DIFF_DIM // 2, 1))
            dst[:, cs] = ((y * cos + partner * sin) * sc).astype(BF16)


def attn_prep(px, cos_t, sin_t, qk_g2, lay, W):
    T = px.shape[0]
    tr = lay.gr
    return pl.pallas_call(
        _attn_prep_body,
        out_shape=(jax.ShapeDtypeStruct((T, W), BF16), jax.ShapeDtypeStruct((T, W), BF16)),
        grid=(T // tr,),
        in_specs=[pl.BlockSpec((tr, W), lambda i: (i, 0)),
                  pl.BlockSpec((tr, W), lambda i: (i, 1)),
                  pl.BlockSpec((tr, LANES), lambda i: (i % lay.NG, 0)),
                  pl.BlockSpec((tr, LANES), lambda i: (i % lay.NG, 0)),
                  pl.BlockSpec((2, LANES), lambda i: (0, 0))],
        out_specs=(pl.BlockSpec((tr, W), lambda i: (i, 0)), pl.BlockSpec((tr, W), lambda i: (i, 0))),
        compiler_params=_params("parallel"),
        name="attn_prep",
    )(px, px, cos_t, sin_t, qk_g2)


def _attn_body(q_ref, k_ref, v_ref, lam_ref, g_ref, o_ref, *, lay, lam_init):
    qi = pl.program_id(2)
    lp = lam_ref[...]
    lam = (jnp.exp(jnp.sum(lp[0:1] * lp[1:2], axis=-1, keepdims=True))
           - jnp.exp(jnp.sum(lp[2:3] * lp[3:4], axis=-1, keepdims=True)) + lam_init)

    def run(nk):
        q = q_ref[...]
        k = k_ref[0:nk, :]
        v = v_ref[0:nk, :].astype(BF16)
        lane = lax.broadcasted_iota(jnp.int32, q.shape, 1)
        zero = jnp.zeros_like(q)
        outs = []
        for qq in (jnp.where(lane < DIFF_DIM, q, zero), jnp.where(lane < DIFF_DIM, zero, q)):
            s = lax.dot_general(qq, k, _NT, preferred_element_type=F32)
            p = jnp.exp(s - jnp.max(s, axis=-1, keepdims=True))
            l = jnp.sum(p, axis=-1, keepdims=True)
            outs.append(jnp.dot(p.astype(BF16), v, preferred_element_type=F32) / l)
        o = outs[0] - lam * outs[1]
        y = o * lax.rsqrt(jnp.mean(o * o, axis=-1, keepdims=True) + EPS) * g_ref[...]
        o_ref[...] = (y * (1.0 - lam_init)).astype(BF16)

    @pl.when(qi < lay.NGc)
    def _():
        run(lay.Lc)

    @pl.when(qi >= lay.NGc)
    def _():
        run(lay.S)


def attention(q_rot, k_rot, px, lam_p, subln_g, lay, W, lam_init):
    T = px.shape[0]
    H = W // HEAD_DIM
    tq = lay.gr
    S = lay.S
    vb = W // HEAD_DIM * 2
    return pl.pallas_call(
        functools.partial(_attn_body, lay=lay, lam_init=lam_init),
        out_shape=jax.ShapeDtypeStruct((T, W), BF16),
        grid=(lay.B, H, lay.NG),
        in_specs=[pl.BlockSpec((tq, HEAD_DIM), lambda b, h, i: (b * lay.NG + i, h)),
                  pl.BlockSpec((S, HEAD_DIM), lambda b, h, i: (b, h)),
                  pl.BlockSpec((S, HEAD_DIM), lambda b, h, i: (b, vb + h)),
                  pl.BlockSpec((4, DIFF_DIM), lambda b, h, i: (0, 0)),
                  pl.BlockSpec((1, HEAD_DIM), lambda b, h, i: (0, 0))],
        out_specs=pl.BlockSpec((tq, HEAD_DIM), lambda b, h, i: (b * lay.NG + i, h)),
        compiler_params=_params("parallel", "parallel", "arbitrary"),
        name="diff_attention",
    )(q_rot, k_rot, px, lam_p, subln_g.reshape(1, HEAD_DIM))


def _chunk_scan(y, reverse):
    S = y.shape[0]
    pos = lax.broadcasted_iota(jnp.int32, y.shape, 0) % CHUNK
    sh = 1
    while sh < CHUNK:
        if reverse:
            y = y + jnp.where(pos < CHUNK - sh, pltpu.roll(y, S - sh, 0), 0.0)
        else:
            y = y + jnp.where(pos >= sh, pltpu.roll(y, sh, 0), 0.0)
        sh *= 2
    return y


def _chunk_cumsum(y, reverse):
    S, C = y.shape
    nc = S // CHUNK
    ii = lax.broadcasted_iota(jnp.int32, (nc, CHUNK, CHUNK), 1)
    jj = lax.broadcasted_iota(jnp.int32, (nc, CHUNK, CHUNK), 2)
    tri = jnp.where((ii <= jj) if reverse else (ii >= jj), 1.0, 0.0).astype(BF16)
    y3 = y.reshape(nc, CHUNK, C)
    p1 = y3.astype(BF16)
    r1 = y3 - p1.astype(F32)
    p2 = r1.astype(BF16)
    p3 = (r1 - p2.astype(F32)).astype(BF16)
    out = _bmm(tri, p1) + _bmm(tri, p2) + _bmm(tri, p3)
    return out.reshape(S, C)


def _chunk_of_step(n, d, lay):
    ncc = lay.Lc // CHUNK
    nc = lay.S // CHUNK
    if d == 0:
        return n
    return jnp.where(n < ncc, ncc - 1 - n, nc - 1 - (n - ncc))


def _gated_head_norm(o, z, g):
    y = o * lax.rsqrt(jnp.mean(o * o, axis=-1, keepdims=True) + EPS) * g
    return (y * _silu(z)).astype(BF16)


def _centred_conv(x, w, prev_ok, next_ok):
    S = x.shape[0]
    xp = jnp.where(prev_ok, pltpu.roll(x, 1, 0), 0.0)
    xn = jnp.where(next_ok, pltpu.roll(x, S - 1, 0), 0.0)
    return xp * w[0:1] + x * w[1:2] + xn * w[2:3]


def _conv_masks(shape, lay):
    row = lax.broadcasted_iota(jnp.int32, shape, 0)
    prev_ok = (row * (row - lay.Lc)) != 0
    next_ok = ((row - (lay.Lc - 1)) * (row - (lay.S - 1))) != 0
    return prev_ok, next_ok


TRI_BLOCK = 8


def _bmm(x, y):
    return jnp.einsum('gij,gjk->gik', x, y, preferred_element_type=F32)


def _bmm_nt(x, y):
    return jnp.einsum('gik,gjk->gij', x, y, preferred_element_type=F32)


def _bmm_tn(x, y):
    return jnp.einsum('gjk,gjl->gkl', x, y, preferred_element_type=F32)


def _column_spread_matrix():
    m = jnp.arange(CHUNK)[:, None]
    c = jnp.arange(TRI_BLOCK * LANES)[None, :]
    j, lane = c // LANES, c % LANES
    hit = (lane < CHUNK) & (m % TRI_BLOCK == j) & (m // TRI_BLOCK == lane // TRI_BLOCK)
    return hit.astype(BF16)


def _unit_tri_inverse(a, spread, ii, jj, d):
    G = a.shape[0]
    nb = TRI_BLOCK
    a_d = jnp.where(ii // nb == jj // nb, a, 0.0).reshape(G * CHUNK, CHUNK)
    cols = jnp.dot(a_d.astype(BF16), spread, preferred_element_type=F32)
    row = lax.broadcasted_iota(jnp.int32, (G, CHUNK, LANES), 1)
    lane = lax.broadcasted_iota(jnp.int32, (G, CHUNK, LANES), 2)
    x = jnp.where(row == lane, 1.0, 0.0)
    for j in (range(nb - 1) if d == 0 else range(nb - 1, 0, -1)):
        col_j = cols[:, j * LANES:(j + 1) * LANES].reshape(G, CHUNK, LANES)
        x4 = x.reshape(G, CHUNK // nb, nb, LANES)
        row_j = jnp.broadcast_to(x4[:, :, j:j + 1, :], x4.shape).reshape(G, CHUNK, LANES)
        x = x - col_j * row_j
    t = x[:, :, :CHUNK]
    width = nb
    while width < CHUNK:
        e = jnp.where(((ii // width) ^ (jj // width)) == 1, a, 0.0).astype(BF16)
        tb = t.astype(BF16)
        t = t - _bmm(tb, _bmm(e, tb).astype(BF16))
        width *= 2
    return t


def _gdn_gates_body(ps_ref, alog_ref, dtb_ref, o_ref, *, n_heads):
    p = ps_ref[...]
    lane = lax.broadcasted_iota(jnp.int32, p.shape, 1)
    g_all = -jnp.exp(alog_ref[...]) * _softplus(p + dtb_ref[...])
    o_ref[...] = jnp.where(lane < n_heads, _chunk_scan(g_all, False),
                           jnp.where(lane < 2 * n_heads, _chunk_scan(g_all, True), _sigmoid(p)))


def gdn_gates(ps, alog_l, dtb_l, lay, H):
    T = ps.shape[0]
    S = lay.S
    one = pl.BlockSpec((1, LANES), lambda b: (0, 0))
    return pl.pallas_call(
        functools.partial(_gdn_gates_body, n_heads=H),
        out_shape=jax.ShapeDtypeStruct((T, LANES), F32),
        grid=(lay.B,),
        in_specs=[pl.BlockSpec((S, LANES), lambda b: (b, 0)), one, one],
        out_specs=pl.BlockSpec((S, LANES), lambda b: (b, 0)),
        compiler_params=_params("parallel"),
        name="gdn_gates",
    )(ps, alog_l, dtb_l)


def _gdn_body(q_ref, k_ref, v_ref, z_ref, ps_ref, cq_ref, ck_ref, cv_ref, g_ref, sel_ref, o_ref,
              qs, ks, vs, gc, bb, p_s, r_s, k_s, n_s, dl_s, o_s, *, lay, n_heads, group):
    h = pl.program_id(1)
    S = lay.S
    nc = S // CHUNK
    shape = (S, HEAD_DIM)
    prev_ok, next_ok = _conv_masks(shape, lay)
    lane = lax.broadcasted_iota(jnp.int32, shape, 1)

    q = _silu(_centred_conv(q_ref[...], cq_ref[...], prev_ok, next_ok))
    qs[...] = q * lax.rsqrt(jnp.sum(q * q, axis=-1, keepdims=True) + EPS) * HEAD_DIM ** -0.5
    k = _silu(_centred_conv(k_ref[...], ck_ref[...], prev_ok, next_ok))
    ks[...] = k * lax.rsqrt(jnp.sum(k * k, axis=-1, keepdims=True) + EPS)
    vs[...] = _silu(_centred_conv(v_ref[...], cv_ref[...], prev_ok, next_ok))

    p = ps_ref[...]
    for d in range(2):
        gsel = jnp.sum(jnp.where(lane == d * n_heads + h, p, 0.0), axis=-1, keepdims=True)
        gc[d] = jnp.broadcast_to(gsel, shape)
        bsel = jnp.sum(jnp.where(lane == (2 + d) * n_heads + h, p, 0.0), axis=-1, keepdims=True)
        bb[d] = jnp.broadcast_to(bsel, shape)

    G = group
    ii = lax.broadcasted_iota(jnp.int32, (G, CHUNK, CHUNK), 1)
    jj = lax.broadcasted_iota(jnp.int32, (G, CHUNK, CHUNK), 2)
    bmm, bmm_nt, bmm_tn = _bmm, _bmm_nt, _bmm_tn

    def prep(gi, d):
        rows = G * CHUNK
        sl = pl.ds(pl.multiple_of(gi * rows, rows), rows)
        cs = pl.ds(gi * G, G)
        r3 = lambda x: x.reshape(G, CHUNK, HEAD_DIM)
        q3, k3, v3, gcb, b3 = r3(qs[sl, :]), r3(ks[sl, :]), r3(vs[sl, :]), r3(gc[d, sl, :]), r3(bb[d, sl, :])
        glast = gcb[:, CHUNK - 1:CHUNK, :] if d == 0 else gcb[:, 0:1, :]
        eg = jnp.exp(gcb)
        kb = k3 * b3
        diff = gcb[:, :, :CHUNK] - jnp.swapaxes(gcb, 1, 2)[:, :CHUNK, :]
        incl = (ii >= jj) if d == 0 else (ii <= jj)
        strict = (ii > jj) if d == 0 else (ii < jj)
        decay = jnp.exp(jnp.where(incl, diff, MASK_NEG))
        kbf = k3.astype(BF16)
        a = jnp.where(strict, bmm_nt(kb.astype(BF16), kbf) * decay, 0.0)
        tb = _unit_tri_inverse(a, sel_ref[...], ii, jj, d).astype(BF16)
        ub = bmm(tb, (v3 * b3).astype(BF16)).astype(BF16)
        wb = bmm(tb, (kb * eg).astype(BF16)).astype(BF16)
        attb = (bmm_nt(q3.astype(BF16), kbf) * decay).astype(BF16)
        kgb = (k3 * jnp.exp(glast - gcb)).astype(BF16)
        p_s[d, sl, :] = (q3 * eg - bmm(attb, wb)).astype(BF16).reshape(rows, HEAD_DIM)
        r_s[d, sl, :] = bmm(attb, ub).reshape(rows, HEAD_DIM)
        k_s[d, cs, :, :] = bmm_tn(kgb, wb).astype(BF16)
        n_s[d, cs, :, :] = bmm_tn(kgb, ub)
        dl_s[d, cs, :, :] = jnp.broadcast_to(jnp.exp(glast), (G, 8, HEAD_DIM))

    def prep_loop(gi, carry):
        prep(gi, 0)
        prep(gi, 1)
        return carry

    lax.fori_loop(0, nc // G, prep_loop, 0)

    def step_d(d, n, s):
        c = _chunk_of_step(n, d, lay)
        sl = pl.ds(pl.multiple_of(c * CHUNK, CHUNK), CHUNK)
        sb = s.astype(BF16)
        o_s[d, sl, :] = r_s[d, sl, :] + jnp.dot(p_s[d, sl, :], sb, preferred_element_type=F32)
        dl = dl_s[d, pl.ds(c, 1), :, :][0][0:1, :]
        return (s * dl + n_s[d, pl.ds(c, 1), :, :][0]
                - jnp.dot(k_s[d, pl.ds(c, 1), :, :][0], sb, preferred_element_type=F32))

    def both(n, carry):
        return (step_d(0, n, carry[0]), step_d(1, n, carry[1]))

    zero = jnp.zeros((HEAD_DIM, HEAD_DIM), F32)
    lax.fori_loop(0, nc, both, (zero, zero))
    o_ref[...] = _gated_head_norm(o_s[0] + o_s[1], z_ref[...], g_ref[...])


def gated_deltanet(px, gates, conv_w, norm_g, lay, W, col0):
    T = px.shape[0]
    H = W // HEAD_DIM
    S = lay.S
    nc = S // CHUNK

    def col(o):
        return pl.BlockSpec((S, HEAD_DIM), lambda b, h: (b, col0 + o * H + h))

    def cw(o):
        return pl.BlockSpec((3, HEAD_DIM), lambda b, h: (0, o * H + h))

    one = pl.BlockSpec((1, HEAD_DIM), lambda b, h: (0, 0))
    sh = (S, HEAD_DIM)
    st = (2, nc, HEAD_DIM, HEAD_DIM)
    group = max(g for g in range(1, 37) if nc % g == 0)
    return pl.pallas_call(
        functools.partial(_gdn_body, lay=lay, n_heads=H, group=group),
        out_shape=jax.ShapeDtypeStruct((T, W), BF16),
        grid=(lay.B, H),
        in_specs=[col(0), col(1), col(2), col(3),
                  pl.BlockSpec((S, LANES), lambda b, h: (b, 0)),
                  cw(0), cw(1), cw(2), one,
                  pl.BlockSpec((CHUNK, TRI_BLOCK * LANES), lambda b, h: (0, 0))],
        out_specs=pl.BlockSpec((S, HEAD_DIM), lambda b, h: (b, h)),
        scratch_shapes=[pltpu.VMEM(sh, F32), pltpu.VMEM(sh, F32), pltpu.VMEM(sh, F32),
                        pltpu.VMEM((2,) + sh, F32), pltpu.VMEM((2,) + sh, F32),
                        pltpu.VMEM((2,) + sh, BF16), pltpu.VMEM((2,) + sh, F32),
                        pltpu.VMEM(st, BF16), pltpu.VMEM(st, F32),
                        pltpu.VMEM((2, nc, 8, HEAD_DIM), F32), pltpu.VMEM((2,) + sh, F32)],
        compiler_params=_params("parallel", "arbitrary"),
        name="gated_deltanet",
    )(px, px, px, px, gates, conv_w, conv_w, conv_w, norm_g.reshape(1, HEAD_DIM), _column_spread_matrix())


def _hgrn_body(q_ref, f0_ref, f1_ref, i_ref, z_ref, lb0_ref, lb1_ref, g_ref, o_ref,
               gc, kk, o_s, *, lay, layer):
    S = lay.S
    nc = S // CHUNK
    for d, (f_ref, lb_ref) in enumerate(((f0_ref, lb0_ref), (f1_ref, lb1_ref))):
        lg = lb_ref[...]
        depth = lg.shape[0]
        mx = lg[0:1]
        for r in range(1, depth):
            mx = jnp.maximum(mx, lg[r:r + 1])
        e = [jnp.exp(lg[r:r + 1] - mx) for r in range(depth)]
        tot = e[0]
        for r in range(1, depth):
            tot = tot + e[r]
        lb = jnp.zeros_like(mx)
        for r in range(1, layer + 1):
            lb = lb + e[r] / tot
        f = f_ref[...]
        e_neg = jnp.exp(-jnp.abs(f))
        big = 1.0 / (1.0 + e_neg)
        small = e_neg * big
        pos = f >= 0.0
        logf = jnp.log(jnp.maximum(lb, LB_FLOOR) + (1.0 - lb) * jnp.where(pos, big, small))
        gc[d] = _chunk_cumsum(logf, reverse=(d == 1))
        kk[d] = (1.0 - lb) * jnp.where(pos, small, big)

    SB = 16
    nsb = CHUNK // SB
    jrow = lax.broadcasted_iota(jnp.int32, (CHUNK, HEAD_DIM), 0)
    jsub = lax.broadcasted_iota(jnp.int32, (SB, HEAD_DIM), 0)
    alane = lax.broadcasted_iota(jnp.int32, (SB, CHUNK), 1)

    def intra_t(d, qc, kc, g):
        if d == 0:
            refs = [g[0:1, :]] + [g[SB * I - 1:SB * I, :] for I in range(1, nsb)]
            far = range(1, nsb)
        else:
            refs = [g[SB * (I + 1):SB * (I + 1) + 1, :] for I in range(nsb - 1)] + [g[CHUNK - 1:CHUNK, :]]
            far = range(nsb - 1)
        rvec = jnp.concatenate([jnp.broadcast_to(r, (SB, HEAD_DIM)) for r in refs], axis=0)
        qt = qc * jnp.exp(g - rvec)
        off = None
        for I in far:
            seen = (jrow < SB * I) if d == 0 else (jrow >= SB * (I + 1))
            kt = (kc * jnp.exp(jnp.where(seen, refs[I] - g, MASK_NEG))).astype(BF16)
            qi = jnp.where((jrow // SB) == I, qt, 0.0).astype(BF16)
            part = lax.dot_general(kt, qi, _NT, preferred_element_type=F32)
            off = part if off is None else off + part
        blocks = []
        for I in range(nsb):
            rs = slice(SB * I, SB * (I + 1))
            g_i, k_i = g[rs, :], kc[rs, :]
            acc = jnp.zeros((SB, CHUNK), F32)
            for r in range(SB):
                i = SB * I + r
                keep = (jsub <= r) if d == 0 else (jsub >= r)
                rel = jnp.exp(jnp.where(keep, g[i:i + 1, :] - g_i, MASK_NEG))
                colv = jnp.sum(rel * k_i * qc[i:i + 1, :], axis=-1, keepdims=True)
                acc = jnp.where(alane == i, colv, acc)
            blocks.append(acc)
        return jnp.concatenate(blocks, axis=0) + off

    def step_d(d, n, st):
        c = _chunk_of_step(n, d, lay)
        sl = pl.ds(pl.multiple_of(c * CHUNK, CHUNK), CHUNK)
        qc, kc, vc, g = q_ref[sl, :], kk[d, sl, :], i_ref[sl, :], gc[d, sl, :]
        glast = g[CHUNK - 1:CHUNK, :] if d == 0 else g[0:1, :]
        a_t = intra_t(d, qc, kc, g)
        vb = vc.astype(BF16)
        o = (lax.dot_general((qc * jnp.exp(g)).astype(BF16), st.astype(BF16), _NT, preferred_element_type=F32)
             + lax.dot_general(a_t.astype(BF16), vb, _TN, preferred_element_type=F32))
        o_s[d, sl, :] = o
        kg = (kc * jnp.exp(glast - g)).astype(BF16)
        return st * jnp.exp(glast) + lax.dot_general(vb, kg, _TN, preferred_element_type=F32)

    def both(n, carry):
        return (step_d(0, n, carry[0]), step_d(1, n, carry[1]))

    zero = jnp.zeros((HEAD_DIM, HEAD_DIM), F32)
    lax.fori_loop(0, nc, both, (zero, zero), unroll=4 if nc % 4 == 0 else 1)
    o_ref[...] = _gated_head_norm(o_s[0] + o_s[1], z_ref[...], g_ref[...])


def hgrn2(px, lb_logits, norm_g, lay, W, col0, layer):
    T = px.shape[0]
    H = W // HEAD_DIM
    S = lay.S
    depth = lb_logits.shape[0]

    def col(o):
        return pl.BlockSpec((S, HEAD_DIM), lambda b, h: (b, col0 + o * H + h))

    def lbs(d):
        return pl.BlockSpec((depth, HEAD_DIM), lambda b, h: (0, d * H + h))

    sh = (S, HEAD_DIM)
    return pl.pallas_call(
        functools.partial(_hgrn_body, lay=lay, layer=layer),
        out_shape=jax.ShapeDtypeStruct((T, W), BF16),
        grid=(lay.B, H),
        in_specs=[col(0), col(1), col(2), col(3), col(4), lbs(0), lbs(1),
                  pl.BlockSpec((1, HEAD_DIM), lambda b, h: (0, 0))],
        out_specs=pl.BlockSpec((S, HEAD_DIM), lambda b, h: (b, h)),
        scratch_shapes=[pltpu.VMEM((2,) + sh, F32), pltpu.VMEM((2,) + sh, F32), pltpu.VMEM((2,) + sh, F32)],
        compiler_params=_params("parallel", "arbitrary"),
        name="hgrn2",
    )(px, px, px, px, px, lb_logits, lb_logits, norm_g.reshape(1, HEAD_DIM))


def _split_bf16(x):
    hi = x.astype(BF16)
    return hi, (x - hi.astype(F32)).astype(BF16)


def _dot3(a_hi, a_lo, b_hi, b_lo):
    return (jnp.dot(a_hi, b_hi, preferred_element_type=F32)
            + jnp.dot(a_hi, b_lo, preferred_element_type=F32)
            + jnp.dot(a_lo, b_hi, preferred_element_type=F32))


def _hy_filter_body(z_ref, w1_ref, b1_ref, w2_ref, b2_ref, w3_ref, fr_ref, dl_ref, o_ref, *, L):
    i = pl.program_id(0)
    z = z_ref[...]
    tr = z.shape[0]
    W = o_ref.shape[1]
    h = jnp.sin(fr_ref[0:1, :] * (jnp.dot(z, w1_ref[...], precision=HIGHEST, preferred_element_type=F32) + b1_ref[...]))
    h = jnp.sin(fr_ref[1:2, :] * (jnp.dot(h, w2_ref[...], precision=HIGHEST, preferred_element_type=F32) + b2_ref[...]))
    h = jnp.dot(h, w3_ref[...], precision=HIGHEST, preferred_element_type=F32)
    h = h * jnp.exp(-z[:, 0:1] * dl_ref[...])
    row = lax.broadcasted_iota(jnp.int32, (tr, W), 0) + i * tr
    o_ref[...] = jnp.where(row < L, h[:, :W], jnp.where(row == L, 0.0, h[:, W:]))


def hyena_filter(z2, w1p, b1, w2, b2, w3, freq, deltas2, L):
    n = z2.shape[0]
    W = w3.shape[1] // 2
    O = w2.shape[0]
    tr = _tile(n, 512)
    full = lambda a: pl.BlockSpec(a.shape, lambda i: (0,) * a.ndim)
    args = (w1p, b1.reshape(1, O), w2, b2.reshape(1, O), w3, freq, deltas2)
    return pl.pallas_call(
        functools.partial(_hy_filter_body, L=L),
        out_shape=jax.ShapeDtypeStruct((n, W), F32),
        grid=(n // tr,),
        in_specs=[pl.BlockSpec((tr, LANES), lambda i: (i, 0))] + [full(a) for a in args],
        out_specs=pl.BlockSpec((tr, W), lambda i: (i, 0)),
        compiler_params=_params("parallel"),
        name="hyena_filter",
    )(z2, *args)


def _hy_prep_body(v_ref, x0_ref, x1_ref, cv_ref, c0_ref, c1_ref, bv_ref, b0_ref, b1_ref,
                  vgx_ref, vge_ref, x0u_ref, *, lay):
    shape = v_ref.shape
    prev_ok, next_ok = _conv_masks(shape, lay)
    v = _centred_conv(v_ref[...], cv_ref[...], prev_ok, next_ok) + bv_ref[...]
    x0 = _centred_conv(x0_ref[...], c0_ref[...], prev_ok, next_ok) + b0_ref[...]
    x1 = _centred_conv(x1_ref[...], c1_ref[...], prev_ok, next_ok) + b1_ref[...]
    vg = v * x1
    vge_ref[...] = vg[:lay.Lc, :]
    vgx_ref[...] = vg[lay.Lc:, :]
    x0u_ref[...] = x0


def hyena_prep(px, conv_w, conv_b, lay, W, col0):
    T = px.shape[0]
    H = W // HEAD_DIM
    S, L, Lc, B = lay.S, lay.L, lay.Lc, lay.B

    def col(o):
        return pl.BlockSpec((S, HEAD_DIM), lambda b, j: (b, col0 + o * H + j))

    def cw(o, rows):
        return pl.BlockSpec((rows, HEAD_DIM), lambda b, j: (0, o * H + j))

    return pl.pallas_call(
        functools.partial(_hy_prep_body, lay=lay),
        out_shape=(jax.ShapeDtypeStruct((B * L, W), F32), jax.ShapeDtypeStruct((B * Lc, W), F32),
                   jax.ShapeDtypeStruct((T, W), F32)),
        grid=(B, H),
        in_specs=[col(0), col(1), col(2), cw(0, 3), cw(1, 3), cw(2, 3), cw(0, 1), cw(1, 1), cw(2, 1)],
        out_specs=(pl.BlockSpec((L, HEAD_DIM), lambda b, j: (b, j)),
                   pl.BlockSpec((Lc, HEAD_DIM), lambda b, j: (b, j)),
                   pl.BlockSpec((S, HEAD_DIM), lambda b, j: (b, j))),
        compiler_params=_params("parallel", "parallel"),
        name="hyena_prep",
    )(px, px, px, conv_w, conv_w, conv_w, conv_b, conv_b, conv_b)


def _dft_tables(L):
    n = 2 * L
    FT = min(256, L)
    sub = 64
    idx = jnp.arange(n, dtype=jnp.int32)
    r = idx % (2 * FT)
    is_im = r >= FT
    f = (idx // (2 * FT)) * FT + jnp.where(is_im, r - FT, r)
    nyq = is_im & (f == 0)
    f = jnp.where(nyq, L, f)
    step = jnp.arange(sub, dtype=jnp.int32)
    ang_hi = (2.0 * math.pi / n) * ((f[:, None] * (step * sub)[None, :]) % n).astype(F32)
    ang_lo = (2.0 * math.pi / n) * ((f[:, None] * step[None, :]) % n).astype(F32)
    ch, sh, cl, sl = jnp.cos(ang_hi), jnp.sin(ang_hi), jnp.cos(ang_lo), jnp.sin(ang_lo)
    wgt = jnp.where((f == 0) | (f == L), 1.0 / n, 2.0 / n)
    minus_sin = is_im & ~nyq

    def table(hi_steps, transpose):
        if transpose:
            c_h, s_h, c_l, s_l = (t.T for t in (ch, sh, cl, sl))
            cos = c_h[:hi_steps, None, :] * c_l[None, :, :] - s_h[:hi_steps, None, :] * s_l[None, :, :]
            sin = s_h[:hi_steps, None, :] * c_l[None, :, :] + c_h[:hi_steps, None, :] * s_l[None, :, :]
            out = jnp.where(minus_sin[None, None, :], -sin, cos) * wgt[None, None, :]
            return out.reshape(hi_steps * sub, n)
        cos = ch[:, :hi_steps, None] * cl[:, None, :] - sh[:, :hi_steps, None] * sl[:, None, :]
        sin = sh[:, :hi_steps, None] * cl[:, None, :] + ch[:, :hi_steps, None] * sl[:, None, :]
        return jnp.where(minus_sin[:, None, None], -sin, cos).reshape(n, hi_steps * sub)

    return table(n // sub, False), table(L // sub, False), table(L // sub, True), FT


def _hy_fwd_body(a_ref, u_ref, kf_ref, y_ref, *, FT):
    t = pl.program_id(2)
    x = jnp.dot(a_ref[...], u_ref[...].astype(BF16), preferred_element_type=F32)
    kf = kf_ref[...]
    xr, xi, kr, ki = x[:FT], x[FT:], kf[:FT], kf[FT:]
    row = lax.broadcasted_iota(jnp.int32, xr.shape, 0)
    packed = (row + t) == 0
    yr = jnp.where(packed, xr * kr, xr * kr - xi * ki)
    yi = jnp.where(packed, xi * ki, xr * ki + xi * kr)
    y_ref[...] = jnp.concatenate([yr, yi], axis=0).astype(BF16)


def _hy_spec_body(ah_ref, al_ref, u_ref, o_ref):
    uh, ul = _split_bf16(u_ref[...])
    o_ref[...] = _dot3(ah_ref[...], al_ref[...], uh, ul)


def hyena_spectrum(fh, fl, kern, FT):
    n, W = kern.shape
    tn = _tile(W, 256)
    return pl.pallas_call(
        _hy_spec_body,
        out_shape=jax.ShapeDtypeStruct((n, W), F32),
        grid=(W // tn, n // (2 * FT)),
        in_specs=[pl.BlockSpec((2 * FT, n), lambda j, t: (t, 0)),
                  pl.BlockSpec((2 * FT, n), lambda j, t: (t, 0)),
                  pl.BlockSpec((n, tn), lambda j, t: (0, j))],
        out_specs=pl.BlockSpec((2 * FT, tn), lambda j, t: (t, j)),
        compiler_params=_params("parallel", "arbitrary"),
        name="hyena_spectrum",
    )(fh, fl, kern)


def hyena_fwd(f_b, vg, kf, B, Lq, FT):
    W = vg.shape[1]
    n = 2 * Lq
    nt = n // (2 * FT)
    tn = _tile(W, 1024)
    return pl.pallas_call(
        functools.partial(_hy_fwd_body, FT=FT),
        out_shape=jax.ShapeDtypeStruct((B * n, W), BF16),
        grid=(B, W // tn, nt),
        in_specs=[pl.BlockSpec((2 * FT, Lq), lambda b, j, t: (t, 0)),
                  pl.BlockSpec((Lq, tn), lambda b, j, t: (b, j)),
                  pl.BlockSpec((2 * FT, tn), lambda b, j, t: (t, j))],
        out_specs=pl.BlockSpec((2 * FT, tn), lambda b, j, t: (b * nt + t, j)),
        compiler_params=_params("parallel", "parallel", "arbitrary"),
        name="hyena_fwd",
    )(f_b, vg, kf)


def _hy_inv_body(g_ref, y_ref, vg_ref, x0_ref, bias_ref, o_ref):
    y = jnp.dot(g_ref[...], y_ref[...], preferred_element_type=F32)
    o_ref[...] = (x0_ref[...] * (y + vg_ref[...] * bias_ref[...])).astype(BF16)


def hyena_inv(g_b, y_b, vg, x0u, bias, lay, Lq, row0):
    W = vg.shape[1]
    n = 2 * Lq
    B = lay.B
    tt = lay.gr
    ntt = Lq // tt
    tn = _tile(W, 1024)
    g0 = row0 // tt
    return pl.pallas_call(
        _hy_inv_body,
        out_shape=jax.ShapeDtypeStruct((B * Lq, W), BF16),
        grid=(B, W // tn, ntt),
        in_specs=[pl.BlockSpec((tt, n), lambda b, j, i: (i, 0)),
                  pl.BlockSpec((n, tn), lambda b, j, i: (b, j)),
                  pl.BlockSpec((tt, tn), lambda b, j, i: (b * ntt + i, j)),
                  pl.BlockSpec((tt, tn), lambda b, j, i: (b * lay.NG + g0 + i, j)),
                  pl.BlockSpec((1, tn), lambda b, j, i: (0, j))],
        out_specs=pl.BlockSpec((tt, tn), lambda b, j, i: (b * ntt + i, j)),
        compiler_params=_params("parallel", "parallel", "arbitrary"),
        name="hyena_inv",
    )(g_b, y_b, vg, x0u, bias.reshape(1, W))


def _hyena_features(L):
    bands = (HY_EMB - 1) // 2
    t = jnp.linspace(0.0, 1.0, L, dtype=F32)[:, None]
    wpos = 2.0 * math.pi * jnp.arange(L, dtype=F32)[:, None] / L
    fb = jnp.linspace(1e-4, bands - 1, bands, dtype=F32)[None]
    z = jnp.concatenate([t, jnp.cos(fb * wpos), -jnp.sin(fb * wpos)], axis=-1)
    z2 = jnp.concatenate([z, z[::-1]], axis=0)
    return jnp.pad(z2, ((0, 0), (0, LANES - HY_EMB)))


def hyena_stream(vg, x0u, bias, wts, lay, Lq, row0):
    w1p, b1, w2, b2, w3, freq, deltas2 = wts
    fwd_full, fwd_half, inv, FT = _dft_tables(Lq)
    kh, kl = _split_bf16(fwd_full)
    kern = hyena_filter(_hyena_features(Lq), w1p, b1, w2, b2, w3, freq, deltas2, Lq)
    kf = hyena_spectrum(kh, kl, kern, FT)
    y_b = hyena_fwd(fwd_half.astype(BF16), vg, kf, lay.B, Lq, FT)
    return hyena_inv(inv.astype(BF16), y_b, vg, x0u, bias, lay, Lq, row0)


def _rope_tables(lay):
    n_freq = DIFF_DIM // 4
    inv = ROPE_THETA ** (-jnp.arange(n_freq, dtype=F32) / n_freq)
    rows = lay.L // GRID_W
    r = jnp.repeat(jnp.arange(rows, dtype=F32), GRID_W)
    col = jnp.tile(jnp.arange(GRID_W, dtype=F32), rows)
    ang = jnp.concatenate([r[:, None] * inv, col[:, None] * inv], axis=-1)
    cos, sin = jnp.cos(ang), jnp.sin(ang)
    cos = jnp.concatenate([jnp.ones((lay.Lc, DIFF_DIM // 2), F32), cos], axis=0)
    sin = jnp.concatenate([jnp.zeros((lay.Lc, DIFF_DIM // 2), F32), sin], axis=0)
    cos_t = jnp.tile(cos, (1, 4))
    sin_t = jnp.tile(jnp.concatenate([-sin, sin], axis=-1), (1, 2))
    return cos_t, sin_t


def _main_cols(W, H):
    small0 = 7 * W
    small1 = small0 + 4 * H
    return small0, small1


def mixers(px1, px2, ps, lay, W, layer, need_ctx, p):
    H = W // HEAD_DIM
    lam_init = 0.8 - 0.6 * math.exp(-0.3 * layer)
    q_rot, k_rot = attn_prep(px1, p["cos_t"], p["sin_t"], p["qk_g2"], lay, W)
    att = attention(q_rot, k_rot, px1, p["attn_lambda"], p["attn_subln_g"], lay, W, lam_init)
    gates = gdn_gates(ps, p["alog"], p["dtb"], lay, H)
    gdn = gated_deltanet(px1, gates, p["gdn_conv_w"], p["gdn_norm_g"], lay, W, 3 * H)
    hg = hgrn2(px2, p["hg_lb_logits"], p["hg_norm_g"], lay, W, 0, layer)
    vgx, vge, x0u = hyena_prep(px2, p["hy_conv_w"], p["hy_conv_b"], lay, W, 5 * H)
    hy_x = hyena_stream(vgx, x0u, p["hy_bias"], p["hy_wts"], lay, lay.L, lay.Lc)
    if need_ctx:
        hy_e = hyena_stream(vge, x0u, p["hy_bias"], p["hy_wts"], lay, lay.Lc, 0)
    else:
        hy_e = jnp.zeros((lay.B, lay.Lc, W), BF16)
    hy = jnp.concatenate([hy_e.reshape(lay.B, lay.Lc, W), hy_x.reshape(lay.B, lay.L, W)], axis=1)
    return att, gdn, hg, hy.reshape(lay.T, W)


def _layer_params(l, lay, W, attn_qk_g, attn_lambda, attn_subln_g, gdn_conv_w, gdn_a_log, gdn_dt_bias, gdn_norm_g,
                  hg_lb_logits, hg_norm_g, hy_conv_w, hy_conv_b, hy_w1, hy_b1, hy_w2, hy_b2, hy_w3, hy_freq,
                  hy_bias, cos_t, sin_t):
    H = W // HEAD_DIM
    pad16 = lambda a: jnp.pad(a.astype(F32).reshape(1, 2 * H), ((0, 0), (0, LANES - 2 * H)))
    deltas = jnp.abs(jnp.linspace(math.log(HY_TARGET) / HY_SLOW_DECAY, math.log(HY_TARGET) / HY_FAST_DECAY, W, dtype=F32))
    return dict(
        cos_t=cos_t, sin_t=sin_t,
        qk_g2=jnp.tile(attn_qk_g[l].astype(F32), (1, 2)),
        attn_lambda=attn_lambda[l].astype(F32), attn_subln_g=attn_subln_g[l].astype(F32),
        gdn_conv_w=gdn_conv_w[l], alog=pad16(gdn_a_log[l]), dtb=pad16(gdn_dt_bias[l]), gdn_norm_g=gdn_norm_g[l],
        hg_lb_logits=hg_lb_logits.astype(F32).reshape(hg_lb_logits.shape[0], 2 * W), hg_norm_g=hg_norm_g[l],
        hy_conv_w=hy_conv_w[l], hy_conv_b=hy_conv_b[l].reshape(1, 3 * W), hy_bias=hy_bias[l].astype(F32),
        hy_wts=(jnp.pad(hy_w1[l].astype(F32), ((0, LANES - HY_EMB), (0, 0))), hy_b1[l].astype(F32),
                hy_w2[l].astype(F32), hy_b2[l].astype(F32), hy_w3[l].astype(F32), hy_freq[l].astype(F32),
                jnp.tile(deltas, 2).reshape(1, 2 * W)),
    )


def kernel(x, c, ctx, c_ctx, norm_g, w_mod, b_mod, ffn_w_in, ffn_w_out, w_in, attn_qk_g, attn_lambda, attn_subln_g, gdn_conv_w, gdn_a_log, gdn_dt_bias, gdn_norm_g, hg_lb_logits, hg_norm_g, hy_conv_w, hy_conv_b, hy_w1, hy_b1, hy_w2, hy_b2, hy_w3, hy_freq, hy_bias, w_gate, w_up, w_out):
    B, L, D = x.shape
    Lc = ctx.shape[1]
    depth = w_mod.shape[0]
    W = D // N_BRANCH
    H = W // HEAD_DIM
    lay = Layout(B, L, Lc)
    h = jnp.concatenate([ctx, x], axis=1).reshape(lay.T, D)
    G = -(-(B + 1) // 8) * 8
    c_all = jnp.concatenate([c, c_ctx[None], jnp.zeros((G - B - 1, D), F32)], axis=0)
    cos_t, sin_t = _rope_tables(lay)
    small0, small1 = _main_cols(W, H)
    w_gate_b, w_up_b = w_gate.astype(BF16), w_up.astype(BF16)
    for l in range(depth):
        last = l == depth - 1
        p = _layer_params(l, lay, W, attn_qk_g, attn_lambda, attn_subln_g, gdn_conv_w, gdn_a_log, gdn_dt_bias,
                          gdn_norm_g, hg_lb_logits, hg_norm_g, hy_conv_w, hy_conv_b, hy_w1, hy_b1, hy_w2, hy_b2,
                          hy_w3, hy_freq, hy_bias, cos_t, sin_t)
        mod = mod_table(c_all, w_mod, b_mod, l).reshape(G, MOD_CHUNKS, D)
        act = mm_swiglu(norm_mod(h, norm_g[l, 0], mod, lay, 0), ffn_w_in, (l, 0))
        h = mm_resid(act, ffn_w_out, (l, 0), h, mod, lay, 2, 0.5)
        xn = norm_mod(h, norm_g[l, 1], mod, lay, 1)
        cols_t = lambda lo, hi: w_in[l, :, lo:hi].T
        px1 = mm_plain(xn, cols_t(0, small0))
        px2 = mm_plain(xn, cols_t(small1, w_in.shape[-1]))
        ps = mm_plain(xn, jnp.pad(cols_t(small0, small1), ((0, LANES - (small1 - small0)), (0, 0))), tn_pref=LANES)
        branches = mixers(px1, px2, ps, lay, W, l, not last, p)
        if last:
            latent = lambda a: a.reshape(B, lay.S, a.shape[-1])[:, Lc:, :].reshape(B * L, a.shape[-1])
            xn, h, branches = latent(xn), latent(h), tuple(latent(a) for a in branches)
            lay = Layout(B, L, 0, gr=lay.gr)
        acc = merge_branches(xn, branches, w_gate_b, w_up_b, l)
        h = mm_resid(acc, w_out, (l,), h, mod, lay, 5, 1.0)
        act = mm_swiglu(norm_mod(h, norm_g[l, 2], mod, lay, 2), ffn_w_in, (l, 1))
        h = mm_resid(act, ffn_w_out, (l, 1), h, mod, lay, 8, 0.5)
    return h.reshape(B, L, D)
```

```python
import functools
import math

import jax
import jax.numpy as jnp
from jax import lax
from jax.experimental import pallas as pl
from jax.experimental.pallas import tpu as pltpu

F32 = jnp.float32
BF16 = jnp.bfloat16

GRID_W = 64
N_BRANCH = 4
HEAD_DIM = 128
DIFF_DIM = HEAD_DIM // 2
MOD_CHUNKS = 9
ROPE_THETA = 10000.0
CHUNK = 64
HY_EMB = 33
HY_FAST_DECAY = 0.3
HY_SLOW_DECAY = 1.5
HY_TARGET = 1e-2
EPS = 1e-6
MASK_NEG = -1e30
LB_FLOOR = 1e-20

LANES = 128
V7X_VMEM_LIMIT = 56 * 1024 * 1024
HIGHEST = lax.Precision.HIGHEST

_NT = (((1,), (1,)), ((), ()))
_TN = (((0,), (0,)), ((), ()))


def _params(*sem):
    return pltpu.CompilerParams(dimension_semantics=sem, vmem_limit_bytes=V7X_VMEM_LIMIT)


def _tile(n, pref):
    t = min(n, pref)
    while n % t:
        t //= 2
    return t


def _wspec(at, K, tn, col):
    return pl.BlockSpec((None,) * len(at) + (K, tn), lambda *ids: tuple(at) + (0, col(*ids)))


def _sigmoid(x):
    return 1.0 / (1.0 + jnp.exp(-x))


def _silu(x):
    return x * _sigmoid(x)


def _softplus(x):
    return jnp.maximum(x, 0.0) + jnp.log1p(jnp.exp(-jnp.abs(x)))


class Layout:
    def __init__(self, B, L, Lc, gr=None):
        self.B, self.L, self.Lc = B, L, Lc
        self.S = L + Lc
        self.T = B * self.S
        self.gr = gr or math.gcd(L, Lc)
        self.NG = self.S // self.gr
        self.NGc = Lc // self.gr


def _mod_row(mod_ref, lay, r, k):
    b = r // lay.NG
    g = jnp.where(r - b * lay.NG < lay.NGc, lay.B, b)
    return mod_ref[pl.ds(g, 1), k:k + 1, :][0]


def _mod_body(c_ref, w_ref, b_ref, o_ref):
    a = _silu(c_ref[...]).astype(BF16)
    o_ref[...] = jnp.dot(a, w_ref[...].astype(BF16), preferred_element_type=F32) + b_ref[...]


def mod_table(c16, w_mod, b_mod, l):
    R, D = c16.shape
    N = w_mod.shape[-1]
    tn = _tile(N, 512)
    return pl.pallas_call(
        _mod_body,
        out_shape=jax.ShapeDtypeStruct((R, N), F32),
        grid=(N // tn,),
        in_specs=[pl.BlockSpec((R, D), lambda j: (0, 0)),
                  _wspec((l,), D, tn, lambda j: j),
                  _wspec((l,), 1, tn, lambda j: j)],
        out_specs=pl.BlockSpec((R, tn), lambda j: (0, j)),
        compiler_params=_params("parallel"),
        name="mod_table",
    )(c16, w_mod, b_mod.reshape(b_mod.shape[0], 1, N))


def _norm_mod_body(x_ref, g_ref, mod_ref, o_ref, *, lay, k):
    i = pl.program_id(0)
    gr = lay.gr
    n_sub = x_ref.shape[0] // gr
    for s in range(n_sub):
        x = x_ref[s * gr:(s + 1) * gr, :]
        y = x * lax.rsqrt(jnp.mean(x * x, axis=-1, keepdims=True) + EPS) * g_ref[...]
        r = i * n_sub + s
        shift = _mod_row(mod_ref, lay, r, 3 * k)
        scale = _mod_row(mod_ref, lay, r, 3 * k + 1)
        o_ref[s * gr:(s + 1) * gr, :] = (y * (1.0 + scale) + shift).astype(BF16)


def norm_mod(h, g, mod, lay, k):
    T, D = h.shape
    tm = 2 * lay.gr if T % (2 * lay.gr) == 0 else lay.gr
    G = mod.shape[0]
    return pl.pallas_call(
        functools.partial(_norm_mod_body, lay=lay, k=k),
        out_shape=jax.ShapeDtypeStruct((T, D), BF16),
        grid=(T // tm,),
        in_specs=[pl.BlockSpec((tm, D), lambda i: (i, 0)),
                  pl.BlockSpec((1, D), lambda i: (0, 0)),
                  pl.BlockSpec((G, MOD_CHUNKS, D), lambda i: (0, 0, 0))],
        out_specs=pl.BlockSpec((tm, D), lambda i: (i, 0)),
        compiler_params=_params("parallel"),
        name="norm_mod",
    )(h, g.reshape(1, D), mod)


def _mm_swiglu_body(a_ref, wg_ref, wu_ref, o_ref, wgb, wub):
    @pl.when(pl.program_id(1) == 0)
    def _():
        wgb[...] = wg_ref[...].astype(BF16)
        wub[...] = wu_ref[...].astype(BF16)

    a = a_ref[...]
    g = jnp.dot(a, wgb[...], preferred_element_type=F32)
    u = jnp.dot(a, wub[...], preferred_element_type=F32)
    o_ref[...] = (_silu(g) * u).astype(BF16)


def mm_swiglu(a, w, at):
    T, K = a.shape
    F = w.shape[-1] // 2
    tm, tn = _tile(T, 1024), _tile(F, 256)
    nj = F // tn
    return pl.pallas_call(
        _mm_swiglu_body,
        out_shape=jax.ShapeDtypeStruct((T, F), BF16),
        grid=(nj, T // tm),
        in_specs=[pl.BlockSpec((tm, K), lambda j, i: (i, 0)),
                  _wspec(at, K, tn, lambda j, i: j),
                  _wspec(at, K, tn, lambda j, i: j + nj)],
        out_specs=pl.BlockSpec((tm, tn), lambda j, i: (i, j)),
        scratch_shapes=[pltpu.VMEM((K, tn), BF16), pltpu.VMEM((K, tn), BF16)],
        compiler_params=_params("parallel", "arbitrary"),
        name="mm_swiglu",
    )(a, w, w)


def _mm_resid_body(a_ref, w_ref, h_ref, mod_ref, o_ref, wb, *, lay, k, scale):
    i = pl.program_id(1)

    @pl.when(i == 0)
    def _():
        wb[...] = w_ref[...].astype(BF16)

    y = jnp.dot(a_ref[...], wb[...], preferred_element_type=F32)
    gr = lay.gr
    n_sub = a_ref.shape[0] // gr
    for s in range(n_sub):
        gate = _mod_row(mod_ref, lay, i * n_sub + s, k)
        sl = slice(s * gr, (s + 1) * gr)
        o_ref[sl, :] = h_ref[sl, :] + (scale * gate) * y[sl, :]


def mm_resid(a, w, at, h, mod, lay, k, scale):
    T, K = a.shape
    N = w.shape[-1]
    G = mod.shape[0]
    def vmem_bytes(tm, tn):
        return 2 * tm * K * 2 + 2 * K * tn * 4 + K * tn * 2 + 4 * tm * tn * 4

    tm, tn = max(_tile(T, 1024), lay.gr), _tile(N, 512)
    if vmem_bytes(tm, tn) > (V7X_VMEM_LIMIT * 7) // 8:
        tm = max(_tile(T, 512), lay.gr)
    return pl.pallas_call(
        functools.partial(_mm_resid_body, lay=lay, k=k, scale=scale),
        out_shape=jax.ShapeDtypeStruct((T, N), F32),
        grid=(N // tn, T // tm),
        in_specs=[pl.BlockSpec((tm, K), lambda j, i: (i, 0)),
                  _wspec(at, K, tn, lambda j, i: j),
                  pl.BlockSpec((tm, tn), lambda j, i: (i, j)),
                  pl.BlockSpec((G, MOD_CHUNKS, tn), lambda j, i: (0, 0, j))],
        out_specs=pl.BlockSpec((tm, tn), lambda j, i: (i, j)),
        scratch_shapes=[pltpu.VMEM((K, tn), BF16)],
        compiler_params=_params("parallel", "arbitrary"),
        name="mm_resid",
    )(a, w, h, mod)


def _mm_plain_body(a_ref, wt_ref, o_ref, wb):
    @pl.when(pl.program_id(1) == 0)
    def _():
        wb[...] = wt_ref[...].T.astype(BF16)

    o_ref[...] = jnp.dot(a_ref[...], wb[...], preferred_element_type=F32)


def mm_plain(a, w_t, tn_pref=512):
    T, K = a.shape
    N = w_t.shape[0]
    tm, tn = _tile(T, 1024), _tile(N, tn_pref)
    return pl.pallas_call(
        _mm_plain_body,
        out_shape=jax.ShapeDtypeStruct((T, N), F32),
        grid=(N // tn, T // tm),
        in_specs=[pl.BlockSpec((tm, K), lambda j, i: (i, 0)),
                  pl.BlockSpec((tn, K), lambda j, i: (j, 0))],
        out_specs=pl.BlockSpec((tm, tn), lambda j, i: (i, j)),
        scratch_shapes=[pltpu.VMEM((K, tn), BF16)],
        compiler_params=_params("parallel", "arbitrary"),
        name="mm_plain",
    )(a, w_t)


def _merge_body(xn_ref, o0_ref, o1_ref, o2_ref, o3_ref, wg_ref, wu_ref, out_ref):
    xn = xn_ref[...]
    acc = None
    for br, o_ref in enumerate((o0_ref, o1_ref, o2_ref, o3_ref)):
        g = jnp.dot(xn, wg_ref[br], preferred_element_type=F32)
        u = jnp.dot(o_ref[...], wu_ref[br], preferred_element_type=F32)
        term = _sigmoid(g) * u
        acc = term if acc is None else acc + term
    out_ref[...] = acc.astype(BF16)


def merge_branches(xn, branches, w_gate_b, w_up_b, l):
    T, D = xn.shape
    W = branches[0].shape[1]
    tm, tn = _tile(T, 512), _tile(D, 256)
    bspec = pl.BlockSpec((tm, W), lambda i, j: (i, 0))
    return pl.pallas_call(
        _merge_body,
        out_shape=jax.ShapeDtypeStruct((T, D), BF16),
        grid=(T // tm, D // tn),
        in_specs=[pl.BlockSpec((tm, D), lambda i, j: (i, 0)), bspec, bspec, bspec, bspec,
                  pl.BlockSpec((None, N_BRANCH, D, tn), lambda i, j: (l, 0, 0, j)),
                  pl.BlockSpec((None, N_BRANCH, W, tn), lambda i, j: (l, 0, 0, j))],
        out_specs=pl.BlockSpec((tm, tn), lambda i, j: (i, j)),
        compiler_params=_params("parallel", "arbitrary"),
        name="merge_branches",
    )(xn, *branches, w_gate_b, w_up_b)


def _attn_prep_body(q_ref, k_ref, cos_ref, sin_ref, g_ref, qo_ref, ko_ref):
    cos = cos_ref[...]
    sin = sin_ref[...]
    lane = lax.broadcasted_iota(jnp.int32, cos.shape, 1)
    lo = lane < DIFF_DIM
    first = (lane % DIFF_DIM) < DIFF_DIM // 2
    n_heads = q_ref.shape[1] // HEAD_DIM
    for src, dst, gi, sc in ((q_ref, qo_ref, 0, DIFF_DIM ** -0.5), (k_ref, ko_ref, 1, 1.0)):
        g = g_ref[gi:gi + 1, :]
        for h in range(n_heads):
            cs = slice(h * HEAD_DIM, (h + 1) * HEAD_DIM)
            t = src[:, cs]
            ss = t * t
            s_lo = jnp.sum(jnp.where(lo, ss, 0.0), axis=-1, keepdims=True)
            s_hi = jnp.sum(jnp.where(lo, 0.0, ss), axis=-1, keepdims=True)
            inv = jnp.where(lo, lax.rsqrt(s_lo / DIFF_DIM + EPS), lax.rsqrt(s_hi / DIFF_DIM + EPS))
            y = t * inv * g
            partner = jnp.where(first, pltpu.roll(y, HEAD_DIM - DIFF_DIM // 2, 1), pltpu.roll(y, DIFF_DIM // 2, 1))
            dst[:, cs] = ((y * cos + partner * sin) * sc).astype(BF16)


def attn_prep(px, cos_t, sin_t, qk_g2, lay, W):
    T = px.shape[0]
    tr = lay.gr
    return pl.pallas_call(
        _attn_prep_body,
        out_shape=(jax.ShapeDtypeStruct((T, W), BF16), jax.ShapeDtypeStruct((T, W), BF16)),
        grid=(T // tr,),
        in_specs=[pl.BlockSpec((tr, W), lambda i: (i, 0)),
                  pl.BlockSpec((tr, W), lambda i: (i, 1)),
                  pl.BlockSpec((tr, LANES), lambda i: (i % lay.NG, 0)),
                  pl.BlockSpec((tr, LANES), lambda i: (i % lay.NG, 0)),
                  pl.BlockSpec((2, LANES), lambda i: (0, 0))],
        out_specs=(pl.BlockSpec((tr, W), lambda i: (i, 0)), pl.BlockSpec((tr, W), lambda i: (i, 0))),
        compiler_params=_params("parallel"),
        name="attn_prep",
    )(px, px, cos_t, sin_t, qk_g2)


def _attn_body(q_ref, k_ref, v_ref, lam_ref, g_ref, o_ref, vb, *, lay, lam_init):
    qi = pl.program_id(2)

    @pl.when(qi == 0)
    def _():
        vb[...] = v_ref[...].astype(BF16)

    lp = lam_ref[...]
    lam = (jnp.exp(jnp.sum(lp[0:1] * lp[1:2], axis=-1, keepdims=True))
           - jnp.exp(jnp.sum(lp[2:3] * lp[3:4], axis=-1, keepdims=True)) + lam_init)

    def run(nk):
        q = q_ref[...]
        k = k_ref[0:nk, :]
        v = vb[0:nk, :]
        lane = lax.broadcasted_iota(jnp.int32, q.shape, 1)
        zero = jnp.zeros_like(q)
        outs = []
        for qq in (jnp.where(lane < DIFF_DIM, q, zero), jnp.where(lane < DIFF_DIM, zero, q)):
            s = lax.dot_general(qq, k, _NT, preferred_element_type=F32)
            p = jnp.exp(s - jnp.max(s, axis=-1, keepdims=True))
            l = jnp.sum(p, axis=-1, keepdims=True)
            outs.append(jnp.dot(p.astype(BF16), v, preferred_element_type=F32) / l)
        o = outs[0] - lam * outs[1]
        y = o * lax.rsqrt(jnp.mean(o * o, axis=-1, keepdims=True) + EPS) * g_ref[...]
        o_ref[...] = (y * (1.0 - lam_init)).astype(BF16)

    @pl.when(qi < lay.NGc)
    def _():
        run(lay.Lc)

    @pl.when(qi >= lay.NGc)
    def _():
        run(lay.S)


def attention(q_rot, k_rot, px, lam_p, subln_g, lay, W, lam_init):
    T = px.shape[0]
    H = W // HEAD_DIM
    tq = lay.gr
    S = lay.S
    vb = W // HEAD_DIM * 2
    return pl.pallas_call(
        functools.partial(_attn_body, lay=lay, lam_init=lam_init),
        out_shape=jax.ShapeDtypeStruct((T, W), BF16),
        grid=(lay.B, H, lay.NG),
        in_specs=[pl.BlockSpec((tq, HEAD_DIM), lambda b, h, i: (b * lay.NG + i, h)),
                  pl.BlockSpec((S, HEAD_DIM), lambda b, h, i: (b, h)),
                  pl.BlockSpec((S, HEAD_DIM), lambda b, h, i: (b, vb + h)),
                  pl.BlockSpec((4, DIFF_DIM), lambda b, h, i: (0, 0)),
                  pl.BlockSpec((1, HEAD_DIM), lambda b, h, i: (0, 0))],
        out_specs=pl.BlockSpec((tq, HEAD_DIM), lambda b, h, i: (b * lay.NG + i, h)),
        scratch_shapes=[pltpu.VMEM((S, HEAD_DIM), BF16)],
        compiler_params=_params("parallel", "parallel", "arbitrary"),
        name="diff_attention",
    )(q_rot, k_rot, px, lam_p, subln_g.reshape(1, HEAD_DIM))


def _chunk_scan(y, reverse):
    S = y.shape[0]
    pos = lax.broadcasted_iota(jnp.int32, y.shape, 0) % CHUNK
    sh = 1
    while sh < CHUNK:
        if reverse:
            y = y + jnp.where(pos < CHUNK - sh, pltpu.roll(y, S - sh, 0), 0.0)
        else:
            y = y + jnp.where(pos >= sh, pltpu.roll(y, sh, 0), 0.0)
        sh *= 2
    return y


def _chunk_cumsum(y, reverse):
    S, C = y.shape
    nc = S // CHUNK
    ii = lax.broadcasted_iota(jnp.int32, (nc, CHUNK, CHUNK), 1)
    jj = lax.broadcasted_iota(jnp.int32, (nc, CHUNK, CHUNK), 2)
    tri = jnp.where((ii <= jj) if reverse else (ii >= jj), 1.0, 0.0).astype(BF16)
    y3 = y.reshape(nc, CHUNK, C)
    p1 = y3.astype(BF16)
    r1 = y3 - p1.astype(F32)
    p2 = r1.astype(BF16)
    p3 = (r1 - p2.astype(F32)).astype(BF16)
    out = _bmm(tri, p1) + _bmm(tri, p2) + _bmm(tri, p3)
    return out.reshape(S, C)


def _chunk_of_step(n, d, lay):
    ncc = lay.Lc // CHUNK
    nc = lay.S // CHUNK
    if d == 0:
        return n
    return jnp.where(n < ncc, ncc - 1 - n, nc - 1 - (n - ncc))


def _gated_head_norm(o, z, g):
    y = o * lax.rsqrt(jnp.mean(o * o, axis=-1, keepdims=True) + EPS) * g
    return (y * _silu(z)).astype(BF16)


def _centred_conv(x, w, prev_ok, next_ok):
    S = x.shape[0]
    xp = jnp.where(prev_ok, pltpu.roll(x, 1, 0), 0.0)
    xn = jnp.where(next_ok, pltpu.roll(x, S - 1, 0), 0.0)
    return xp * w[0:1] + x * w[1:2] + xn * w[2:3]


def _conv_masks(shape, lay):
    row = lax.broadcasted_iota(jnp.int32, shape, 0)
    prev_ok = (row * (row - lay.Lc)) != 0
    next_ok = ((row - (lay.Lc - 1)) * (row - (lay.S - 1))) != 0
    return prev_ok, next_ok


TRI_BLOCK = 8


def _bmm(x, y):
    return jnp.einsum('gij,gjk->gik', x, y, preferred_element_type=F32)


def _bmm_nt(x, y):
    return jnp.einsum('gik,gjk->gij', x, y, preferred_element_type=F32)


def _bmm_tn(x, y):
    return jnp.einsum('gjk,gjl->gkl', x, y, preferred_element_type=F32)


def _column_spread_matrix():
    m = jnp.arange(CHUNK)[:, None]
    c = jnp.arange(TRI_BLOCK * LANES)[None, :]
    j, lane = c // LANES, c % LANES
    hit = (lane < CHUNK) & (m % TRI_BLOCK == j) & (m // TRI_BLOCK == lane // TRI_BLOCK)
    return hit.astype(BF16)


def _unit_tri_inverse(a, spread, ii, jj, d):
    G = a.shape[0]
    nb = TRI_BLOCK
    a_d = jnp.where(ii // nb == jj // nb, a, 0.0).reshape(G * CHUNK, CHUNK)
    cols = jnp.dot(a_d.astype(BF16), spread, preferred_element_type=F32)
    row = lax.broadcasted_iota(jnp.int32, (G, CHUNK, LANES), 1)
    lane = lax.broadcasted_iota(jnp.int32, (G, CHUNK, LANES), 2)
    x = jnp.where(row == lane, 1.0, 0.0)
    for j in (range(nb - 1) if d == 0 else range(nb - 1, 0, -1)):
        col_j = cols[:, j * LANES:(j + 1) * LANES].reshape(G, CHUNK, LANES)
        x4 = x.reshape(G, CHUNK // nb, nb, LANES)
        row_j = jnp.broadcast_to(x4[:, :, j:j + 1, :], x4.shape).reshape(G, CHUNK, LANES)
        x = x - col_j * row_j
    t = x[:, :, :CHUNK]
    width = nb
    while width < CHUNK:
        e = jnp.where(((ii // width) ^ (jj // width)) == 1, a, 0.0).astype(BF16)
        tb = t.astype(BF16)
        t = t - _bmm(tb, _bmm(e, tb).astype(BF16))
        width *= 2
    return t


def _gdn_gates_body(ps_ref, alog_ref, dtb_ref, o_ref, *, n_heads):
    p = ps_ref[...]
    lane = lax.broadcasted_iota(jnp.int32, p.shape, 1)
    g_all = -jnp.exp(alog_ref[...]) * _softplus(p + dtb_ref[...])
    o_ref[...] = jnp.where(lane < n_heads, _chunk_scan(g_all, False),
                           jnp.where(lane < 2 * n_heads, _chunk_scan(g_all, True), _sigmoid(p)))


def gdn_gates(ps, alog_l, dtb_l, lay, H):
    T = ps.shape[0]
    S = lay.S
    one = pl.BlockSpec((1, LANES), lambda b: (0, 0))
    return pl.pallas_call(
        functools.partial(_gdn_gates_body, n_heads=H),
        out_shape=jax.ShapeDtypeStruct((T, LANES), F32),
        grid=(lay.B,),
        in_specs=[pl.BlockSpec((S, LANES), lambda b: (b, 0)), one, one],
        out_specs=pl.BlockSpec((S, LANES), lambda b: (b, 0)),
        compiler_params=_params("parallel"),
        name="gdn_gates",
    )(ps, alog_l, dtb_l)


def _gdn_body(q_ref, k_ref, v_ref, z_ref, ps_ref, cq_ref, ck_ref, cv_ref, g_ref, sel_ref, o_ref,
              qs, ks, vs, gc, bb, p_s, r_s, k_s, n_s, dl_s, o_s, *, lay, n_heads, group):
    h = pl.program_id(1)
    S = lay.S
    nc = S // CHUNK
    shape = (S, HEAD_DIM)
    prev_ok, next_ok = _conv_masks(shape, lay)
    lane = lax.broadcasted_iota(jnp.int32, shape, 1)

    q = _silu(_centred_conv(q_ref[...], cq_ref[...], prev_ok, next_ok))
    qs[...] = q * lax.rsqrt(jnp.sum(q * q, axis=-1, keepdims=True) + EPS) * HEAD_DIM ** -0.5
    k = _silu(_centred_conv(k_ref[...], ck_ref[...], prev_ok, next_ok))
    ks[...] = k * lax.rsqrt(jnp.sum(k * k, axis=-1, keepdims=True) + EPS)
    vs[...] = _silu(_centred_conv(v_ref[...], cv_ref[...], prev_ok, next_ok))

    p = ps_ref[...]
    for d in range(2):
        gsel = jnp.sum(jnp.where(lane == d * n_heads + h, p, 0.0), axis=-1, keepdims=True)
        gc[d] = jnp.broadcast_to(gsel, shape)
        bsel = jnp.sum(jnp.where(lane == (2 + d) * n_heads + h, p, 0.0), axis=-1, keepdims=True)
        bb[d] = jnp.broadcast_to(bsel, shape)

    G = group
    ii = lax.broadcasted_iota(jnp.int32, (G, CHUNK, CHUNK), 1)
    jj = lax.broadcasted_iota(jnp.int32, (G, CHUNK, CHUNK), 2)
    bmm, bmm_nt, bmm_tn = _bmm, _bmm_nt, _bmm_tn

    def prep(gi, d):
        rows = G * CHUNK
        sl = pl.ds(pl.multiple_of(gi * rows, rows), rows)
        cs = pl.ds(gi * G, G)
        r3 = lambda x: x.reshape(G, CHUNK, HEAD_DIM)
        q3, k3, v3, gcb, b3 = r3(qs[sl, :]), r3(ks[sl, :]), r3(vs[sl, :]), r3(gc[d, sl, :]), r3(bb[d, sl, :])
        glast = gcb[:, CHUNK - 1:CHUNK, :] if d == 0 else gcb[:, 0:1, :]
        eg = jnp.exp(gcb)
        kb = k3 * b3
        diff = gcb[:, :, :CHUNK] - jnp.swapaxes(gcb, 1, 2)[:, :CHUNK, :]
        incl = (ii >= jj) if d == 0 else (ii <= jj)
        strict = (ii > jj) if d == 0 else (ii < jj)
        decay = jnp.exp(jnp.where(incl, diff, MASK_NEG))
        kbf = k3.astype(BF16)
        a = jnp.where(strict, bmm_nt(kb.astype(BF16), kbf) * decay, 0.0)
        tb = _unit_tri_inverse(a, sel_ref[...], ii, jj, d).astype(BF16)
        ub = bmm(tb, (v3 * b3).astype(BF16)).astype(BF16)
        wb = bmm(tb, (kb * eg).astype(BF16)).astype(BF16)
        attb = (bmm_nt(q3.astype(BF16), kbf) * decay).astype(BF16)
        kgb = (k3 * jnp.exp(glast - gcb)).astype(BF16)
        p_s[d, sl, :] = (q3 * eg - bmm(attb, wb)).astype(BF16).reshape(rows, HEAD_DIM)
        r_s[d, sl, :] = bmm(attb, ub).reshape(rows, HEAD_DIM)
        k_s[d, cs, :, :] = bmm_tn(kgb, wb).astype(BF16)
        n_s[d, cs, :, :] = bmm_tn(kgb, ub)
        dl_s[d, cs, :, :] = jnp.broadcast_to(jnp.exp(glast), (G, 8, HEAD_DIM))

    def prep_loop(gi, carry):
        prep(gi, 0)
        prep(gi, 1)
        return carry

    lax.fori_loop(0, nc // G, prep_loop, 0)

    def step_d(d, n, s):
        c = _chunk_of_step(n, d, lay)
        sl = pl.ds(pl.multiple_of(c * CHUNK, CHUNK), CHUNK)
        sb = s.astype(BF16)
        o_s[d, sl, :] = r_s[d, sl, :] + jnp.dot(p_s[d, sl, :], sb, preferred_element_type=F32)
        dl = dl_s[d, pl.ds(c, 1), :, :][0][0:1, :]
        return (s * dl + n_s[d, pl.ds(c, 1), :, :][0]
                - jnp.dot(k_s[d, pl.ds(c, 1), :, :][0], sb, preferred_element_type=F32))

    def both(n, carry):
        return (step_d(0, n, carry[0]), step_d(1, n, carry[1]))

    zero = jnp.zeros((HEAD_DIM, HEAD_DIM), F32)
    lax.fori_loop(0, nc, both, (zero, zero), unroll=4 if nc % 4 == 0 else 1)
    o_ref[...] = _gated_head_norm(o_s[0] + o_s[1], z_ref[...], g_ref[...])


def gated_deltanet(px, gates, conv_w, norm_g, lay, W, col0):
    T = px.shape[0]
    H = W // HEAD_DIM
    S = lay.S
    nc = S // CHUNK

    def col(o):
        return pl.BlockSpec((S, HEAD_DIM), lambda b, h: (b, col0 + o * H + h))

    def cw(o):
        return pl.BlockSpec((3, HEAD_DIM), lambda b, h: (0, o * H + h))

    one = pl.BlockSpec((1, HEAD_DIM), lambda b, h: (0, 0))
    sh = (S, HEAD_DIM)
    st = (2, nc, HEAD_DIM, HEAD_DIM)
    group = max(g for g in range(1, 37) if nc % g == 0)
    return pl.pallas_call(
        functools.partial(_gdn_body, lay=lay, n_heads=H, group=group),
        out_shape=jax.ShapeDtypeStruct((T, W), BF16),
        grid=(lay.B, H),
        in_specs=[col(0), col(1), col(2), col(3),
                  pl.BlockSpec((S, LANES), lambda b, h: (b, 0)),
                  cw(0), cw(1), cw(2), one,
                  pl.BlockSpec((CHUNK, TRI_BLOCK * LANES), lambda b, h: (0, 0))],
        out_specs=pl.BlockSpec((S, HEAD_DIM), lambda b, h: (b, h)),
        scratch_shapes=[pltpu.VMEM(sh, F32), pltpu.VMEM(sh, F32), pltpu.VMEM(sh, F32),
                        pltpu.VMEM((2,) + sh, F32), pltpu.VMEM((2,) + sh, F32),
                        pltpu.VMEM((2,) + sh, BF16), pltpu.VMEM((2,) + sh, F32),
                        pltpu.VMEM(st, BF16), pltpu.VMEM(st, F32),
                        pltpu.VMEM((2, nc, 8, HEAD_DIM), F32), pltpu.VMEM((2,) + sh, F32)],
        compiler_params=_params("parallel", "arbitrary"),
        name="gated_deltanet",
    )(px, px, px, px, gates, conv_w, conv_w, conv_w, norm_g.reshape(1, HEAD_DIM), _column_spread_matrix())


def _hgrn_body(q_ref, f0_ref, f1_ref, i_ref, z_ref, lb0_ref, lb1_ref, g_ref, o_ref,
               gc, kk, o_s, *, lay, layer):
    S = lay.S
    nc = S // CHUNK
    for d, (f_ref, lb_ref) in enumerate(((f0_ref, lb0_ref), (f1_ref, lb1_ref))):
        lg = lb_ref[...]
        depth = lg.shape[0]
        mx = lg[0:1]
        for r in range(1, depth):
            mx = jnp.maximum(mx, lg[r:r + 1])
        e = [jnp.exp(lg[r:r + 1] - mx) for r in range(depth)]
        tot = e[0]
        for r in range(1, depth):
            tot = tot + e[r]
        lb = jnp.zeros_like(mx)
        for r in range(1, layer + 1):
            lb = lb + e[r] / tot
        f = f_ref[...]
        e_neg = jnp.exp(-jnp.abs(f))
        big = 1.0 / (1.0 + e_neg)
        small = e_neg * big
        pos = f >= 0.0
        logf = jnp.log(jnp.maximum(lb, LB_FLOOR) + (1.0 - lb) * jnp.where(pos, big, small))
        gc[d] = _chunk_cumsum(logf, reverse=(d == 1))
        kk[d] = (1.0 - lb) * jnp.where(pos, small, big)

    SB = 16
    nsb = CHUNK // SB
    jrow = lax.broadcasted_iota(jnp.int32, (CHUNK, HEAD_DIM), 0)
    jsub = lax.broadcasted_iota(jnp.int32, (SB, HEAD_DIM), 0)
    alane = lax.broadcasted_iota(jnp.int32, (SB, CHUNK), 1)

    def intra_t(d, qc, kc, g):
        if d == 0:
            refs = [g[0:1, :]] + [g[SB * I - 1:SB * I, :] for I in range(1, nsb)]
            far = range(1, nsb)
        else:
            refs = [g[SB * (I + 1):SB * (I + 1) + 1, :] for I in range(nsb - 1)] + [g[CHUNK - 1:CHUNK, :]]
            far = range(nsb - 1)
        rvec = jnp.concatenate([jnp.broadcast_to(r, (SB, HEAD_DIM)) for r in refs], axis=0)
        qt = qc * jnp.exp(g - rvec)
        off = None
        for I in far:
            seen = (jrow < SB * I) if d == 0 else (jrow >= SB * (I + 1))
            kt = (kc * jnp.exp(jnp.where(seen, refs[I] - g, MASK_NEG))).astype(BF16)
            qi = jnp.where((jrow // SB) == I, qt, 0.0).astype(BF16)
            part = lax.dot_general(kt, qi, _NT, preferred_element_type=F32)
            off = part if off is None else off + part
        blocks = []
        for I in range(nsb):
            rs = slice(SB * I, SB * (I + 1))
            g_i, k_i = g[rs, :], kc[rs, :]
            acc = jnp.zeros((SB, CHUNK), F32)
            for r in range(SB):
                i = SB * I + r
                keep = (jsub <= r) if d == 0 else (jsub >= r)
                rel = jnp.exp(jnp.where(keep, g[i:i + 1, :] - g_i, MASK_NEG))
                colv = jnp.sum(rel * k_i * qc[i:i + 1, :], axis=-1, keepdims=True)
                acc = jnp.where(alane == i, colv, acc)
            blocks.append(acc)
        return jnp.concatenate(blocks, axis=0) + off

    def step_d(d, n, st):
        c = _chunk_of_step(n, d, lay)
        sl = pl.ds(pl.multiple_of(c * CHUNK, CHUNK), CHUNK)
        qc, kc, vc, g = q_ref[sl, :], kk[d, sl, :], i_ref[sl, :], gc[d, sl, :]
        glast = g[CHUNK - 1:CHUNK, :] if d == 0 else g[0:1, :]
        a_t = intra_t(d, qc, kc, g)
        vb = vc.astype(BF16)
        o = (lax.dot_general((qc * jnp.exp(g)).astype(BF16), st.astype(BF16), _NT, preferred_element_type=F32)
             + lax.dot_general(a_t.astype(BF16), vb, _TN, preferred_element_type=F32))
        o_s[d, sl, :] = o
        kg = (kc * jnp.exp(glast - g)).astype(BF16)
        return st * jnp.exp(glast) + lax.dot_general(vb, kg, _TN, preferred_element_type=F32)

    def both(n, carry):
        return (step_d(0, n, carry[0]), step_d(1, n, carry[1]))

    zero = jnp.zeros((HEAD_DIM, HEAD_DIM), F32)
    lax.fori_loop(0, nc, both, (zero, zero), unroll=4 if nc % 4 == 0 else 1)
    o_ref[...] = _gated_head_norm(o_s[0] + o_s[1], z_ref[...], g_ref[...])


def hgrn2(px, lb_logits, norm_g, lay, W, col0, layer):
    T = px.shape[0]
    H = W // HEAD_DIM
    S = lay.S
    depth = lb_logits.shape[0]

    def col(o):
        return pl.BlockSpec((S, HEAD_DIM), lambda b, h: (b, col0 + o * H + h))

    def lbs(d):
        return pl.BlockSpec((depth, HEAD_DIM), lambda b, h: (0, d * H + h))

    sh = (S, HEAD_DIM)
    return pl.pallas_call(
        functools.partial(_hgrn_body, lay=lay, layer=layer),
        out_shape=jax.ShapeDtypeStruct((T, W), BF16),
        grid=(lay.B, H),
        in_specs=[col(0), col(1), col(2), col(3), col(4), lbs(0), lbs(1),
                  pl.BlockSpec((1, HEAD_DIM), lambda b, h: (0, 0))],
        out_specs=pl.BlockSpec((S, HEAD_DIM), lambda b, h: (b, h)),
        scratch_shapes=[pltpu.VMEM((2,) + sh, F32), pltpu.VMEM((2,) + sh, F32), pltpu.VMEM((2,) + sh, F32)],
        compiler_params=_params("parallel", "arbitrary"),
        name="hgrn2",
    )(px, px, px, px, px, lb_logits, lb_logits, norm_g.reshape(1, HEAD_DIM))


def _split_bf16(x):
    hi = x.astype(BF16)
    return hi, (x - hi.astype(F32)).astype(BF16)


def _dot3(a_hi, a_lo, b_hi, b_lo):
    return (jnp.dot(a_hi, b_hi, preferred_element_type=F32)
            + jnp.dot(a_hi, b_lo, preferred_element_type=F32)
            + jnp.dot(a_lo, b_hi, preferred_element_type=F32))


def _hy_filter_body(z_ref, w1_ref, b1_ref, w2_ref, b2_ref, w3_ref, fr_ref, dl_ref, o_ref, *, L):
    i = pl.program_id(0)
    z = z_ref[...]
    tr = z.shape[0]
    W = o_ref.shape[1]
    h = jnp.sin(fr_ref[0:1, :] * (jnp.dot(z, w1_ref[...], precision=HIGHEST, preferred_element_type=F32) + b1_ref[...]))
    h = jnp.sin(fr_ref[1:2, :] * (jnp.dot(h, w2_ref[...], precision=HIGHEST, preferred_element_type=F32) + b2_ref[...]))
    h = jnp.dot(h, w3_ref[...], precision=HIGHEST, preferred_element_type=F32)
    h = h * jnp.exp(-z[:, 0:1] * dl_ref[...])
    row = lax.broadcasted_iota(jnp.int32, (tr, W), 0) + i * tr
    o_ref[...] = jnp.where(row < L, h[:, :W], jnp.where(row == L, 0.0, h[:, W:]))


def hyena_filter(z2, w1p, b1, w2, b2, w3, freq, deltas2, L):
    n = z2.shape[0]
    W = w3.shape[1] // 2
    O = w2.shape[0]
    tr = _tile(n, 512)
    full = lambda a: pl.BlockSpec(a.shape, lambda i: (0,) * a.ndim)
    args = (w1p, b1.reshape(1, O), w2, b2.reshape(1, O), w3, freq, deltas2)
    return pl.pallas_call(
        functools.partial(_hy_filter_body, L=L),
        out_shape=jax.ShapeDtypeStruct((n, W), F32),
        grid=(n // tr,),
        in_specs=[pl.BlockSpec((tr, LANES), lambda i: (i, 0))] + [full(a) for a in args],
        out_specs=pl.BlockSpec((tr, W), lambda i: (i, 0)),
        compiler_params=_params("parallel"),
        name="hyena_filter",
    )(z2, *args)


def _hy_prep_body(v_ref, x0_ref, x1_ref, cv_ref, c0_ref, c1_ref, bv_ref, b0_ref, b1_ref,
                  vgx_ref, vge_ref, x0u_ref, *, lay):
    shape = v_ref.shape
    prev_ok, next_ok = _conv_masks(shape, lay)
    v = _centred_conv(v_ref[...], cv_ref[...], prev_ok, next_ok) + bv_ref[...]
    x0 = _centred_conv(x0_ref[...], c0_ref[...], prev_ok, next_ok) + b0_ref[...]
    x1 = _centred_conv(x1_ref[...], c1_ref[...], prev_ok, next_ok) + b1_ref[...]
    vg = v * x1
    vge_ref[...] = vg[:lay.Lc, :]
    vgx_ref[...] = vg[lay.Lc:, :]
    x0u_ref[...] = x0


def hyena_prep(px, conv_w, conv_b, lay, W, col0):
    T = px.shape[0]
    H = W // HEAD_DIM
    S, L, Lc, B = lay.S, lay.L, lay.Lc, lay.B

    def col(o):
        return pl.BlockSpec((S, HEAD_DIM), lambda b, j: (b, col0 + o * H + j))

    def cw(o, rows):
        return pl.BlockSpec((rows, HEAD_DIM), lambda b, j: (0, o * H + j))

    return pl.pallas_call(
        functools.partial(_hy_prep_body, lay=lay),
        out_shape=(jax.ShapeDtypeStruct((B * L, W), F32), jax.ShapeDtypeStruct((B * Lc, W), F32),
                   jax.ShapeDtypeStruct((T, W), F32)),
        grid=(B, H),
        in_specs=[col(0), col(1), col(2), cw(0, 3), cw(1, 3), cw(2, 3), cw(0, 1), cw(1, 1), cw(2, 1)],
        out_specs=(pl.BlockSpec((L, HEAD_DIM), lambda b, j: (b, j)),
                   pl.BlockSpec((Lc, HEAD_DIM), lambda b, j: (b, j)),
                   pl.BlockSpec((S, HEAD_DIM), lambda b, j: (b, j))),
        compiler_params=_params("parallel", "parallel"),
        name="hyena_prep",
    )(px, px, px, conv_w, conv_w, conv_w, conv_b, conv_b, conv_b)


def _dft_tables(L):
    n = 2 * L
    FT = min(256, L)
    sub = 64
    idx = jnp.arange(n, dtype=jnp.int32)
    r = idx % (2 * FT)
    is_im = r >= FT
    f = (idx // (2 * FT)) * FT + jnp.where(is_im, r - FT, r)
    nyq = is_im & (f == 0)
    f = jnp.where(nyq, L, f)
    step = jnp.arange(sub, dtype=jnp.int32)
    ang_hi = (2.0 * math.pi / n) * ((f[:, None] * (step * sub)[None, :]) % n).astype(F32)
    ang_lo = (2.0 * math.pi / n) * ((f[:, None] * step[None, :]) % n).astype(F32)
    ch, sh, cl, sl = jnp.cos(ang_hi), jnp.sin(ang_hi), jnp.cos(ang_lo), jnp.sin(ang_lo)
    wgt = jnp.where((f == 0) | (f == L), 1.0 / n, 2.0 / n)
    minus_sin = is_im & ~nyq

    def table(hi_steps, transpose):
        if transpose:
            c_h, s_h, c_l, s_l = (t.T for t in (ch, sh, cl, sl))
            cos = c_h[:hi_steps, None, :] * c_l[None, :, :] - s_h[:hi_steps, None, :] * s_l[None, :, :]
            sin = s_h[:hi_steps, None, :] * c_l[None, :, :] + c_h[:hi_steps, None, :] * s_l[None, :, :]
            out = jnp.where(minus_sin[None, None, :], -sin, cos) * wgt[None, None, :]
            return out.reshape(hi_steps * sub, n)
        cos = ch[:, :hi_steps, None] * cl[:, None, :] - sh[:, :hi_steps, None] * sl[:, None, :]
        sin = sh[:, :hi_steps, None] * cl[:, None, :] + ch[:, :hi_steps, None] * sl[:, None, :]
        return jnp.where(minus_sin[:, None, None], -sin, cos).reshape(n, hi_steps * sub)

    return table(n // sub, False), table(L // sub, False), table(L // sub, True), FT


def _hy_fwd_body(a_ref, u_ref, kf_ref, y_ref, *, FT):
    t = pl.program_id(2)
    x = jnp.dot(a_ref[...], u_ref[...].astype(BF16), preferred_element_type=F32)
    kf = kf_ref[...]
    xr, xi, kr, ki = x[:FT], x[FT:], kf[:FT], kf[FT:]
    row = lax.broadcasted_iota(jnp.int32, xr.shape, 0)
    packed = (row + t) == 0
    yr = jnp.where(packed, xr * kr, xr * kr - xi * ki)
    yi = jnp.where(packed, xi * ki, xr * ki + xi * kr)
    y_ref[...] = jnp.concatenate([yr, yi], axis=0).astype(BF16)


def _hy_spec_body(ah_ref, al_ref, u_ref, o_ref):
    uh, ul = _split_bf16(u_ref[...])
    o_ref[...] = _dot3(ah_ref[...], al_ref[...], uh, ul)


def hyena_spectrum(fh, fl, kern, FT):
    n, W = kern.shape
    tn = _tile(W, 256)
    return pl.pallas_call(
        _hy_spec_body,
        out_shape=jax.ShapeDtypeStruct((n, W), F32),
        grid=(W // tn, n // (2 * FT)),
        in_specs=[pl.BlockSpec((2 * FT, n), lambda j, t: (t, 0)),
                  pl.BlockSpec((2 * FT, n), lambda j, t: (t, 0)),
                  pl.BlockSpec((n, tn), lambda j, t: (0, j))],
        out_specs=pl.BlockSpec((2 * FT, tn), lambda j, t: (t, j)),
        compiler_params=_params("parallel", "arbitrary"),
        name="hyena_spectrum",
    )(fh, fl, kern)


def hyena_fwd(f_b, vg, kf, B, Lq, FT):
    W = vg.shape[1]
    n = 2 * Lq
    nt = n // (2 * FT)
    tn = _tile(W, 1024)
    return pl.pallas_call(
        functools.partial(_hy_fwd_body, FT=FT),
        out_shape=jax.ShapeDtypeStruct((B * n, W), BF16),
        grid=(B, W // tn, nt),
        in_specs=[pl.BlockSpec((2 * FT, Lq), lambda b, j, t: (t, 0)),
                  pl.BlockSpec((Lq, tn), lambda b, j, t: (b, j)),
                  pl.BlockSpec((2 * FT, tn), lambda b, j, t: (t, j))],
        out_specs=pl.BlockSpec((2 * FT, tn), lambda b, j, t: (b * nt + t, j)),
        compiler_params=_params("parallel", "parallel", "arbitrary"),
        name="hyena_fwd",
    )(f_b, vg, kf)


def _hy_inv_body(g_ref, y_ref, vg_ref, x0_ref, bias_ref, o_ref):
    y = jnp.dot(g_ref[...], y_ref[...], preferred_element_type=F32)
    o_ref[...] = (x0_ref[...] * (y + vg_ref[...] * bias_ref[...])).astype(BF16)


def hyena_inv(g_b, y_b, vg, x0u, bias, lay, Lq, row0):
    W = vg.shape[1]
    n = 2 * Lq
    B = lay.B
    tt = lay.gr
    ntt = Lq // tt
    tn = _tile(W, 1024)
    g0 = row0 // tt
    return pl.pallas_call(
        _hy_inv_body,
        out_shape=jax.ShapeDtypeStruct((B * Lq, W), BF16),
        grid=(B, W // tn, ntt),
        in_specs=[pl.BlockSpec((tt, n), lambda b, j, i: (i, 0)),
                  pl.BlockSpec((n, tn), lambda b, j, i: (b, j)),
                  pl.BlockSpec((tt, tn), lambda b, j, i: (b * ntt + i, j)),
                  pl.BlockSpec((tt, tn), lambda b, j, i: (b * lay.NG + g0 + i, j)),
                  pl.BlockSpec((1, tn), lambda b, j, i: (0, j))],
        out_specs=pl.BlockSpec((tt, tn), lambda b, j, i: (b * ntt + i, j)),
        compiler_params=_params("parallel", "parallel", "arbitrary"),
        name="hyena_inv",
    )(g_b, y_b, vg, x0u, bias.reshape(1, W))


def _hyena_features(L):
    bands = (HY_EMB - 1) // 2
    t = jnp.linspace(0.0, 1.0, L, dtype=F32)[:, None]
    wpos = 2.0 * math.pi * jnp.arange(L, dtype=F32)[:, None] / L
    fb = jnp.linspace(1e-4, bands - 1, bands, dtype=F32)[None]
    z = jnp.concatenate([t, jnp.cos(fb * wpos), -jnp.sin(fb * wpos)], axis=-1)
    z2 = jnp.concatenate([z, z[::-1]], axis=0)
    return jnp.pad(z2, ((0, 0), (0, LANES - HY_EMB)))


def hyena_stream(vg, x0u, bias, wts, lay, Lq, row0):
    w1p, b1, w2, b2, w3, freq, deltas2 = wts
    fwd_full, fwd_half, inv, FT = _dft_tables(Lq)
    kh, kl = _split_bf16(fwd_full)
    kern = hyena_filter(_hyena_features(Lq), w1p, b1, w2, b2, w3, freq, deltas2, Lq)
    kf = hyena_spectrum(kh, kl, kern, FT)
    y_b = hyena_fwd(fwd_half.astype(BF16), vg, kf, lay.B, Lq, FT)
    return hyena_inv(inv.astype(BF16), y_b, vg, x0u, bias, lay, Lq, row0)


def _rope_tables(lay):
    n_freq = DIFF_DIM // 4
    inv = ROPE_THETA ** (-jnp.arange(n_freq, dtype=F32) / n_freq)
    rows = lay.L // GRID_W
    r = jnp.repeat(jnp.arange(rows, dtype=F32), GRID_W)
    col = jnp.tile(jnp.arange(GRID_W, dtype=F32), rows)
    ang = jnp.concatenate([r[:, None] * inv, col[:, None] * inv], axis=-1)
    cos, sin = jnp.cos(ang), jnp.sin(ang)
    cos = jnp.concatenate([jnp.ones((lay.Lc, DIFF_DIM // 2), F32), cos], axis=0)
    sin = jnp.concatenate([jnp.zeros((lay.Lc, DIFF_DIM // 2), F32), sin], axis=0)
    cos_t = jnp.tile(cos, (1, 4))
    sin_t = jnp.tile(jnp.concatenate([-sin, sin], axis=-1), (1, 2))
    return cos_t, sin_t


def _main_cols(W, H):
    small0 = 7 * W
    small1 = small0 + 4 * H
    return small0, small1


def mixers(px1, px2, ps, lay, W, layer, need_ctx, p):
    H = W // HEAD_DIM
    lam_init = 0.8 - 0.6 * math.exp(-0.3 * layer)
    q_rot, k_rot = attn_prep(px1, p["cos_t"], p["sin_t"], p["qk_g2"], lay, W)
    att = attention(q_rot, k_rot, px1, p["attn_lambda"], p["attn_subln_g"], lay, W, lam_init)
    gates = gdn_gates(ps, p["alog"], p["dtb"], lay, H)
    gdn = gated_deltanet(px1, gates, p["gdn_conv_w"], p["gdn_norm_g"], lay, W, 3 * H)
    hg = hgrn2(px2, p["hg_lb_logits"], p["hg_norm_g"], lay, W, 0, layer)
    vgx, vge, x0u = hyena_prep(px2, p["hy_conv_w"], p["hy_conv_b"], lay, W, 5 * H)
    hy_x = hyena_stream(vgx, x0u, p["hy_bias"], p["hy_wts"], lay, lay.L, lay.Lc)
    if need_ctx:
        hy_e = hyena_stream(vge, x0u, p["hy_bias"], p["hy_wts"], lay, lay.Lc, 0)
    else:
        hy_e = jnp.zeros((lay.B, lay.Lc, W), BF16)
    hy = jnp.concatenate([hy_e.reshape(lay.B, lay.Lc, W), hy_x.reshape(lay.B, lay.L, W)], axis=1)
    return att, gdn, hg, hy.reshape(lay.T, W)


def _layer_params(l, lay, W, attn_qk_g, attn_lambda, attn_subln_g, gdn_conv_w, gdn_a_log, gdn_dt_bias, gdn_norm_g,
                  hg_lb_logits, hg_norm_g, hy_conv_w, hy_conv_b, hy_w1, hy_b1, hy_w2, hy_b2, hy_w3, hy_freq,
                  hy_bias, cos_t, sin_t):
    H = W // HEAD_DIM
    pad16 = lambda a: jnp.pad(a.astype(F32).reshape(1, 2 * H), ((0, 0), (0, LANES - 2 * H)))
    deltas = jnp.abs(jnp.linspace(math.log(HY_TARGET) / HY_SLOW_DECAY, math.log(HY_TARGET) / HY_FAST_DECAY, W, dtype=F32))
    return dict(
        cos_t=cos_t, sin_t=sin_t,
        qk_g2=jnp.tile(attn_qk_g[l].astype(F32), (1, 2)),
        attn_lambda=attn_lambda[l].astype(F32), attn_subln_g=attn_subln_g[l].astype(F32),
        gdn_conv_w=gdn_conv_w[l], alog=pad16(gdn_a_log[l]), dtb=pad16(gdn_dt_bias[l]), gdn_norm_g=gdn_norm_g[l],
        hg_lb_logits=hg_lb_logits.astype(F32).reshape(hg_lb_logits.shape[0], 2 * W), hg_norm_g=hg_norm_g[l],
        hy_conv_w=hy_conv_w[l], hy_conv_b=hy_conv_b[l].reshape(1, 3 * W), hy_bias=hy_bias[l].astype(F32),
        hy_wts=(jnp.pad(hy_w1[l].astype(F32), ((0, LANES - HY_EMB), (0, 0))), hy_b1[l].astype(F32),
                hy_w2[l].astype(F32), hy_b2[l].astype(F32), hy_w3[l].astype(F32), hy_freq[l].astype(F32),
                jnp.tile(deltas, 2).reshape(1, 2 * W)),
    )


def kernel(x, c, ctx, c_ctx, norm_g, w_mod, b_mod, ffn_w_in, ffn_w_out, w_in, attn_qk_g, attn_lambda, attn_subln_g, gdn_conv_w, gdn_a_log, gdn_dt_bias, gdn_norm_g, hg_lb_logits, hg_norm_g, hy_conv_w, hy_conv_b, hy_w1, hy_b1, hy_w2, hy_b2, hy_w3, hy_freq, hy_bias, w_gate, w_up, w_out):
    B, L, D = x.shape
    Lc = ctx.shape[1]
    depth = w_mod.shape[0]
    W = D // N_BRANCH
    H = W // HEAD_DIM
    lay = Layout(B, L, Lc)
    h = jnp.concatenate([ctx, x], axis=1).reshape(lay.T, D)
    G = -(-(B + 1) // 8) * 8
    c_all = jnp.concatenate([c, c_ctx[None], jnp.zeros((G - B - 1, D), F32)], axis=0)
    cos_t, sin_t = _rope_tables(lay)
    small0, small1 = _main_cols(W, H)
    w_gate_b, w_up_b = w_gate.astype(BF16), w_up.astype(BF16)
    for l in range(depth):
        last = l == depth - 1
        p = _layer_params(l, lay, W, attn_qk_g, attn_lambda, attn_subln_g, gdn_conv_w, gdn_a_log, gdn_dt_bias,
                          gdn_norm_g, hg_lb_logits, hg_norm_g, hy_conv_w, hy_conv_b, hy_w1, hy_b1, hy_w2, hy_b2,
                          hy_w3, hy_freq, hy_bias, cos_t, sin_t)
        mod = mod_table(c_all, w_mod, b_mod, l).reshape(G, MOD_CHUNKS, D)
        act = mm_swiglu(norm_mod(h, norm_g[l, 0], mod, lay, 0), ffn_w_in, (l, 0))
        h = mm_resid(act, ffn_w_out, (l, 0), h, mod, lay, 2, 0.5)
        xn = norm_mod(h, norm_g[l, 1], mod, lay, 1)
        cols_t = lambda lo, hi: w_in[l, :, lo:hi].T
        px1 = mm_plain(xn, cols_t(0, small0))
        px2 = mm_plain(xn, cols_t(small1, w_in.shape[-1]))
        ps = mm_plain(xn, jnp.pad(cols_t(small0, small1), ((0, LANES - (small1 - small0)), (0, 0))), tn_pref=LANES)
        branches = mixers(px1, px2, ps, lay, W, l, not last, p)
        if last:
            latent = lambda a: a.reshape(B, lay.S, a.shape[-1])[:, Lc:, :].reshape(B * L, a.shape[-1])
            xn, h, branches = latent(xn), latent(h), tuple(latent(a) for a in branches)
            lay = Layout(B, L, 0, gr=lay.gr)
        acc = merge_branches(xn, branches, w_gate_b, w_up_b, l)
        h = mm_resid(acc, w_out, (l,), h, mod, lay, 5, 1.0)
        act = mm_swiglu(norm_mod(h, norm_g[l, 2], mod, lay, 2), ffn_w_in, (l, 1))
        h = mm_resid(act, ffn_w_out, (l, 1), h, mod, lay, 8, 0.5)
    return h.reshape(B, L, D)
```

```python
import functools
import math

import jax
import jax.numpy as jnp
from jax import lax
from jax.experimental import pallas as pl
from jax.experimental.pallas import tpu as pltpu

F32 = jnp.float32
BF16 = jnp.bfloat16

GRID_W = 64
N_BRANCH = 4
HEAD_DIM = 128
DIFF_DIM = HEAD_DIM // 2
MOD_CHUNKS = 9
ROPE_THETA = 10000.0
CHUNK = 64
HY_EMB = 33
HY_FAST_DECAY = 0.3
HY_SLOW_DECAY = 1.5
HY_TARGET = 1e-2
EPS = 1e-6
MASK_NEG = -1e30
LB_FLOOR = 1e-20

LANES = 128
V7X_VMEM_LIMIT = 56 * 1024 * 1024
HIGHEST = lax.Precision.HIGHEST

_NT = (((1,), (1,)), ((), ()))
_TN = (((0,), (0,)), ((), ()))


def _params(*sem):
    return pltpu.CompilerParams(dimension_semantics=sem, vmem_limit_bytes=V7X_VMEM_LIMIT)


def _tile(n, pref):
    t = min(n, pref)
    while n % t:
        t //= 2
    return t


def _wspec(at, K, tn, col):
    return pl.BlockSpec((None,) * len(at) + (K, tn), lambda *ids: tuple(at) + (0, col(*ids)))


def _sigmoid(x):
    return 1.0 / (1.0 + jnp.exp(-x))


def _silu(x):
    return x * _sigmoid(x)


def _softplus(x):
    return jnp.maximum(x, 0.0) + jnp.log1p(jnp.exp(-jnp.abs(x)))


class Layout:
    def __init__(self, B, L, Lc, gr=None):
        self.B, self.L, self.Lc = B, L, Lc
        self.S = L + Lc
        self.T = B * self.S
        self.gr = gr or math.gcd(L, Lc)
        self.NG = self.S // self.gr
        self.NGc = Lc // self.gr


def _mod_row(mod_ref, lay, r, k):
    b = r // lay.NG
    g = jnp.where(r - b * lay.NG < lay.NGc, lay.B, b)
    return mod_ref[pl.ds(g, 1), k:k + 1, :][0]


def _mod_body(c_ref, w_ref, b_ref, o_ref):
    a = _silu(c_ref[...]).astype(BF16)
    o_ref[...] = jnp.dot(a, w_ref[...].astype(BF16), preferred_element_type=F32) + b_ref[...]


def mod_table(c16, w_mod, b_mod, l):
    R, D = c16.shape
    N = w_mod.shape[-1]
    tn = _tile(N, 512)
    return pl.pallas_call(
        _mod_body,
        out_shape=jax.ShapeDtypeStruct((R, N), F32),
        grid=(N // tn,),
        in_specs=[pl.BlockSpec((R, D), lambda j: (0, 0)),
                  _wspec((l,), D, tn, lambda j: j),
                  _wspec((l,), 1, tn, lambda j: j)],
        out_specs=pl.BlockSpec((R, tn), lambda j: (0, j)),
        compiler_params=_params("parallel"),
        name="mod_table",
    )(c16, w_mod, b_mod.reshape(b_mod.shape[0], 1, N))


def _norm_mod_body(x_ref, g_ref, mod_ref, o_ref, *, lay, k):
    i = pl.program_id(0)
    gr = lay.gr
    n_sub = x_ref.shape[0] // gr
    for s in range(n_sub):
        x = x_ref[s * gr:(s + 1) * gr, :]
        y = x * lax.rsqrt(jnp.mean(x * x, axis=-1, keepdims=True) + EPS) * g_ref[...]
        r = i * n_sub + s
        shift = _mod_row(mod_ref, lay, r, 3 * k)
        scale = _mod_row(mod_ref, lay, r, 3 * k + 1)
        o_ref[s * gr:(s + 1) * gr, :] = (y * (1.0 + scale) + shift).astype(BF16)


def norm_mod(h, g, mod, lay, k):
    T, D = h.shape
    tm = 2 * lay.gr if T % (2 * lay.gr) == 0 else lay.gr
    G = mod.shape[0]
    return pl.pallas_call(
        functools.partial(_norm_mod_body, lay=lay, k=k),
        out_shape=jax.ShapeDtypeStruct((T, D), BF16),
        grid=(T // tm,),
        in_specs=[pl.BlockSpec((tm, D), lambda i: (i, 0)),
                  pl.BlockSpec((1, D), lambda i: (0, 0)),
                  pl.BlockSpec((G, MOD_CHUNKS, D), lambda i: (0, 0, 0))],
        out_specs=pl.BlockSpec((tm, D), lambda i: (i, 0)),
        compiler_params=_params("parallel"),
        name="norm_mod",
    )(h, g.reshape(1, D), mod)


def _mm_swiglu_body(a_ref, wg_ref, wu_ref, o_ref, wgb, wub):
    @pl.when(pl.program_id(1) == 0)
    def _():
        wgb[...] = wg_ref[...].astype(BF16)
        wub[...] = wu_ref[...].astype(BF16)

    a = a_ref[...]
    g = jnp.dot(a, wgb[...], preferred_element_type=F32)
    u = jnp.dot(a, wub[...], preferred_element_type=F32)
    o_ref[...] = (_silu(g) * u).astype(BF16)


def mm_swiglu(a, w, at):
    T, K = a.shape
    F = w.shape[-1] // 2
    tm, tn = _tile(T, 1024), _tile(F, 256)
    nj = F // tn
    return pl.pallas_call(
        _mm_swiglu_body,
        out_shape=jax.ShapeDtypeStruct((T, F), BF16),
        grid=(nj, T // tm),
        in_specs=[pl.BlockSpec((tm, K), lambda j, i: (i, 0)),
                  _wspec(at, K, tn, lambda j, i: j),
                  _wspec(at, K, tn, lambda j, i: j + nj)],
        out_specs=pl.BlockSpec((tm, tn), lambda j, i: (i, j)),
        scratch_shapes=[pltpu.VMEM((K, tn), BF16), pltpu.VMEM((K, tn), BF16)],
        compiler_params=_params("parallel", "arbitrary"),
        name="mm_swiglu",
    )(a, w, w)


def _mm_resid_body(a_ref, w_ref, h_ref, mod_ref, o_ref, wb, *, lay, k, scale):
    i = pl.program_id(1)

    @pl.when(i == 0)
    def _():
        wb[...] = w_ref[...].astype(BF16)

    y = jnp.dot(a_ref[...], wb[...], preferred_element_type=F32)
    gr = lay.gr
    n_sub = a_ref.shape[0] // gr
    for s in range(n_sub):
        gate = _mod_row(mod_ref, lay, i * n_sub + s, k)
        sl = slice(s * gr, (s + 1) * gr)
        o_ref[sl, :] = h_ref[sl, :] + (scale * gate) * y[sl, :]


def mm_resid(a, w, at, h, mod, lay, k, scale):
    T, K = a.shape
    N = w.shape[-1]
    G = mod.shape[0]
    def vmem_bytes(tm, tn):
        return 2 * tm * K * 2 + 2 * K * tn * 4 + K * tn * 2 + 4 * tm * tn * 4

    tm, tn = max(_tile(T, 1024), lay.gr), _tile(N, 512)
    if vmem_bytes(tm, tn) > (V7X_VMEM_LIMIT * 7) // 8:
        tm = max(_tile(T, 512), lay.gr)
    return pl.pallas_call(
        functools.partial(_mm_resid_body, lay=lay, k=k, scale=scale),
        out_shape=jax.ShapeDtypeStruct((T, N), F32),
        grid=(N // tn, T // tm),
        in_specs=[pl.BlockSpec((tm, K), lambda j, i: (i, 0)),
                  _wspec(at, K, tn, lambda j, i: j),
                  pl.BlockSpec((tm, tn), lambda j, i: (i, j)),
                  pl.BlockSpec((G, MOD_CHUNKS, tn), lambda j, i: (0, 0, j))],
        out_specs=pl.BlockSpec((tm, tn), lambda j, i: (i, j)),
        scratch_shapes=[pltpu.VMEM((K, tn), BF16)],
        compiler_params=_params("parallel", "arbitrary"),
        name="mm_resid",
    )(a, w, h, mod)


def _mm_plain_body(a_ref, wt_ref, o_ref, wb):
    @pl.when(pl.program_id(1) == 0)
    def _():
        wb[...] = wt_ref[...].T.astype(BF16)

    o_ref[...] = jnp.dot(a_ref[...], wb[...], preferred_element_type=F32)


def mm_plain(a, w_t, tn_pref=512):
    T, K = a.shape
    N = w_t.shape[0]
    tm, tn = _tile(T, 1024), _tile(N, tn_pref)
    return pl.pallas_call(
        _mm_plain_body,
        out_shape=jax.ShapeDtypeStruct((T, N), F32),
        grid=(N // tn, T // tm),
        in_specs=[pl.BlockSpec((tm, K), lambda j, i: (i, 0)),
                  pl.BlockSpec((tn, K), lambda j, i: (j, 0))],
        out_specs=pl.BlockSpec((tm, tn), lambda j, i: (i, j)),
        scratch_shapes=[pltpu.VMEM((K, tn), BF16)],
        compiler_params=_params("parallel", "arbitrary"),
        name="mm_plain",
    )(a, w_t)


def _merge_body(xn_ref, o0_ref, o1_ref, o2_ref, o3_ref, wg_ref, wu_ref, out_ref):
    xn = xn_ref[...]
    acc = None
    for br, o_ref in enumerate((o0_ref, o1_ref, o2_ref, o3_ref)):
        g = jnp.dot(xn, wg_ref[br], preferred_element_type=F32)
        u = jnp.dot(o_ref[...], wu_ref[br], preferred_element_type=F32)
        term = _sigmoid(g) * u
        acc = term if acc is None else acc + term
    out_ref[...] = acc.astype(BF16)


def merge_branches(xn, branches, w_gate_b, w_up_b, l):
    T, D = xn.shape
    W = branches[0].shape[1]
    tm, tn = _tile(T, 512), _tile(D, 256)
    bspec = pl.BlockSpec((tm, W), lambda i, j: (i, 0))
    return pl.pallas_call(
        _merge_body,
        out_shape=jax.ShapeDtypeStruct((T, D), BF16),
        grid=(T // tm, D // tn),
        in_specs=[pl.BlockSpec((tm, D), lambda i, j: (i, 0)), bspec, bspec, bspec, bspec,
                  pl.BlockSpec((None, N_BRANCH, D, tn), lambda i, j: (l, 0, 0, j)),
                  pl.BlockSpec((None, N_BRANCH, W, tn), lambda i, j: (l, 0, 0, j))],
        out_specs=pl.BlockSpec((tm, tn), lambda i, j: (i, j)),
        compiler_params=_params("parallel", "arbitrary"),
        name="merge_branches",
    )(xn, *branches, w_gate_b, w_up_b)


def _attn_prep_body(q_ref, k_ref, cos_ref, sin_ref, g_ref, qo_ref, ko_ref):
    cos = cos_ref[...]
    sin = sin_ref[...]
    lane = lax.broadcasted_iota(jnp.int32, cos.shape, 1)
    lo = lane < DIFF_DIM
    first = (lane % DIFF_DIM) < DIFF_DIM // 2
    n_heads = q_ref.shape[1] // HEAD_DIM
    for src, dst, gi, sc in ((q_ref, qo_ref, 0, DIFF_DIM ** -0.5), (k_ref, ko_ref, 1, 1.0)):
        g = g_ref[gi:gi + 1, :]
        for h in range(n_heads):
            cs = slice(h * HEAD_DIM, (h + 1) * HEAD_DIM)
            t = src[:, cs]
            ss = t * t
            s_lo = jnp.sum(jnp.where(lo, ss, 0.0), axis=-1, keepdims=True)
            s_hi = jnp.sum(jnp.where(lo, 0.0, ss), axis=-1, keepdims=True)
            inv = jnp.where(lo, lax.rsqrt(s_lo / DIFF_DIM + EPS), lax.rsqrt(s_hi / DIFF_DIM + EPS))
            y = t * inv * g
            partner = jnp.where(first, pltpu.roll(y, HEAD_DIM - DIFF_DIM // 2, 1), pltpu.roll(y, DIFF_DIM // 2, 1))
            dst[:, cs] = ((y * cos + partner * sin) * sc).astype(BF16)


def attn_prep(px, cos_t, sin_t, qk_g2, lay, W):
    T = px.shape[0]
    tr = lay.gr
    return pl.pallas_call(
        _attn_prep_body,
        out_shape=(jax.ShapeDtypeStruct((T, W), BF16), jax.ShapeDtypeStruct((T, W), BF16)),
        grid=(T // tr,),
        in_specs=[pl.BlockSpec((tr, W), lambda i: (i, 0)),
                  pl.BlockSpec((tr, W), lambda i: (i, 1)),
                  pl.BlockSpec((tr, LANES), lambda i: (i % lay.NG, 0)),
                  pl.BlockSpec((tr, LANES), lambda i: (i % lay.NG, 0)),
                  pl.BlockSpec((2, LANES), lambda i: (0, 0))],
        out_specs=(pl.BlockSpec((tr, W), lambda i: (i, 0)), pl.BlockSpec((tr, W), lambda i: (i, 0))),
        compiler_params=_params("parallel"),
        name="attn_prep",
    )(px, px, cos_t, sin_t, qk_g2)


def _attn_body(q_ref, k_ref, v_ref, lam_ref, g_ref, o_ref, vb, *, lay, lam_init):
    qi = pl.program_id(2)

    @pl.when(qi == 0)
    def _():
        vb[...] = v_ref[...].astype(BF16)

    lp = lam_ref[...]
    lam = (jnp.exp(jnp.sum(lp[0:1] * lp[1:2], axis=-1, keepdims=True))
           - jnp.exp(jnp.sum(lp[2:3] * lp[3:4], axis=-1, keepdims=True)) + lam_init)

    def run(nk):
        q = q_ref[...]
        k = k_ref[0:nk, :]
        v = vb[0:nk, :]
        lane = lax.broadcasted_iota(jnp.int32, q.shape, 1)
        zero = jnp.zeros_like(q)
        probs = []
        for qq in (jnp.where(lane < DIFF_DIM, q, zero), jnp.where(lane < DIFF_DIM, zero, q)):
            s = lax.dot_general(qq, k, _NT, preferred_element_type=F32)
            p = jnp.exp(s - jnp.max(s, axis=-1, keepdims=True))
            probs.append((p, jnp.sum(p, axis=-1, keepdims=True)))
        mix = probs[0][0] * (1.0 / probs[0][1]) - probs[1][0] * (lam / probs[1][1])
        o = jnp.dot(mix.astype(BF16), v, preferred_element_type=F32)
        y = o * lax.rsqrt(jnp.mean(o * o, axis=-1, keepdims=True) + EPS) * g_ref[...]
        o_ref[...] = (y * (1.0 - lam_init)).astype(BF16)

    @pl.when(qi < lay.NGc)
    def _():
        run(lay.Lc)

    @pl.when(qi >= lay.NGc)
    def _():
        run(lay.S)


def attention(q_rot, k_rot, px, lam_p, subln_g, lay, W, lam_init):
    T = px.shape[0]
    H = W // HEAD_DIM
    tq = lay.gr
    S = lay.S
    vb = W // HEAD_DIM * 2
    return pl.pallas_call(
        functools.partial(_attn_body, lay=lay, lam_init=lam_init),
        out_shape=jax.ShapeDtypeStruct((T, W), BF16),
        grid=(lay.B, H, lay.NG),
        in_specs=[pl.BlockSpec((tq, HEAD_DIM), lambda b, h, i: (b * lay.NG + i, h)),
                  pl.BlockSpec((S, HEAD_DIM), lambda b, h, i: (b, h)),
                  pl.BlockSpec((S, HEAD_DIM), lambda b, h, i: (b, vb + h)),
                  pl.BlockSpec((4, DIFF_DIM), lambda b, h, i: (0, 0)),
                  pl.BlockSpec((1, HEAD_DIM), lambda b, h, i: (0, 0))],
        out_specs=pl.BlockSpec((tq, HEAD_DIM), lambda b, h, i: (b * lay.NG + i, h)),
        scratch_shapes=[pltpu.VMEM((S, HEAD_DIM), BF16)],
        compiler_params=_params("parallel", "parallel", "arbitrary"),
        name="diff_attention",
    )(q_rot, k_rot, px, lam_p, subln_g.reshape(1, HEAD_DIM))


def _chunk_scan(y, reverse):
    S = y.shape[0]
    pos = lax.broadcasted_iota(jnp.int32, y.shape, 0) % CHUNK
    sh = 1
    while sh < CHUNK:
        if reverse:
            y = y + jnp.where(pos < CHUNK - sh, pltpu.roll(y, S - sh, 0), 0.0)
        else:
            y = y + jnp.where(pos >= sh, pltpu.roll(y, sh, 0), 0.0)
        sh *= 2
    return y


def _chunk_cumsum(y, reverse):
    S, C = y.shape
    nc = S // CHUNK
    ii = lax.broadcasted_iota(jnp.int32, (nc, CHUNK, CHUNK), 1)
    jj = lax.broadcasted_iota(jnp.int32, (nc, CHUNK, CHUNK), 2)
    tri = jnp.where((ii <= jj) if reverse else (ii >= jj), 1.0, 0.0).astype(BF16)
    y3 = y.reshape(nc, CHUNK, C)
    p1 = y3.astype(BF16)
    r1 = y3 - p1.astype(F32)
    p2 = r1.astype(BF16)
    p3 = (r1 - p2.astype(F32)).astype(BF16)
    out = _bmm(tri, p1) + _bmm(tri, p2) + _bmm(tri, p3)
    return out.reshape(S, C)


def _chunk_of_step(n, d, lay):
    ncc = lay.Lc // CHUNK
    nc = lay.S // CHUNK
    if d == 0:
        return n
    return jnp.where(n < ncc, ncc - 1 - n, nc - 1 - (n - ncc))


def _gated_head_norm(o, z, g):
    y = o * lax.rsqrt(jnp.mean(o * o, axis=-1, keepdims=True) + EPS) * g
    return (y * _silu(z)).astype(BF16)


def _centred_conv(x, w, prev_ok, next_ok):
    S = x.shape[0]
    xp = jnp.where(prev_ok, pltpu.roll(x, 1, 0), 0.0)
    xn = jnp.where(next_ok, pltpu.roll(x, S - 1, 0), 0.0)
    return xp * w[0:1] + x * w[1:2] + xn * w[2:3]


def _conv_masks(shape, lay):
    row = lax.broadcasted_iota(jnp.int32, shape, 0)
    prev_ok = (row * (row - lay.Lc)) != 0
    next_ok = ((row - (lay.Lc - 1)) * (row - (lay.S - 1))) != 0
    return prev_ok, next_ok


TRI_BLOCK = 8


def _bmm(x, y):
    return jnp.einsum('gij,gjk->gik', x, y, preferred_element_type=F32)


def _bmm_nt(x, y):
    return jnp.einsum('gik,gjk->gij', x, y, preferred_element_type=F32)


def _bmm_tn(x, y):
    return jnp.einsum('gjk,gjl->gkl', x, y, preferred_element_type=F32)


def _column_spread_matrix():
    m = jnp.arange(CHUNK)[:, None]
    c = jnp.arange(TRI_BLOCK * LANES)[None, :]
    j, lane = c // LANES, c % LANES
    hit = (lane < CHUNK) & (m % TRI_BLOCK == j) & (m // TRI_BLOCK == lane // TRI_BLOCK)
    return hit.astype(BF16)


def _unit_tri_inverse(a, spread, ii, jj, d):
    G = a.shape[0]
    nb = TRI_BLOCK
    a_d = jnp.where(ii // nb == jj // nb, a, 0.0).reshape(G * CHUNK, CHUNK)
    cols = jnp.dot(a_d.astype(BF16), spread, preferred_element_type=F32)
    row = lax.broadcasted_iota(jnp.int32, (G, CHUNK, LANES), 1)
    lane = lax.broadcasted_iota(jnp.int32, (G, CHUNK, LANES), 2)
    x = jnp.where(row == lane, 1.0, 0.0)
    for j in (range(nb - 1) if d == 0 else range(nb - 1, 0, -1)):
        col_j = cols[:, j * LANES:(j + 1) * LANES].reshape(G, CHUNK, LANES)
        x4 = x.reshape(G, CHUNK // nb, nb, LANES)
        row_j = jnp.broadcast_to(x4[:, :, j:j + 1, :], x4.shape).reshape(G, CHUNK, LANES)
        x = x - col_j * row_j
    t = x[:, :, :CHUNK]
    width = nb
    while width < CHUNK:
        e = jnp.where(((ii // width) ^ (jj // width)) == 1, a, 0.0).astype(BF16)
        tb = t.astype(BF16)
        t = t - _bmm(tb, _bmm(e, tb).astype(BF16))
        width *= 2
    return t


def _gdn_gates_body(ps_ref, alog_ref, dtb_ref, o_ref, *, n_heads):
    p = ps_ref[...]
    lane = lax.broadcasted_iota(jnp.int32, p.shape, 1)
    g_all = -jnp.exp(alog_ref[...]) * _softplus(p + dtb_ref[...])
    o_ref[...] = jnp.where(lane < n_heads, _chunk_scan(g_all, False),
                           jnp.where(lane < 2 * n_heads, _chunk_scan(g_all, True), _sigmoid(p)))


def gdn_gates(ps, alog_l, dtb_l, lay, H):
    T = ps.shape[0]
    S = lay.S
    one = pl.BlockSpec((1, LANES), lambda b: (0, 0))
    return pl.pallas_call(
        functools.partial(_gdn_gates_body, n_heads=H),
        out_shape=jax.ShapeDtypeStruct((T, LANES), F32),
        grid=(lay.B,),
        in_specs=[pl.BlockSpec((S, LANES), lambda b: (b, 0)), one, one],
        out_specs=pl.BlockSpec((S, LANES), lambda b: (b, 0)),
        compiler_params=_params("parallel"),
        name="gdn_gates",
    )(ps, alog_l, dtb_l)


def _gdn_body(q_ref, k_ref, v_ref, z_ref, ps_ref, cq_ref, ck_ref, cv_ref, g_ref, sel_ref, o_ref,
              qs, ks, vs, gc, bb, p_s, r_s, k_s, n_s, dl_s, o_s, *, lay, n_heads, group):
    h = pl.program_id(1)
    S = lay.S
    nc = S // CHUNK
    shape = (S, HEAD_DIM)
    prev_ok, next_ok = _conv_masks(shape, lay)
    lane = lax.broadcasted_iota(jnp.int32, shape, 1)

    q = _silu(_centred_conv(q_ref[...], cq_ref[...], prev_ok, next_ok))
    qs[...] = q * lax.rsqrt(jnp.sum(q * q, axis=-1, keepdims=True) + EPS) * HEAD_DIM ** -0.5
    k = _silu(_centred_conv(k_ref[...], ck_ref[...], prev_ok, next_ok))
    ks[...] = k * lax.rsqrt(jnp.sum(k * k, axis=-1, keepdims=True) + EPS)
    vs[...] = _silu(_centred_conv(v_ref[...], cv_ref[...], prev_ok, next_ok))

    p = ps_ref[...]
    for d in range(2):
        gsel = jnp.sum(jnp.where(lane == d * n_heads + h, p, 0.0), axis=-1, keepdims=True)
        gc[d] = jnp.broadcast_to(gsel, shape)
        bsel = jnp.sum(jnp.where(lane == (2 + d) * n_heads + h, p, 0.0), axis=-1, keepdims=True)
        bb[d] = jnp.broadcast_to(bsel, shape)

    G = group
    ii = lax.broadcasted_iota(jnp.int32, (G, CHUNK, CHUNK), 1)
    jj = lax.broadcasted_iota(jnp.int32, (G, CHUNK, CHUNK), 2)
    bmm, bmm_nt, bmm_tn = _bmm, _bmm_nt, _bmm_tn

    def prep(gi, d):
        rows = G * CHUNK
        sl = pl.ds(pl.multiple_of(gi * rows, rows), rows)
        cs = pl.ds(gi * G, G)
        r3 = lambda x: x.reshape(G, CHUNK, HEAD_DIM)
        q3, k3, v3, gcb, b3 = r3(qs[sl, :]), r3(ks[sl, :]), r3(vs[sl, :]), r3(gc[d, sl, :]), r3(bb[d, sl, :])
        glast = gcb[:, CHUNK - 1:CHUNK, :] if d == 0 else gcb[:, 0:1, :]
        eg = jnp.exp(gcb)
        kb = k3 * b3
        diff = gcb[:, :, :CHUNK] - jnp.swapaxes(gcb, 1, 2)[:, :CHUNK, :]
        incl = (ii >= jj) if d == 0 else (ii <= jj)
        strict = (ii > jj) if d == 0 else (ii < jj)
        decay = jnp.exp(jnp.where(incl, diff, MASK_NEG))
        kbf = k3.astype(BF16)
        a = jnp.where(strict, bmm_nt(kb.astype(BF16), kbf) * decay, 0.0)
        tb = _unit_tri_inverse(a, sel_ref[...], ii, jj, d).astype(BF16)
        ub = bmm(tb, (v3 * b3).astype(BF16)).astype(BF16)
        wb = bmm(tb, (kb * eg).astype(BF16)).astype(BF16)
        attb = (bmm_nt(q3.astype(BF16), kbf) * decay).astype(BF16)
        kgb = (k3 * jnp.exp(glast - gcb)).astype(BF16)
        p_s[d, sl, :] = (q3 * eg - bmm(attb, wb)).astype(BF16).reshape(rows, HEAD_DIM)
        r_s[d, sl, :] = bmm(attb, ub).reshape(rows, HEAD_DIM)
        k_s[d, cs, :, :] = bmm_tn(kgb, wb).astype(BF16)
        n_s[d, cs, :, :] = bmm_tn(kgb, ub)
        dl_s[d, cs, :, :] = jnp.broadcast_to(jnp.exp(glast), (G, 8, HEAD_DIM))

    def prep_loop(gi, carry):
        prep(gi, 0)
        prep(gi, 1)
        return carry

    lax.fori_loop(0, nc // G, prep_loop, 0)

    def step_d(d, n, s):
        c = _chunk_of_step(n, d, lay)
        sl = pl.ds(pl.multiple_of(c * CHUNK, CHUNK), CHUNK)
        sb = s.astype(BF16)
        o_s[d, sl, :] = r_s[d, sl, :] + jnp.dot(p_s[d, sl, :], sb, preferred_element_type=F32)
        dl = dl_s[d, pl.ds(c, 1), :, :][0][0:1, :]
        return (s * dl + n_s[d, pl.ds(c, 1), :, :][0]
                - jnp.dot(k_s[d, pl.ds(c, 1), :, :][0], sb, preferred_element_type=F32))

    def both(n, carry):
        return (step_d(0, n, carry[0]), step_d(1, n, carry[1]))

    zero = jnp.zeros((HEAD_DIM, HEAD_DIM), F32)
    lax.fori_loop(0, nc, both, (zero, zero), unroll=4 if nc % 4 == 0 else 1)
    o_ref[...] = _gated_head_norm(o_s[0] + o_s[1], z_ref[...], g_ref[...])


def gated_deltanet(px, gates, conv_w, norm_g, lay, W, col0):
    T = px.shape[0]
    H = W // HEAD_DIM
    S = lay.S
    nc = S // CHUNK

    def col(o):
        return pl.BlockSpec((S, HEAD_DIM), lambda b, h: (b, col0 + o * H + h))

    def cw(o):
        return pl.BlockSpec((3, HEAD_DIM), lambda b, h: (0, o * H + h))

    one = pl.BlockSpec((1, HEAD_DIM), lambda b, h: (0, 0))
    sh = (S, HEAD_DIM)
    st = (2, nc, HEAD_DIM, HEAD_DIM)
    group = max(g for g in range(1, 37) if nc % g == 0)
    return pl.pallas_call(
        functools.partial(_gdn_body, lay=lay, n_heads=H, group=group),
        out_shape=jax.ShapeDtypeStruct((T, W), BF16),
        grid=(lay.B, H),
        in_specs=[col(0), col(1), col(2), col(3),
                  pl.BlockSpec((S, LANES), lambda b, h: (b, 0)),
                  cw(0), cw(1), cw(2), one,
                  pl.BlockSpec((CHUNK, TRI_BLOCK * LANES), lambda b, h: (0, 0))],
        out_specs=pl.BlockSpec((S, HEAD_DIM), lambda b, h: (b, h)),
        scratch_shapes=[pltpu.VMEM(sh, F32), pltpu.VMEM(sh, F32), pltpu.VMEM(sh, F32),
                        pltpu.VMEM((2,) + sh, F32), pltpu.VMEM((2,) + sh, F32),
                        pltpu.VMEM((2,) + sh, BF16), pltpu.VMEM((2,) + sh, F32),
                        pltpu.VMEM(st, BF16), pltpu.VMEM(st, F32),
                        pltpu.VMEM((2, nc, 8, HEAD_DIM), F32), pltpu.VMEM((2,) + sh, F32)],
        compiler_params=_params("parallel", "arbitrary"),
        name="gated_deltanet",
    )(px, px, px, px, gates, conv_w, conv_w, conv_w, norm_g.reshape(1, HEAD_DIM), _column_spread_matrix())


def _hgrn_body(q_ref, f0_ref, f1_ref, i_ref, z_ref, lb0_ref, lb1_ref, g_ref, o_ref,
               gc, kk, o_s, *, lay, layer):
    S = lay.S
    nc = S // CHUNK
    for d, (f_ref, lb_ref) in enumerate(((f0_ref, lb0_ref), (f1_ref, lb1_ref))):
        lg = lb_ref[...]
        depth = lg.shape[0]
        mx = lg[0:1]
        for r in range(1, depth):
            mx = jnp.maximum(mx, lg[r:r + 1])
        e = [jnp.exp(lg[r:r + 1] - mx) for r in range(depth)]
        tot = e[0]
        for r in range(1, depth):
            tot = tot + e[r]
        lb = jnp.zeros_like(mx)
        for r in range(1, layer + 1):
            lb = lb + e[r] / tot
        f = f_ref[...]
        e_neg = jnp.exp(-jnp.abs(f))
        big = 1.0 / (1.0 + e_neg)
        small = e_neg * big
        pos = f >= 0.0
        logf = jnp.log(jnp.maximum(lb, LB_FLOOR) + (1.0 - lb) * jnp.where(pos, big, small))
        gc[d] = _chunk_cumsum(logf, reverse=(d == 1))
        kk[d] = (1.0 - lb) * jnp.where(pos, small, big)

    SB = 16
    nsb = CHUNK // SB
    jrow = lax.broadcasted_iota(jnp.int32, (CHUNK, HEAD_DIM), 0)
    jsub = lax.broadcasted_iota(jnp.int32, (SB, HEAD_DIM), 0)
    alane = lax.broadcasted_iota(jnp.int32, (SB, CHUNK), 1)

    def intra_t(d, qc, kc, g):
        if d == 0:
            refs = [g[0:1, :]] + [g[SB * I - 1:SB * I, :] for I in range(1, nsb)]
            far = range(1, nsb)
        else:
            refs = [g[SB * (I + 1):SB * (I + 1) + 1, :] for I in range(nsb - 1)] + [g[CHUNK - 1:CHUNK, :]]
            far = range(nsb - 1)
        rvec = jnp.concatenate([jnp.broadcast_to(r, (SB, HEAD_DIM)) for r in refs], axis=0)
        qt = qc * jnp.exp(g - rvec)
        off = None
        for I in far:
            seen = (jrow < SB * I) if d == 0 else (jrow >= SB * (I + 1))
            kt = (kc * jnp.exp(jnp.where(seen, refs[I] - g, MASK_NEG))).astype(BF16)
            qi = jnp.where((jrow // SB) == I, qt, 0.0).astype(BF16)
            part = lax.dot_general(kt, qi, _NT, preferred_element_type=F32)
            off = part if off is None else off + part
        blocks = []
        for I in range(nsb):
            rs = slice(SB * I, SB * (I + 1))
            g_i, k_i = g[rs, :], kc[rs, :]
            acc = jnp.zeros((SB, CHUNK), F32)
            for r in range(SB):
                i = SB * I + r
                keep = (jsub <= r) if d == 0 else (jsub >= r)
                rel = jnp.exp(jnp.where(keep, g[i:i + 1, :] - g_i, MASK_NEG))
                colv = jnp.sum(rel * k_i * qc[i:i + 1, :], axis=-1, keepdims=True)
                acc = jnp.where(alane == i, colv, acc)
            blocks.append(acc)
        return jnp.concatenate(blocks, axis=0) + off

    def step_d(d, n, st):
        c = _chunk_of_step(n, d, lay)
        sl = pl.ds(pl.multiple_of(c * CHUNK, CHUNK), CHUNK)
        qc, kc, vc, g = q_ref[sl, :], kk[d, sl, :], i_ref[sl, :], gc[d, sl, :]
        glast = g[CHUNK - 1:CHUNK, :] if d == 0 else g[0:1, :]
        a_t = intra_t(d, qc, kc, g)
        vb = vc.astype(BF16)
        o = (lax.dot_general((qc * jnp.exp(g)).astype(BF16), st.astype(BF16), _NT, preferred_element_type=F32)
             + lax.dot_general(a_t.astype(BF16), vb, _TN, preferred_element_type=F32))
        o_s[d, sl, :] = o
        kg = (kc * jnp.exp(glast - g)).astype(BF16)
        return st * jnp.exp(glast) + lax.dot_general(vb, kg, _TN, preferred_element_type=F32)

    def both(n, carry):
        return (step_d(0, n, carry[0]), step_d(1, n, carry[1]))

    zero = jnp.zeros((HEAD_DIM, HEAD_DIM), F32)
    lax.fori_loop(0, nc, both, (zero, zero), unroll=4 if nc % 4 == 0 else 1)
    o_ref[...] = _gated_head_norm(o_s[0] + o_s[1], z_ref[...], g_ref[...])


def hgrn2(px, lb_logits, norm_g, lay, W, col0, layer):
    T = px.shape[0]
    H = W // HEAD_DIM
    S = lay.S
    depth = lb_logits.shape[0]

    def col(o):
        return pl.BlockSpec((S, HEAD_DIM), lambda b, h: (b, col0 + o * H + h))

    def lbs(d):
        return pl.BlockSpec((depth, HEAD_DIM), lambda b, h: (0, d * H + h))

    sh = (S, HEAD_DIM)
    return pl.pallas_call(
        functools.partial(_hgrn_body, lay=lay, layer=layer),
        out_shape=jax.ShapeDtypeStruct((T, W), BF16),
        grid=(lay.B, H),
        in_specs=[col(0), col(1), col(2), col(3), col(4), lbs(0), lbs(1),
                  pl.BlockSpec((1, HEAD_DIM), lambda b, h: (0, 0))],
        out_specs=pl.BlockSpec((S, HEAD_DIM), lambda b, h: (b, h)),
        scratch_shapes=[pltpu.VMEM((2,) + sh, F32), pltpu.VMEM((2,) + sh, F32), pltpu.VMEM((2,) + sh, F32)],
        compiler_params=_params("parallel", "arbitrary"),
        name="hgrn2",
    )(px, px, px, px, px, lb_logits, lb_logits, norm_g.reshape(1, HEAD_DIM))


def _split_bf16(x):
    hi = x.astype(BF16)
    return hi, (x - hi.astype(F32)).astype(BF16)


def _dot3(a_hi, a_lo, b_hi, b_lo):
    return (jnp.dot(a_hi, b_hi, preferred_element_type=F32)
            + jnp.dot(a_hi, b_lo, preferred_element_type=F32)
            + jnp.dot(a_lo, b_hi, preferred_element_type=F32))


def _hy_filter_body(z_ref, w1_ref, b1_ref, w2_ref, b2_ref, w3_ref, fr_ref, dl_ref, o_ref, *, L):
    i = pl.program_id(0)
    z = z_ref[...]
    tr = z.shape[0]
    W = o_ref.shape[1]
    h = jnp.sin(fr_ref[0:1, :] * (jnp.dot(z, w1_ref[...], precision=HIGHEST, preferred_element_type=F32) + b1_ref[...]))
    h = jnp.sin(fr_ref[1:2, :] * (jnp.dot(h, w2_ref[...], precision=HIGHEST, preferred_element_type=F32) + b2_ref[...]))
    h = jnp.dot(h, w3_ref[...], precision=HIGHEST, preferred_element_type=F32)
    h = h * jnp.exp(-z[:, 0:1] * dl_ref[...])
    row = lax.broadcasted_iota(jnp.int32, (tr, W), 0) + i * tr
    o_ref[...] = jnp.where(row < L, h[:, :W], jnp.where(row == L, 0.0, h[:, W:]))


def hyena_filter(z2, w1p, b1, w2, b2, w3, freq, deltas2, L):
    n = z2.shape[0]
    W = w3.shape[1] // 2
    O = w2.shape[0]
    tr = _tile(n, 512)
    full = lambda a: pl.BlockSpec(a.shape, lambda i: (0,) * a.ndim)
    args = (w1p, b1.reshape(1, O), w2, b2.reshape(1, O), w3, freq, deltas2)
    return pl.pallas_call(
        functools.partial(_hy_filter_body, L=L),
        out_shape=jax.ShapeDtypeStruct((n, W), F32),
        grid=(n // tr,),
        in_specs=[pl.BlockSpec((tr, LANES), lambda i: (i, 0))] + [full(a) for a in args],
        out_specs=pl.BlockSpec((tr, W), lambda i: (i, 0)),
        compiler_params=_params("parallel"),
        name="hyena_filter",
    )(z2, *args)


def _hy_prep_body(v_ref, x0_ref, x1_ref, cv_ref, c0_ref, c1_ref, bv_ref, b0_ref, b1_ref,
                  vgx_ref, vge_ref, x0u_ref, *, lay):
    shape = v_ref.shape
    prev_ok, next_ok = _conv_masks(shape, lay)
    v = _centred_conv(v_ref[...], cv_ref[...], prev_ok, next_ok) + bv_ref[...]
    x0 = _centred_conv(x0_ref[...], c0_ref[...], prev_ok, next_ok) + b0_ref[...]
    x1 = _centred_conv(x1_ref[...], c1_ref[...], prev_ok, next_ok) + b1_ref[...]
    vg = v * x1
    vge_ref[...] = vg[:lay.Lc, :]
    vgx_ref[...] = vg[lay.Lc:, :]
    x0u_ref[...] = x0


def hyena_prep(px, conv_w, conv_b, lay, W, col0):
    T = px.shape[0]
    H = W // HEAD_DIM
    S, L, Lc, B = lay.S, lay.L, lay.Lc, lay.B

    def col(o):
        return pl.BlockSpec((S, HEAD_DIM), lambda b, j: (b, col0 + o * H + j))

    def cw(o, rows):
        return pl.BlockSpec((rows, HEAD_DIM), lambda b, j: (0, o * H + j))

    return pl.pallas_call(
        functools.partial(_hy_prep_body, lay=lay),
        out_shape=(jax.ShapeDtypeStruct((B * L, W), F32), jax.ShapeDtypeStruct((B * Lc, W), F32),
                   jax.ShapeDtypeStruct((T, W), F32)),
        grid=(B, H),
        in_specs=[col(0), col(1), col(2), cw(0, 3), cw(1, 3), cw(2, 3), cw(0, 1), cw(1, 1), cw(2, 1)],
        out_specs=(pl.BlockSpec((L, HEAD_DIM), lambda b, j: (b, j)),
                   pl.BlockSpec((Lc, HEAD_DIM), lambda b, j: (b, j)),
                   pl.BlockSpec((S, HEAD_DIM), lambda b, j: (b, j))),
        compiler_params=_params("parallel", "parallel"),
        name="hyena_prep",
    )(px, px, px, conv_w, conv_w, conv_w, conv_b, conv_b, conv_b)


def _dft_tables(L):
    n = 2 * L
    FT = min(256, L)
    sub = 64
    idx = jnp.arange(n, dtype=jnp.int32)
    r = idx % (2 * FT)
    is_im = r >= FT
    f = (idx // (2 * FT)) * FT + jnp.where(is_im, r - FT, r)
    nyq = is_im & (f == 0)
    f = jnp.where(nyq, L, f)
    step = jnp.arange(sub, dtype=jnp.int32)
    ang_hi = (2.0 * math.pi / n) * ((f[:, None] * (step * sub)[None, :]) % n).astype(F32)
    ang_lo = (2.0 * math.pi / n) * ((f[:, None] * step[None, :]) % n).astype(F32)
    ch, sh, cl, sl = jnp.cos(ang_hi), jnp.sin(ang_hi), jnp.cos(ang_lo), jnp.sin(ang_lo)
    wgt = jnp.where((f == 0) | (f == L), 1.0 / n, 2.0 / n)
    minus_sin = is_im & ~nyq

    def table(hi_steps, transpose):
        if transpose:
            c_h, s_h, c_l, s_l = (t.T for t in (ch, sh, cl, sl))
            cos = c_h[:hi_steps, None, :] * c_l[None, :, :] - s_h[:hi_steps, None, :] * s_l[None, :, :]
            sin = s_h[:hi_steps, None, :] * c_l[None, :, :] + c_h[:hi_steps, None, :] * s_l[None, :, :]
            out = jnp.where(minus_sin[None, None, :], -sin, cos) * wgt[None, None, :]
            return out.reshape(hi_steps * sub, n)
        cos = ch[:, :hi_steps, None] * cl[:, None, :] - sh[:, :hi_steps, None] * sl[:, None, :]
        sin = sh[:, :hi_steps, None] * cl[:, None, :] + ch[:, :hi_steps, None] * sl[:, None, :]
        return jnp.where(minus_sin[:, None, None], -sin, cos).reshape(n, hi_steps * sub)

    return table(n // sub, False), table(L // sub, False), table(L // sub, True), FT


def _hy_fwd_body(a_ref, u_ref, kf_ref, y_ref, *, FT):
    t = pl.program_id(2)
    x = jnp.dot(a_ref[...], u_ref[...].astype(BF16), preferred_element_type=F32)
    kf = kf_ref[...]
    xr, xi, kr, ki = x[:FT], x[FT:], kf[:FT], kf[FT:]
    row = lax.broadcasted_iota(jnp.int32, xr.shape, 0)
    packed = (row + t) == 0
    yr = jnp.where(packed, xr * kr, xr * kr - xi * ki)
    yi = jnp.where(packed, xi * ki, xr * ki + xi * kr)
    y_ref[...] = jnp.concatenate([yr, yi], axis=0).astype(BF16)


def _hy_spec_body(ah_ref, al_ref, u_ref, o_ref):
    uh, ul = _split_bf16(u_ref[...])
    o_ref[...] = _dot3(ah_ref[...], al_ref[...], uh, ul)


def hyena_spectrum(fh, fl, kern, FT):
    n, W = kern.shape
    tn = _tile(W, 256)
    return pl.pallas_call(
        _hy_spec_body,
        out_shape=jax.ShapeDtypeStruct((n, W), F32),
        grid=(W // tn, n // (2 * FT)),
        in_specs=[pl.BlockSpec((2 * FT, n), lambda j, t: (t, 0)),
                  pl.BlockSpec((2 * FT, n), lambda j, t: (t, 0)),
                  pl.BlockSpec((n, tn), lambda j, t: (0, j))],
        out_specs=pl.BlockSpec((2 * FT, tn), lambda j, t: (t, j)),
        compiler_params=_params("parallel", "arbitrary"),
        name="hyena_spectrum",
    )(fh, fl, kern)


def hyena_fwd(f_b, vg, kf, B, Lq, FT):
    W = vg.shape[1]
    n = 2 * Lq
    nt = n // (2 * FT)
    tn = _tile(W, 1024)
    return pl.pallas_call(
        functools.partial(_hy_fwd_body, FT=FT),
        out_shape=jax.ShapeDtypeStruct((B * n, W), BF16),
        grid=(B, W // tn, nt),
        in_specs=[pl.BlockSpec((2 * FT, Lq), lambda b, j, t: (t, 0)),
                  pl.BlockSpec((Lq, tn), lambda b, j, t: (b, j)),
                  pl.BlockSpec((2 * FT, tn), lambda b, j, t: (t, j))],
        out_specs=pl.BlockSpec((2 * FT, tn), lambda b, j, t: (b * nt + t, j)),
        compiler_params=_params("parallel", "parallel", "arbitrary"),
        name="hyena_fwd",
    )(f_b, vg, kf)


def _hy_inv_body(g_ref, y_ref, vg_ref, x0_ref, bias_ref, o_ref):
    y = jnp.dot(g_ref[...], y_ref[...], preferred_element_type=F32)
    o_ref[...] = (x0_ref[...] * (y + vg_ref[...] * bias_ref[...])).astype(BF16)


def hyena_inv(g_b, y_b, vg, x0u, bias, lay, Lq, row0):
    W = vg.shape[1]
    n = 2 * Lq
    B = lay.B
    tt = lay.gr
    ntt = Lq // tt
    tn = _tile(W, 1024)
    g0 = row0 // tt
    return pl.pallas_call(
        _hy_inv_body,
        out_shape=jax.ShapeDtypeStruct((B * Lq, W), BF16),
        grid=(B, W // tn, ntt),
        in_specs=[pl.BlockSpec((tt, n), lambda b, j, i: (i, 0)),
                  pl.BlockSpec((n, tn), lambda b, j, i: (b, j)),
                  pl.BlockSpec((tt, tn), lambda b, j, i: (b * ntt + i, j)),
                  pl.BlockSpec((tt, tn), lambda b, j, i: (b * lay.NG + g0 + i, j)),
                  pl.BlockSpec((1, tn), lambda b, j, i: (0, j))],
        out_specs=pl.BlockSpec((tt, tn), lambda b, j, i: (b * ntt + i, j)),
        compiler_params=_params("parallel", "parallel", "arbitrary"),
        name="hyena_inv",
    )(g_b, y_b, vg, x0u, bias.reshape(1, W))


def _hyena_features(L):
    bands = (HY_EMB - 1) // 2
    t = jnp.linspace(0.0, 1.0, L, dtype=F32)[:, None]
    wpos = 2.0 * math.pi * jnp.arange(L, dtype=F32)[:, None] / L
    fb = jnp.linspace(1e-4, bands - 1, bands, dtype=F32)[None]
    z = jnp.concatenate([t, jnp.cos(fb * wpos), -jnp.sin(fb * wpos)], axis=-1)
    z2 = jnp.concatenate([z, z[::-1]], axis=0)
    return jnp.pad(z2, ((0, 0), (0, LANES - HY_EMB)))


def hyena_stream(vg, x0u, bias, wts, lay, Lq, row0):
    w1p, b1, w2, b2, w3, freq, deltas2 = wts
    fwd_full, fwd_half, inv, FT = _dft_tables(Lq)
    kh, kl = _split_bf16(fwd_full)
    kern = hyena_filter(_hyena_features(Lq), w1p, b1, w2, b2, w3, freq, deltas2, Lq)
    kf = hyena_spectrum(kh, kl, kern, FT)
    y_b = hyena_fwd(fwd_half.astype(BF16), vg, kf, lay.B, Lq, FT)
    return hyena_inv(inv.astype(BF16), y_b, vg, x0u, bias, lay, Lq, row0)


def _rope_tables(lay):
    n_freq = DIFF_DIM // 4
    inv = ROPE_THETA ** (-jnp.arange(n_freq, dtype=F32) / n_freq)
    rows = lay.L // GRID_W
    r = jnp.repeat(jnp.arange(rows, dtype=F32), GRID_W)
    col = jnp.tile(jnp.arange(GRID_W, dtype=F32), rows)
    ang = jnp.concatenate([r[:, None] * inv, col[:, None] * inv], axis=-1)
    cos, sin = jnp.cos(ang), jnp.sin(ang)
    cos = jnp.concatenate([jnp.ones((lay.Lc, DIFF_DIM // 2), F32), cos], axis=0)
    sin = jnp.concatenate([jnp.zeros((lay.Lc, DIFF_DIM // 2), F32), sin], axis=0)
    cos_t = jnp.tile(cos, (1, 4))
    sin_t = jnp.tile(jnp.concatenate([-sin, sin], axis=-1), (1, 2))
    return cos_t, sin_t


def _main_cols(W, H):
    small0 = 7 * W
    small1 = small0 + 4 * H
    return small0, small1


def mixers(px1, px2, ps, lay, W, layer, need_ctx, p):
    H = W // HEAD_DIM
    lam_init = 0.8 - 0.6 * math.exp(-0.3 * layer)
    q_rot, k_rot = attn_prep(px1, p["cos_t"], p["sin_t"], p["qk_g2"], lay, W)
    att = attention(q_rot, k_rot, px1, p["attn_lambda"], p["attn_subln_g"], lay, W, lam_init)
    gates = gdn_gates(ps, p["alog"], p["dtb"], lay, H)
    gdn = gated_deltanet(px1, gates, p["gdn_conv_w"], p["gdn_norm_g"], lay, W, 3 * H)
    hg = hgrn2(px2, p["hg_lb_logits"], p["hg_norm_g"], lay, W, 0, layer)
    vgx, vge, x0u = hyena_prep(px2, p["hy_conv_w"], p["hy_conv_b"], lay, W, 5 * H)
    hy_x = hyena_stream(vgx, x0u, p["hy_bias"], p["hy_wts"], lay, lay.L, lay.Lc)
    if need_ctx:
        hy_e = hyena_stream(vge, x0u, p["hy_bias"], p["hy_wts"], lay, lay.Lc, 0)
    else:
        hy_e = jnp.zeros((lay.B, lay.Lc, W), BF16)
    hy = jnp.concatenate([hy_e.reshape(lay.B, lay.Lc, W), hy_x.reshape(lay.B, lay.L, W)], axis=1)
    return att, gdn, hg, hy.reshape(lay.T, W)


def _layer_params(l, lay, W, attn_qk_g, attn_lambda, attn_subln_g, gdn_conv_w, gdn_a_log, gdn_dt_bias, gdn_norm_g,
                  hg_lb_logits, hg_norm_g, hy_conv_w, hy_conv_b, hy_w1, hy_b1, hy_w2, hy_b2, hy_w3, hy_freq,
                  hy_bias, cos_t, sin_t):
    H = W // HEAD_DIM
    pad16 = lambda a: jnp.pad(a.astype(F32).reshape(1, 2 * H), ((0, 0), (0, LANES - 2 * H)))
    deltas = jnp.abs(jnp.linspace(math.log(HY_TARGET) / HY_SLOW_DECAY, math.log(HY_TARGET) / HY_FAST_DECAY, W, dtype=F32))
    return dict(
        cos_t=cos_t, sin_t=sin_t,
        qk_g2=jnp.tile(attn_qk_g[l].astype(F32), (1, 2)),
        attn_lambda=attn_lambda[l].astype(F32), attn_subln_g=attn_subln_g[l].astype(F32),
        gdn_conv_w=gdn_conv_w[l], alog=pad16(gdn_a_log[l]), dtb=pad16(gdn_dt_bias[l]), gdn_norm_g=gdn_norm_g[l],
        hg_lb_logits=hg_lb_logits.astype(F32).reshape(hg_lb_logits.shape[0], 2 * W), hg_norm_g=hg_norm_g[l],
        hy_conv_w=hy_conv_w[l], hy_conv_b=hy_conv_b[l].reshape(1, 3 * W), hy_bias=hy_bias[l].astype(F32),
        hy_wts=(jnp.pad(hy_w1[l].astype(F32), ((0, LANES - HY_EMB), (0, 0))), hy_b1[l].astype(F32),
                hy_w2[l].astype(F32), hy_b2[l].astype(F32), hy_w3[l].astype(F32), hy_freq[l].astype(F32),
                jnp.tile(deltas, 2).reshape(1, 2 * W)),
    )


def kernel(x, c, ctx, c_ctx, norm_g, w_mod, b_mod, ffn_w_in, ffn_w_out, w_in, attn_qk_g, attn_lambda, attn_subln_g, gdn_conv_w, gdn_a_log, gdn_dt_bias, gdn_norm_g, hg_lb_logits, hg_norm_g, hy_conv_w, hy_conv_b, hy_w1, hy_b1, hy_w2, hy_b2, hy_w3, hy_freq, hy_bias, w_gate, w_up, w_out):
    B, L, D = x.shape
    Lc = ctx.shape[1]
    depth = w_mod.shape[0]
    W = D // N_BRANCH
    H = W // HEAD_DIM
    lay = Layout(B, L, Lc)
    h = jnp.concatenate([ctx, x], axis=1).reshape(lay.T, D)
    G = -(-(B + 1) // 8) * 8
    c_all = jnp.concatenate([c, c_ctx[None], jnp.zeros((G - B - 1, D), F32)], axis=0)
    cos_t, sin_t = _rope_tables(lay)
    small0, small1 = _main_cols(W, H)
    w_gate_b, w_up_b = w_gate.astype(BF16), w_up.astype(BF16)
    for l in range(depth):
        last = l == depth - 1
        p = _layer_params(l, lay, W, attn_qk_g, attn_lambda, attn_subln_g, gdn_conv_w, gdn_a_log, gdn_dt_bias,
                          gdn_norm_g, hg_lb_logits, hg_norm_g, hy_conv_w, hy_conv_b, hy_w1, hy_b1, hy_w2, hy_b2,
                          hy_w3, hy_freq, hy_bias, cos_t, sin_t)
        mod = mod_table(c_all, w_mod, b_mod, l).reshape(G, MOD_CHUNKS, D)
        act = mm_swiglu(norm_mod(h, norm_g[l, 0], mod, lay, 0), ffn_w_in, (l, 0))
        h = mm_resid(act, ffn_w_out, (l, 0), h, mod, lay, 2, 0.5)
        xn = norm_mod(h, norm_g[l, 1], mod, lay, 1)
        cols_t = lambda lo, hi: w_in[l, :, lo:hi].T
        px1 = mm_plain(xn, cols_t(0, small0))
        px2 = mm_plain(xn, cols_t(small1, w_in.shape[-1]))
        ps = mm_plain(xn, jnp.pad(cols_t(small0, small1), ((0, LANES - (small1 - small0)), (0, 0))), tn_pref=LANES)
        branches = mixers(px1, px2, ps, lay, W, l, not last, p)
        if last:
            latent = lambda a: a.reshape(B, lay.S, a.shape[-1])[:, Lc:, :].reshape(B * L, a.shape[-1])
            xn, h, branches = latent(xn), latent(h), tuple(latent(a) for a in branches)
            lay = Layout(B, L, 0, gr=lay.gr)
        acc = merge_branches(xn, branches, w_gate_b, w_up_b, l)
        h = mm_resid(acc, w_out, (l,), h, mod, lay, 5, 1.0)
        act = mm_swiglu(norm_mod(h, norm_g[l, 2], mod, lay, 2), ffn_w_in, (l, 1))
        h = mm_resid(act, ffn_w_out, (l, 1), h, mod, lay, 8, 0.5)
    return h.reshape(B, L, D)
```

```python
import functools
import math

import jax
import jax.numpy as jnp
from jax import lax
from jax.experimental import pallas as pl
from jax.experimental.pallas import tpu as pltpu

F32 = jnp.float32
BF16 = jnp.bfloat16

GRID_W = 64
N_BRANCH = 4
HEAD_DIM = 128
DIFF_DIM = HEAD_DIM // 2
MOD_CHUNKS = 9
ROPE_THETA = 10000.0
CHUNK = 64
HY_EMB = 33
HY_FAST_DECAY = 0.3
HY_SLOW_DECAY = 1.5
HY_TARGET = 1e-2
EPS = 1e-6
MASK_NEG = -1e30
LB_FLOOR = 1e-20

LANES = 128
V7X_VMEM_LIMIT = 56 * 1024 * 1024
HIGHEST = lax.Precision.HIGHEST

_NT = (((1,), (1,)), ((), ()))
_TN = (((0,), (0,)), ((), ()))


def _params(*sem):
    return pltpu.CompilerParams(dimension_semantics=sem, vmem_limit_bytes=V7X_VMEM_LIMIT)


def _tile(n, pref):
    t = min(n, pref)
    while n % t:
        t //= 2
    return t


def _wspec(at, K, tn, col):
    return pl.BlockSpec((None,) * len(at) + (K, tn), lambda *ids: tuple(at) + (0, col(*ids)))


def _sigmoid(x):
    return 1.0 / (1.0 + jnp.exp(-x))


def _silu(x):
    return x * _sigmoid(x)


def _softplus(x):
    return jnp.maximum(x, 0.0) + jnp.log1p(jnp.exp(-jnp.abs(x)))


class Layout:
    def __init__(self, B, L, Lc, gr=None):
        self.B, self.L, self.Lc = B, L, Lc
        self.S = L + Lc
        self.T = B * self.S
        self.gr = gr or math.gcd(L, Lc)
        self.NG = self.S // self.gr
        self.NGc = Lc // self.gr


def _mod_row(mod_ref, lay, r, k):
    b = r // lay.NG
    g = jnp.where(r - b * lay.NG < lay.NGc, lay.B, b)
    return mod_ref[pl.ds(g, 1), k:k + 1, :][0]


def _mod_body(c_ref, w_ref, b_ref, o_ref):
    a = _silu(c_ref[...]).astype(BF16)
    o_ref[...] = jnp.dot(a, w_ref[...].astype(BF16), preferred_element_type=F32) + b_ref[...]


def mod_table(c16, w_mod, b_mod, l):
    R, D = c16.shape
    N = w_mod.shape[-1]
    tn = _tile(N, 512)
    return pl.pallas_call(
        _mod_body,
        out_shape=jax.ShapeDtypeStruct((R, N), F32),
        grid=(N // tn,),
        in_specs=[pl.BlockSpec((R, D), lambda j: (0, 0)),
                  _wspec((l,), D, tn, lambda j: j),
                  _wspec((l,), 1, tn, lambda j: j)],
        out_specs=pl.BlockSpec((R, tn), lambda j: (0, j)),
        compiler_params=_params("parallel"),
        name="mod_table",
    )(c16, w_mod, b_mod.reshape(b_mod.shape[0], 1, N))


def _norm_mod_body(x_ref, g_ref, mod_ref, o_ref, *, lay, k):
    i = pl.program_id(0)
    gr = lay.gr
    n_sub = x_ref.shape[0] // gr
    for s in range(n_sub):
        x = x_ref[s * gr:(s + 1) * gr, :]
        y = x * lax.rsqrt(jnp.mean(x * x, axis=-1, keepdims=True) + EPS) * g_ref[...]
        r = i * n_sub + s
        shift = _mod_row(mod_ref, lay, r, 3 * k)
        scale = _mod_row(mod_ref, lay, r, 3 * k + 1)
        o_ref[s * gr:(s + 1) * gr, :] = (y * (1.0 + scale) + shift).astype(BF16)


def norm_mod(h, g, mod, lay, k):
    T, D = h.shape
    tm = 2 * lay.gr if T % (2 * lay.gr) == 0 else lay.gr
    G = mod.shape[0]
    return pl.pallas_call(
        functools.partial(_norm_mod_body, lay=lay, k=k),
        out_shape=jax.ShapeDtypeStruct((T, D), BF16),
        grid=(T // tm,),
        in_specs=[pl.BlockSpec((tm, D), lambda i: (i, 0)),
                  pl.BlockSpec((1, D), lambda i: (0, 0)),
                  pl.BlockSpec((G, MOD_CHUNKS, D), lambda i: (0, 0, 0))],
        out_specs=pl.BlockSpec((tm, D), lambda i: (i, 0)),
        compiler_params=_params("parallel"),
        name="norm_mod",
    )(h, g.reshape(1, D), mod)


def _mm_swiglu_body(a_ref, wg_ref, wu_ref, o_ref, wgb, wub):
    @pl.when(pl.program_id(1) == 0)
    def _():
        wgb[...] = wg_ref[...].astype(BF16)
        wub[...] = wu_ref[...].astype(BF16)

    a = a_ref[...]
    g = jnp.dot(a, wgb[...], preferred_element_type=F32)
    u = jnp.dot(a, wub[...], preferred_element_type=F32)
    o_ref[...] = (_silu(g) * u).astype(BF16)


def mm_swiglu(a, w, at):
    T, K = a.shape
    F = w.shape[-1] // 2
    tm, tn = _tile(T, 1024), _tile(F, 256)
    nj = F // tn
    return pl.pallas_call(
        _mm_swiglu_body,
        out_shape=jax.ShapeDtypeStruct((T, F), BF16),
        grid=(nj, T // tm),
        in_specs=[pl.BlockSpec((tm, K), lambda j, i: (i, 0)),
                  _wspec(at, K, tn, lambda j, i: j),
                  _wspec(at, K, tn, lambda j, i: j + nj)],
        out_specs=pl.BlockSpec((tm, tn), lambda j, i: (i, j)),
        scratch_shapes=[pltpu.VMEM((K, tn), BF16), pltpu.VMEM((K, tn), BF16)],
        compiler_params=_params("parallel", "arbitrary"),
        name="mm_swiglu",
    )(a, w, w)


def _mm_resid_body(a_ref, w_ref, h_ref, mod_ref, o_ref, wb, *, lay, k, scale):
    i = pl.program_id(1)

    @pl.when(i == 0)
    def _():
        wb[...] = w_ref[...].astype(BF16)

    y = jnp.dot(a_ref[...], wb[...], preferred_element_type=F32)
    gr = lay.gr
    n_sub = a_ref.shape[0] // gr
    for s in range(n_sub):
        gate = _mod_row(mod_ref, lay, i * n_sub + s, k)
        sl = slice(s * gr, (s + 1) * gr)
        o_ref[sl, :] = h_ref[sl, :] + (scale * gate) * y[sl, :]


def mm_resid(a, w, at, h, mod, lay, k, scale):
    T, K = a.shape
    N = w.shape[-1]
    G = mod.shape[0]
    def vmem_bytes(tm, tn):
        return 2 * tm * K * 2 + 2 * K * tn * 4 + K * tn * 2 + 4 * tm * tn * 4

    tm, tn = max(_tile(T, 1024), lay.gr), _tile(N, 512)
    if vmem_bytes(tm, tn) > (V7X_VMEM_LIMIT * 7) // 8:
        tm = max(_tile(T, 512), lay.gr)
    return pl.pallas_call(
        functools.partial(_mm_resid_body, lay=lay, k=k, scale=scale),
        out_shape=jax.ShapeDtypeStruct((T, N), F32),
        grid=(N // tn, T // tm),
        in_specs=[pl.BlockSpec((tm, K), lambda j, i: (i, 0)),
                  _wspec(at, K, tn, lambda j, i: j),
                  pl.BlockSpec((tm, tn), lambda j, i: (i, j)),
                  pl.BlockSpec((G, MOD_CHUNKS, tn), lambda j, i: (0, 0, j))],
        out_specs=pl.BlockSpec((tm, tn), lambda j, i: (i, j)),
        scratch_shapes=[pltpu.VMEM((K, tn), BF16)],
        compiler_params=_params("parallel", "arbitrary"),
        name="mm_resid",
    )(a, w, h, mod)


def _mm_plain_body(a_ref, wt_ref, o_ref, wb):
    @pl.when(pl.program_id(1) == 0)
    def _():
        wb[...] = wt_ref[...].T.astype(BF16)

    o_ref[...] = jnp.dot(a_ref[...], wb[...], preferred_element_type=F32)


def mm_plain(a, w_t, tn_pref=512):
    T, K = a.shape
    N = w_t.shape[0]
    tm, tn = _tile(T, 1024), _tile(N, tn_pref)
    return pl.pallas_call(
        _mm_plain_body,
        out_shape=jax.ShapeDtypeStruct((T, N), F32),
        grid=(N // tn, T // tm),
        in_specs=[pl.BlockSpec((tm, K), lambda j, i: (i, 0)),
                  pl.BlockSpec((tn, K), lambda j, i: (j, 0))],
        out_specs=pl.BlockSpec((tm, tn), lambda j, i: (i, j)),
        scratch_shapes=[pltpu.VMEM((K, tn), BF16)],
        compiler_params=_params("parallel", "arbitrary"),
        name="mm_plain",
    )(a, w_t)


def _merge_body(xn_ref, o0_ref, o1_ref, o2_ref, o3_ref, wg_ref, wu_ref, out_ref):
    xn = xn_ref[...]
    acc = None
    for br, o_ref in enumerate((o0_ref, o1_ref, o2_ref, o3_ref)):
        g = jnp.dot(xn, wg_ref[br], preferred_element_type=F32)
        u = jnp.dot(o_ref[...], wu_ref[br], preferred_element_type=F32)
        term = _sigmoid(g) * u
        acc = term if acc is None else acc + term
    out_ref[...] = acc.astype(BF16)


def merge_branches(xn, branches, w_gate_b, w_up_b, l):
    T, D = xn.shape
    W = branches[0].shape[1]
    tm, tn = _tile(T, 512), _tile(D, 256)
    bspec = pl.BlockSpec((tm, W), lambda i, j: (i, 0))
    return pl.pallas_call(
        _merge_body,
        out_shape=jax.ShapeDtypeStruct((T, D), BF16),
        grid=(T // tm, D // tn),
        in_specs=[pl.BlockSpec((tm, D), lambda i, j: (i, 0)), bspec, bspec, bspec, bspec,
                  pl.BlockSpec((None, N_BRANCH, D, tn), lambda i, j: (l, 0, 0, j)),
                  pl.BlockSpec((None, N_BRANCH, W, tn), lambda i, j: (l, 0, 0, j))],
        out_specs=pl.BlockSpec((tm, tn), lambda i, j: (i, j)),
        compiler_params=_params("parallel", "arbitrary"),
        name="merge_branches",
    )(xn, *branches, w_gate_b, w_up_b)


def _attn_prep_body(q_ref, k_ref, cos_ref, sin_ref, g_ref, qo_ref, ko_ref):
    cos = cos_ref[...]
    sin = sin_ref[...]
    lane = lax.broadcasted_iota(jnp.int32, cos.shape, 1)
    lo = lane < DIFF_DIM
    first = (lane % DIFF_DIM) < DIFF_DIM // 2
    n_heads = q_ref.shape[1] // HEAD_DIM
    for src, dst, gi, sc in ((q_ref, qo_ref, 0, DIFF_DIM ** -0.5), (k_ref, ko_ref, 1, 1.0)):
        g = g_ref[gi:gi + 1, :]
        for h in range(n_heads):
            cs = slice(h * HEAD_DIM, (h + 1) * HEAD_DIM)
            t = src[:, cs]
            ss = t * t
            s_lo = jnp.sum(jnp.where(lo, ss, 0.0), axis=-1, keepdims=True)
            s_hi = jnp.sum(jnp.where(lo, 0.0, ss), axis=-1, keepdims=True)
            inv = jnp.where(lo, lax.rsqrt(s_lo / DIFF_DIM + EPS), lax.rsqrt(s_hi / DIFF_DIM + EPS))
            y = t * inv * g
            partner = jnp.where(first, pltpu.roll(y, HEAD_DIM - DIFF_DIM // 2, 1), pltpu.roll(y, DIFF_DIM // 2, 1))
            dst[:, cs] = ((y * cos + partner * sin) * sc).astype(BF16)


def attn_prep(px, cos_t, sin_t, qk_g2, lay, W):
    T = px.shape[0]
    tr = lay.gr
    return pl.pallas_call(
        _attn_prep_body,
        out_shape=(jax.ShapeDtypeStruct((T, W), BF16), jax.ShapeDtypeStruct((T, W), BF16)),
        grid=(T // tr,),
        in_specs=[pl.BlockSpec((tr, W), lambda i: (i, 0)),
                  pl.BlockSpec((tr, W), lambda i: (i, 1)),
                  pl.BlockSpec((tr, LANES), lambda i: (i % lay.NG, 0)),
                  pl.BlockSpec((tr, LANES), lambda i: (i % lay.NG, 0)),
                  pl.BlockSpec((2, LANES), lambda i: (0, 0))],
        out_specs=(pl.BlockSpec((tr, W), lambda i: (i, 0)), pl.BlockSpec((tr, W), lambda i: (i, 0))),
        compiler_params=_params("parallel"),
        name="attn_prep",
    )(px, px, cos_t, sin_t, qk_g2)


def _attn_body(q_ref, k_ref, v_ref, lam_ref, g_ref, o_ref, vb, *, lay, lam_init):
    qi = pl.program_id(2)

    @pl.when(qi == 0)
    def _():
        vb[...] = v_ref[...].astype(BF16)

    lp = lam_ref[...]
    lam = (jnp.exp(jnp.sum(lp[0:1] * lp[1:2], axis=-1, keepdims=True))
           - jnp.exp(jnp.sum(lp[2:3] * lp[3:4], axis=-1, keepdims=True)) + lam_init)

    def run(nk):
        q = q_ref[...]
        k = k_ref[0:nk, :]
        v = vb[0:nk, :]
        lane = lax.broadcasted_iota(jnp.int32, q.shape, 1)
        zero = jnp.zeros_like(q)
        probs = []
        for qq in (jnp.where(lane < DIFF_DIM, q, zero), jnp.where(lane < DIFF_DIM, zero, q)):
            s = lax.dot_general(qq, k, _NT, preferred_element_type=F32)
            p = jnp.exp(s - jnp.max(s, axis=-1, keepdims=True))
            probs.append((p, jnp.sum(p, axis=-1, keepdims=True)))
        mix = probs[0][0] * (1.0 / probs[0][1]) - probs[1][0] * (lam / probs[1][1])
        o = jnp.dot(mix.astype(BF16), v, preferred_element_type=F32)
        y = o * lax.rsqrt(jnp.mean(o * o, axis=-1, keepdims=True) + EPS) * g_ref[...]
        o_ref[...] = (y * (1.0 - lam_init)).astype(BF16)

    @pl.when(qi < lay.NGc)
    def _():
        run(lay.Lc)

    @pl.when(qi >= lay.NGc)
    def _():
        run(lay.S)


def attention(q_rot, k_rot, px, lam_p, subln_g, lay, W, lam_init):
    T = px.shape[0]
    H = W // HEAD_DIM
    tq = lay.gr
    S = lay.S
    vb = W // HEAD_DIM * 2
    return pl.pallas_call(
        functools.partial(_attn_body, lay=lay, lam_init=lam_init),
        out_shape=jax.ShapeDtypeStruct((T, W), BF16),
        grid=(lay.B, H, lay.NG),
        in_specs=[pl.BlockSpec((tq, HEAD_DIM), lambda b, h, i: (b * lay.NG + i, h)),
                  pl.BlockSpec((S, HEAD_DIM), lambda b, h, i: (b, h)),
                  pl.BlockSpec((S, HEAD_DIM), lambda b, h, i: (b, vb + h)),
                  pl.BlockSpec((4, DIFF_DIM), lambda b, h, i: (0, 0)),
                  pl.BlockSpec((1, HEAD_DIM), lambda b, h, i: (0, 0))],
        out_specs=pl.BlockSpec((tq, HEAD_DIM), lambda b, h, i: (b * lay.NG + i, h)),
        scratch_shapes=[pltpu.VMEM((S, HEAD_DIM), BF16)],
        compiler_params=_params("parallel", "parallel", "arbitrary"),
        name="diff_attention",
    )(q_rot, k_rot, px, lam_p, subln_g.reshape(1, HEAD_DIM))


def _chunk_scan(y, reverse):
    S = y.shape[0]
    pos = lax.broadcasted_iota(jnp.int32, y.shape, 0) % CHUNK
    sh = 1
    while sh < CHUNK:
        if reverse:
            y = y + jnp.where(pos < CHUNK - sh, pltpu.roll(y, S - sh, 0), 0.0)
        else:
            y = y + jnp.where(pos >= sh, pltpu.roll(y, sh, 0), 0.0)
        sh *= 2
    return y


def _chunk_cumsum(y, reverse):
    S, C = y.shape
    nc = S // CHUNK
    ii = lax.broadcasted_iota(jnp.int32, (nc, CHUNK, CHUNK), 1)
    jj = lax.broadcasted_iota(jnp.int32, (nc, CHUNK, CHUNK), 2)
    tri = jnp.where((ii <= jj) if reverse else (ii >= jj), 1.0, 0.0).astype(BF16)
    y3 = y.reshape(nc, CHUNK, C)
    p1 = y3.astype(BF16)
    r1 = y3 - p1.astype(F32)
    p2 = r1.astype(BF16)
    p3 = (r1 - p2.astype(F32)).astype(BF16)
    out = _bmm(tri, p1) + _bmm(tri, p2) + _bmm(tri, p3)
    return out.reshape(S, C)


def _chunk_of_step(n, d, lay):
    ncc = lay.Lc // CHUNK
    nc = lay.S // CHUNK
    if d == 0:
        return n
    return jnp.where(n < ncc, ncc - 1 - n, nc - 1 - (n - ncc))


def _gated_head_norm(o, z, g):
    y = o * lax.rsqrt(jnp.mean(o * o, axis=-1, keepdims=True) + EPS) * g
    return (y * _silu(z)).astype(BF16)


def _centred_conv(x, w, prev_ok, next_ok):
    S = x.shape[0]
    xp = jnp.where(prev_ok, pltpu.roll(x, 1, 0), 0.0)
    xn = jnp.where(next_ok, pltpu.roll(x, S - 1, 0), 0.0)
    return xp * w[0:1] + x * w[1:2] + xn * w[2:3]


def _conv_masks(shape, lay):
    row = lax.broadcasted_iota(jnp.int32, shape, 0)
    prev_ok = (row * (row - lay.Lc)) != 0
    next_ok = ((row - (lay.Lc - 1)) * (row - (lay.S - 1))) != 0
    return prev_ok, next_ok


TRI_BLOCK = 8


def _bmm(x, y):
    return jnp.einsum('gij,gjk->gik', x, y, preferred_element_type=F32)


def _bmm_nt(x, y):
    return jnp.einsum('gik,gjk->gij', x, y, preferred_element_type=F32)


def _bmm_tn(x, y):
    return jnp.einsum('gjk,gjl->gkl', x, y, preferred_element_type=F32)


def _column_spread_matrix():
    m = jnp.arange(CHUNK)[:, None]
    c = jnp.arange(TRI_BLOCK * LANES)[None, :]
    j, lane = c // LANES, c % LANES
    hit = (lane < CHUNK) & (m % TRI_BLOCK == j) & (m // TRI_BLOCK == lane // TRI_BLOCK)
    return hit.astype(BF16)


def _unit_tri_inverse(a, spread, ii, jj, d):
    G = a.shape[0]
    nb = TRI_BLOCK
    a_d = jnp.where(ii // nb == jj // nb, a, 0.0).reshape(G * CHUNK, CHUNK)
    cols = jnp.dot(a_d.astype(BF16), spread, preferred_element_type=F32)
    row = lax.broadcasted_iota(jnp.int32, (G, CHUNK, LANES), 1)
    lane = lax.broadcasted_iota(jnp.int32, (G, CHUNK, LANES), 2)
    x = jnp.where(row == lane, 1.0, 0.0)
    for j in (range(nb - 1) if d == 0 else range(nb - 1, 0, -1)):
        col_j = cols[:, j * LANES:(j + 1) * LANES].reshape(G, CHUNK, LANES)
        x4 = x.reshape(G, CHUNK // nb, nb, LANES)
        row_j = jnp.broadcast_to(x4[:, :, j:j + 1, :], x4.shape).reshape(G, CHUNK, LANES)
        x = x - col_j * row_j
    t = x[:, :, :CHUNK]
    width = nb
    while width < CHUNK:
        e = jnp.where(((ii // width) ^ (jj // width)) == 1, a, 0.0).astype(BF16)
        tb = t.astype(BF16)
        t = t - _bmm(tb, _bmm(e, tb).astype(BF16))
        width *= 2
    return t


def _gdn_gates_body(ps_ref, alog_ref, dtb_ref, o_ref, *, n_heads):
    p = ps_ref[...]
    lane = lax.broadcasted_iota(jnp.int32, p.shape, 1)
    g_all = -jnp.exp(alog_ref[...]) * _softplus(p + dtb_ref[...])
    o_ref[...] = jnp.where(lane < n_heads, _chunk_scan(g_all, False),
                           jnp.where(lane < 2 * n_heads, _chunk_scan(g_all, True), _sigmoid(p)))


def gdn_gates(ps, alog_l, dtb_l, lay, H):
    T = ps.shape[0]
    S = lay.S
    one = pl.BlockSpec((1, LANES), lambda b: (0, 0))
    return pl.pallas_call(
        functools.partial(_gdn_gates_body, n_heads=H),
        out_shape=jax.ShapeDtypeStruct((T, LANES), F32),
        grid=(lay.B,),
        in_specs=[pl.BlockSpec((S, LANES), lambda b: (b, 0)), one, one],
        out_specs=pl.BlockSpec((S, LANES), lambda b: (b, 0)),
        compiler_params=_params("parallel"),
        name="gdn_gates",
    )(ps, alog_l, dtb_l)


def _gdn_body(q_ref, k_ref, v_ref, z_ref, ps_ref, cq_ref, ck_ref, cv_ref, g_ref, sel_ref, o_ref,
              qs, ks, vs, gc, bb, p_s, r_s, k_s, n_s, dl_s, o_s, *, lay, n_heads, group):
    h = pl.program_id(1)
    S = lay.S
    nc = S // CHUNK
    shape = (S, HEAD_DIM)
    prev_ok, next_ok = _conv_masks(shape, lay)
    lane = lax.broadcasted_iota(jnp.int32, shape, 1)

    q = _silu(_centred_conv(q_ref[...], cq_ref[...], prev_ok, next_ok))
    qs[...] = q * lax.rsqrt(jnp.sum(q * q, axis=-1, keepdims=True) + EPS) * HEAD_DIM ** -0.5
    k = _silu(_centred_conv(k_ref[...], ck_ref[...], prev_ok, next_ok))
    ks[...] = k * lax.rsqrt(jnp.sum(k * k, axis=-1, keepdims=True) + EPS)
    vs[...] = _silu(_centred_conv(v_ref[...], cv_ref[...], prev_ok, next_ok))

    p = ps_ref[...]
    for d in range(2):
        gsel = jnp.sum(jnp.where(lane == d * n_heads + h, p, 0.0), axis=-1, keepdims=True)
        gc[d] = jnp.broadcast_to(gsel, shape)
        bsel = jnp.sum(jnp.where(lane == (2 + d) * n_heads + h, p, 0.0), axis=-1, keepdims=True)
        bb[d] = jnp.broadcast_to(bsel, shape)

    G = group
    ii = lax.broadcasted_iota(jnp.int32, (G, CHUNK, CHUNK), 1)
    jj = lax.broadcasted_iota(jnp.int32, (G, CHUNK, CHUNK), 2)
    bmm, bmm_nt, bmm_tn = _bmm, _bmm_nt, _bmm_tn

    def prep(gi, d):
        rows = G * CHUNK
        sl = pl.ds(pl.multiple_of(gi * rows, rows), rows)
        cs = pl.ds(gi * G, G)
        r3 = lambda x: x.reshape(G, CHUNK, HEAD_DIM)
        q3, k3, v3, gcb, b3 = r3(qs[sl, :]), r3(ks[sl, :]), r3(vs[sl, :]), r3(gc[d, sl, :]), r3(bb[d, sl, :])
        glast = gcb[:, CHUNK - 1:CHUNK, :] if d == 0 else gcb[:, 0:1, :]
        eg = jnp.exp(gcb)
        kb = k3 * b3
        diff = gcb[:, :, :CHUNK] - jnp.swapaxes(gcb, 1, 2)[:, :CHUNK, :]
        incl = (ii >= jj) if d == 0 else (ii <= jj)
        strict = (ii > jj) if d == 0 else (ii < jj)
        decay = jnp.exp(jnp.where(incl, diff, MASK_NEG))
        kbf = k3.astype(BF16)
        a = jnp.where(strict, bmm_nt(kb.astype(BF16), kbf) * decay, 0.0)
        tb = _unit_tri_inverse(a, sel_ref[...], ii, jj, d).astype(BF16)
        ub = bmm(tb, (v3 * b3).astype(BF16)).astype(BF16)
        wb = bmm(tb, (kb * eg).astype(BF16)).astype(BF16)
        attb = (bmm_nt(q3.astype(BF16), kbf) * decay).astype(BF16)
        kgb = (k3 * jnp.exp(glast - gcb)).astype(BF16)
        p_s[d, sl, :] = (q3 * eg - bmm(attb, wb)).astype(BF16).reshape(rows, HEAD_DIM)
        r_s[d, sl, :] = bmm(attb, ub).reshape(rows, HEAD_DIM)
        k_s[d, cs, :, :] = bmm_tn(kgb, wb).astype(BF16)
        n_s[d, cs, :, :] = bmm_tn(kgb, ub)
        dl_s[d, cs, :, :] = jnp.broadcast_to(jnp.exp(glast), (G, 8, HEAD_DIM))

    def prep_loop(gi, carry):
        prep(gi, 0)
        prep(gi, 1)
        return carry

    lax.fori_loop(0, nc // G, prep_loop, 0)

    def step_d(d, n, s):
        c = _chunk_of_step(n, d, lay)
        sl = pl.ds(pl.multiple_of(c * CHUNK, CHUNK), CHUNK)
        sb = s.astype(BF16)
        o_s[d, sl, :] = r_s[d, sl, :] + jnp.dot(p_s[d, sl, :], sb, preferred_element_type=F32)
        dl = dl_s[d, pl.ds(c, 1), :, :][0][0:1, :]
        return (s * dl + n_s[d, pl.ds(c, 1), :, :][0]
                - jnp.dot(k_s[d, pl.ds(c, 1), :, :][0], sb, preferred_element_type=F32))

    def both(n, carry):
        return (step_d(0, n, carry[0]), step_d(1, n, carry[1]))

    zero = jnp.zeros((HEAD_DIM, HEAD_DIM), F32)
    lax.fori_loop(0, nc, both, (zero, zero), unroll=4 if nc % 4 == 0 else 1)
    o_ref[...] = _gated_head_norm(o_s[0] + o_s[1], z_ref[...], g_ref[...])


def gated_deltanet(px, gates, conv_w, norm_g, lay, W, col0):
    T = px.shape[0]
    H = W // HEAD_DIM
    S = lay.S
    nc = S // CHUNK

    def col(o):
        return pl.BlockSpec((S, HEAD_DIM), lambda b, h: (b, col0 + o * H + h))

    def cw(o):
        return pl.BlockSpec((3, HEAD_DIM), lambda b, h: (0, o * H + h))

    one = pl.BlockSpec((1, HEAD_DIM), lambda b, h: (0, 0))
    sh = (S, HEAD_DIM)
    st = (2, nc, HEAD_DIM, HEAD_DIM)
    group = max(g for g in range(1, 37) if nc % g == 0)
    return pl.pallas_call(
        functools.partial(_gdn_body, lay=lay, n_heads=H, group=group),
        out_shape=jax.ShapeDtypeStruct((T, W), BF16),
        grid=(lay.B, H),
        in_specs=[col(0), col(1), col(2), col(3),
                  pl.BlockSpec((S, LANES), lambda b, h: (b, 0)),
                  cw(0), cw(1), cw(2), one,
                  pl.BlockSpec((CHUNK, TRI_BLOCK * LANES), lambda b, h: (0, 0))],
        out_specs=pl.BlockSpec((S, HEAD_DIM), lambda b, h: (b, h)),
        scratch_shapes=[pltpu.VMEM(sh, F32), pltpu.VMEM(sh, F32), pltpu.VMEM(sh, F32),
                        pltpu.VMEM((2,) + sh, F32), pltpu.VMEM((2,) + sh, F32),
                        pltpu.VMEM((2,) + sh, BF16), pltpu.VMEM((2,) + sh, F32),
                        pltpu.VMEM(st, BF16), pltpu.VMEM(st, F32),
                        pltpu.VMEM((2, nc, 8, HEAD_DIM), F32), pltpu.VMEM((2,) + sh, F32)],
        compiler_params=_params("parallel", "arbitrary"),
        name="gated_deltanet",
    )(px, px, px, px, gates, conv_w, conv_w, conv_w, norm_g.reshape(1, HEAD_DIM), _column_spread_matrix())


def _hgrn_body(q_ref, f0_ref, f1_ref, i_ref, z_ref, lb0_ref, lb1_ref, g_ref, o_ref,
               gc, kk, o_s, *, lay, layer):
    S = lay.S
    nc = S // CHUNK
    for d, (f_ref, lb_ref) in enumerate(((f0_ref, lb0_ref), (f1_ref, lb1_ref))):
        lg = lb_ref[...]
        depth = lg.shape[0]
        mx = lg[0:1]
        for r in range(1, depth):
            mx = jnp.maximum(mx, lg[r:r + 1])
        e = [jnp.exp(lg[r:r + 1] - mx) for r in range(depth)]
        tot = e[0]
        for r in range(1, depth):
            tot = tot + e[r]
        lb = jnp.zeros_like(mx)
        for r in range(1, layer + 1):
            lb = lb + e[r] / tot
        f = f_ref[...]
        e_neg = jnp.exp(-jnp.abs(f))
        big = 1.0 / (1.0 + e_neg)
        small = e_neg * big
        pos = f >= 0.0
        logf = jnp.log(jnp.maximum(lb, LB_FLOOR) + (1.0 - lb) * jnp.where(pos, big, small))
        gc[d] = _chunk_cumsum(logf, reverse=(d == 1))
        kk[d] = (1.0 - lb) * jnp.where(pos, small, big)

    SB = 16
    nsb = CHUNK // SB
    jrow = lax.broadcasted_iota(jnp.int32, (CHUNK, HEAD_DIM), 0)
    jsub = lax.broadcasted_iota(jnp.int32, (SB, HEAD_DIM), 0)
    alane = lax.broadcasted_iota(jnp.int32, (SB, CHUNK), 1)

    def intra_t(d, qc, kc, g):
        if d == 0:
            refs = [g[0:1, :]] + [g[SB * I - 1:SB * I, :] for I in range(1, nsb)]
            far = range(1, nsb)
        else:
            refs = [g[SB * (I + 1):SB * (I + 1) + 1, :] for I in range(nsb - 1)] + [g[CHUNK - 1:CHUNK, :]]
            far = range(nsb - 1)
        rvec = jnp.concatenate([jnp.broadcast_to(r, (SB, HEAD_DIM)) for r in refs], axis=0)
        qt = qc * jnp.exp(g - rvec)
        off = None
        for I in far:
            seen = (jrow < SB * I) if d == 0 else (jrow >= SB * (I + 1))
            kt = (kc * jnp.exp(jnp.where(seen, refs[I] - g, MASK_NEG))).astype(BF16)
            qi = jnp.where((jrow // SB) == I, qt, 0.0).astype(BF16)
            part = lax.dot_general(kt, qi, _NT, preferred_element_type=F32)
            off = part if off is None else off + part
        blocks = []
        for I in range(nsb):
            rs = slice(SB * I, SB * (I + 1))
            g_i, k_i = g[rs, :], kc[rs, :]
            acc = jnp.zeros((SB, CHUNK), F32)
            for r in range(SB):
                i = SB * I + r
                keep = (jsub <= r) if d == 0 else (jsub >= r)
                rel = jnp.exp(jnp.where(keep, g[i:i + 1, :] - g_i, MASK_NEG))
                colv = jnp.sum(rel * k_i * qc[i:i + 1, :], axis=-1, keepdims=True)
                acc = jnp.where(alane == i, colv, acc)
            blocks.append(acc)
        return jnp.concatenate(blocks, axis=0) + off

    def step_d(d, n, st):
        c = _chunk_of_step(n, d, lay)
        sl = pl.ds(pl.multiple_of(c * CHUNK, CHUNK), CHUNK)
        qc, kc, vc, g = q_ref[sl, :], kk[d, sl, :], i_ref[sl, :], gc[d, sl, :]
        glast = g[CHUNK - 1:CHUNK, :] if d == 0 else g[0:1, :]
        a_t = intra_t(d, qc, kc, g)
        vb = vc.astype(BF16)
        o = (lax.dot_general((qc * jnp.exp(g)).astype(BF16), st.astype(BF16), _NT, preferred_element_type=F32)
             + lax.dot_general(a_t.astype(BF16), vb, _TN, preferred_element_type=F32))
        o_s[d, sl, :] = o
        kg = (kc * jnp.exp(glast - g)).astype(BF16)
        return st * jnp.exp(glast) + lax.dot_general(vb, kg, _TN, preferred_element_type=F32)

    def both(n, carry):
        return (step_d(0, n, carry[0]), step_d(1, n, carry[1]))

    zero = jnp.zeros((HEAD_DIM, HEAD_DIM), F32)
    lax.fori_loop(0, nc, both, (zero, zero), unroll=4 if nc % 4 == 0 else 1)
    o_ref[...] = _gated_head_norm(o_s[0] + o_s[1], z_ref[...], g_ref[...])


def hgrn2(px, lb_logits, norm_g, lay, W, col0, layer):
    T = px.shape[0]
    H = W // HEAD_DIM
    S = lay.S
    depth = lb_logits.shape[0]

    def col(o):
        return pl.BlockSpec((S, HEAD_DIM), lambda b, h: (b, col0 + o * H + h))

    def lbs(d):
        return pl.BlockSpec((depth, HEAD_DIM), lambda b, h: (0, d * H + h))

    sh = (S, HEAD_DIM)
    return pl.pallas_call(
        functools.partial(_hgrn_body, lay=lay, layer=layer),
        out_shape=jax.ShapeDtypeStruct((T, W), BF16),
        grid=(lay.B, H),
        in_specs=[col(0), col(1), col(2), col(3), col(4), lbs(0), lbs(1),
                  pl.BlockSpec((1, HEAD_DIM), lambda b, h: (0, 0))],
        out_specs=pl.BlockSpec((S, HEAD_DIM), lambda b, h: (b, h)),
        scratch_shapes=[pltpu.VMEM((2,) + sh, F32), pltpu.VMEM((2,) + sh, F32), pltpu.VMEM((2,) + sh, F32)],
        compiler_params=_params("parallel", "arbitrary"),
        name="hgrn2",
    )(px, px, px, px, px, lb_logits, lb_logits, norm_g.reshape(1, HEAD_DIM))


def _hy_filter_body(z_ref, w1_ref, b1_ref, w2_ref, b2_ref, w3_ref, fr_ref, dl_ref, o_ref, *, L):
    i = pl.program_id(0)
    z = z_ref[...]
    tr = z.shape[0]
    W = o_ref.shape[1]
    h = jnp.sin(fr_ref[0:1, :] * (jnp.dot(z, w1_ref[...], precision=HIGHEST, preferred_element_type=F32) + b1_ref[...]))
    h = jnp.sin(fr_ref[1:2, :] * (jnp.dot(h, w2_ref[...], precision=HIGHEST, preferred_element_type=F32) + b2_ref[...]))
    h = jnp.dot(h, w3_ref[...], precision=HIGHEST, preferred_element_type=F32)
    h = h * jnp.exp(-z[:, 0:1] * dl_ref[...])
    row = lax.broadcasted_iota(jnp.int32, (tr, W), 0) + i * tr
    o_ref[...] = jnp.where(row < L, h[:, :W], jnp.where(row == L, 0.0, h[:, W:]))


def hyena_filter(z2, w1p, b1, w2, b2, w3, freq, deltas2, L):
    n = z2.shape[0]
    W = w3.shape[1] // 2
    O = w2.shape[0]
    tr = _tile(n, 512)
    full = lambda a: pl.BlockSpec(a.shape, lambda i: (0,) * a.ndim)
    args = (w1p, b1.reshape(1, O), w2, b2.reshape(1, O), w3, freq, deltas2)
    return pl.pallas_call(
        functools.partial(_hy_filter_body, L=L),
        out_shape=jax.ShapeDtypeStruct((n, W), F32),
        grid=(n // tr,),
        in_specs=[pl.BlockSpec((tr, LANES), lambda i: (i, 0))] + [full(a) for a in args],
        out_specs=pl.BlockSpec((tr, W), lambda i: (i, 0)),
        compiler_params=_params("parallel"),
        name="hyena_filter",
    )(z2, *args)


def _hy_prep_body(v_ref, x0_ref, x1_ref, cv_ref, c0_ref, c1_ref, bv_ref, b0_ref, b1_ref,
                  vgx_ref, vge_ref, x0u_ref, *, lay):
    shape = v_ref.shape
    prev_ok, next_ok = _conv_masks(shape, lay)
    v = _centred_conv(v_ref[...], cv_ref[...], prev_ok, next_ok) + bv_ref[...]
    x0 = _centred_conv(x0_ref[...], c0_ref[...], prev_ok, next_ok) + b0_ref[...]
    x1 = _centred_conv(x1_ref[...], c1_ref[...], prev_ok, next_ok) + b1_ref[...]
    vg = v * x1
    vge_ref[...] = vg[:lay.Lc, :]
    vgx_ref[...] = vg[lay.Lc:, :]
    x0u_ref[...] = x0


def hyena_prep(px, conv_w, conv_b, lay, W, col0):
    T = px.shape[0]
    H = W // HEAD_DIM
    S, L, Lc, B = lay.S, lay.L, lay.Lc, lay.B

    def col(o):
        return pl.BlockSpec((S, HEAD_DIM), lambda b, j: (b, col0 + o * H + j))

    def cw(o, rows):
        return pl.BlockSpec((rows, HEAD_DIM), lambda b, j: (0, o * H + j))

    return pl.pallas_call(
        functools.partial(_hy_prep_body, lay=lay),
        out_shape=(jax.ShapeDtypeStruct((B * L, W), F32), jax.ShapeDtypeStruct((B * Lc, W), F32),
                   jax.ShapeDtypeStruct((T, W), F32)),
        grid=(B, H),
        in_specs=[col(0), col(1), col(2), cw(0, 3), cw(1, 3), cw(2, 3), cw(0, 1), cw(1, 1), cw(2, 1)],
        out_specs=(pl.BlockSpec((L, HEAD_DIM), lambda b, j: (b, j)),
                   pl.BlockSpec((Lc, HEAD_DIM), lambda b, j: (b, j)),
                   pl.BlockSpec((S, HEAD_DIM), lambda b, j: (b, j))),
        compiler_params=_params("parallel", "parallel"),
        name="hyena_prep",
    )(px, px, px, conv_w, conv_w, conv_w, conv_b, conv_b, conv_b)


def _dft_tables(L):
    n = 2 * L
    FT = min(256, L)
    sub = 64
    idx = jnp.arange(n, dtype=jnp.int32)
    r = idx % (2 * FT)
    is_im = r >= FT
    f = (idx // (2 * FT)) * FT + jnp.where(is_im, r - FT, r)
    nyq = is_im & (f == 0)
    f = jnp.where(nyq, L, f)
    step = jnp.arange(sub, dtype=jnp.int32)
    ang_hi = (2.0 * math.pi / n) * ((f[:, None] * (step * sub)[None, :]) % n).astype(F32)
    ang_lo = (2.0 * math.pi / n) * ((f[:, None] * step[None, :]) % n).astype(F32)
    ch, sh, cl, sl = jnp.cos(ang_hi), jnp.sin(ang_hi), jnp.cos(ang_lo), jnp.sin(ang_lo)
    wgt = jnp.where((f == 0) | (f == L), 1.0 / n, 2.0 / n)
    minus_sin = is_im & ~nyq

    def table(hi_steps, transpose):
        if transpose:
            c_h, s_h, c_l, s_l = (t.T for t in (ch, sh, cl, sl))
            cos = c_h[:hi_steps, None, :] * c_l[None, :, :] - s_h[:hi_steps, None, :] * s_l[None, :, :]
            sin = s_h[:hi_steps, None, :] * c_l[None, :, :] + c_h[:hi_steps, None, :] * s_l[None, :, :]
            out = jnp.where(minus_sin[None, None, :], -sin, cos) * wgt[None, None, :]
            return out.reshape(hi_steps * sub, n)
        cos = ch[:, :hi_steps, None] * cl[:, None, :] - sh[:, :hi_steps, None] * sl[:, None, :]
        sin = sh[:, :hi_steps, None] * cl[:, None, :] + ch[:, :hi_steps, None] * sl[:, None, :]
        return jnp.where(minus_sin[:, None, None], -sin, cos).reshape(n, hi_steps * sub)

    return table(n // sub, False), table(L // sub, False), table(L // sub, True), FT


def _hy_fwd_body(a_ref, u_ref, kf_ref, y_ref, *, FT):
    t = pl.program_id(2)
    x = jnp.dot(a_ref[...], u_ref[...].astype(BF16), preferred_element_type=F32)
    kf = kf_ref[...]
    xr, xi, kr, ki = x[:FT], x[FT:], kf[:FT], kf[FT:]
    row = lax.broadcasted_iota(jnp.int32, xr.shape, 0)
    packed = (row + t) == 0
    yr = jnp.where(packed, xr * kr, xr * kr - xi * ki)
    yi = jnp.where(packed, xi * ki, xr * ki + xi * kr)
    y_ref[...] = jnp.concatenate([yr, yi], axis=0).astype(BF16)


def _hy_spec_body(a_ref, u_ref, o_ref):
    o_ref[...] = jnp.dot(a_ref[...], u_ref[...].astype(BF16), preferred_element_type=F32)


def hyena_spectrum(f_b, kern, FT):
    n, W = kern.shape
    tn = _tile(W, 256)
    return pl.pallas_call(
        _hy_spec_body,
        out_shape=jax.ShapeDtypeStruct((n, W), F32),
        grid=(W // tn, n // (2 * FT)),
        in_specs=[pl.BlockSpec((2 * FT, n), lambda j, t: (t, 0)),
                  pl.BlockSpec((n, tn), lambda j, t: (0, j))],
        out_specs=pl.BlockSpec((2 * FT, tn), lambda j, t: (t, j)),
        compiler_params=_params("parallel", "arbitrary"),
        name="hyena_spectrum",
    )(f_b, kern)


def hyena_fwd(f_b, vg, kf, B, Lq, FT):
    W = vg.shape[1]
    n = 2 * Lq
    nt = n // (2 * FT)
    tn = _tile(W, 1024)
    return pl.pallas_call(
        functools.partial(_hy_fwd_body, FT=FT),
        out_shape=jax.ShapeDtypeStruct((B * n, W), BF16),
        grid=(B, W // tn, nt),
        in_specs=[pl.BlockSpec((2 * FT, Lq), lambda b, j, t: (t, 0)),
                  pl.BlockSpec((Lq, tn), lambda b, j, t: (b, j)),
                  pl.BlockSpec((2 * FT, tn), lambda b, j, t: (t, j))],
        out_specs=pl.BlockSpec((2 * FT, tn), lambda b, j, t: (b * nt + t, j)),
        compiler_params=_params("parallel", "parallel", "arbitrary"),
        name="hyena_fwd",
    )(f_b, vg, kf)


def _hy_inv_body(g_ref, y_ref, vg_ref, x0_ref, bias_ref, o_ref):
    y = jnp.dot(g_ref[...], y_ref[...], preferred_element_type=F32)
    o_ref[...] = (x0_ref[...] * (y + vg_ref[...] * bias_ref[...])).astype(BF16)


def hyena_inv(g_b, y_b, vg, x0u, bias, lay, Lq, row0):
    W = vg.shape[1]
    n = 2 * Lq
    B = lay.B
    tt = lay.gr
    ntt = Lq // tt
    tn = _tile(W, 1024)
    g0 = row0 // tt
    return pl.pallas_call(
        _hy_inv_body,
        out_shape=jax.ShapeDtypeStruct((B * Lq, W), BF16),
        grid=(B, W // tn, ntt),
        in_specs=[pl.BlockSpec((tt, n), lambda b, j, i: (i, 0)),
                  pl.BlockSpec((n, tn), lambda b, j, i: (b, j)),
                  pl.BlockSpec((tt, tn), lambda b, j, i: (b * ntt + i, j)),
                  pl.BlockSpec((tt, tn), lambda b, j, i: (b * lay.NG + g0 + i, j)),
                  pl.BlockSpec((1, tn), lambda b, j, i: (0, j))],
        out_specs=pl.BlockSpec((tt, tn), lambda b, j, i: (b * ntt + i, j)),
        compiler_params=_params("parallel", "parallel", "arbitrary"),
        name="hyena_inv",
    )(g_b, y_b, vg, x0u, bias.reshape(1, W))


def _hyena_features(L):
    bands = (HY_EMB - 1) // 2
    t = jnp.linspace(0.0, 1.0, L, dtype=F32)[:, None]
    wpos = 2.0 * math.pi * jnp.arange(L, dtype=F32)[:, None] / L
    fb = jnp.linspace(1e-4, bands - 1, bands, dtype=F32)[None]
    z = jnp.concatenate([t, jnp.cos(fb * wpos), -jnp.sin(fb * wpos)], axis=-1)
    z2 = jnp.concatenate([z, z[::-1]], axis=0)
    return jnp.pad(z2, ((0, 0), (0, LANES - HY_EMB)))


def hyena_stream(vg, x0u, bias, wts, lay, Lq, row0):
    w1p, b1, w2, b2, w3, freq, deltas2 = wts
    fwd_full, fwd_half, inv, FT = _dft_tables(Lq)
    kern = hyena_filter(_hyena_features(Lq), w1p, b1, w2, b2, w3, freq, deltas2, Lq)
    kf = hyena_spectrum(fwd_full.astype(BF16), kern, FT)
    y_b = hyena_fwd(fwd_half.astype(BF16), vg, kf, lay.B, Lq, FT)
    return hyena_inv(inv.astype(BF16), y_b, vg, x0u, bias, lay, Lq, row0)


def _rope_tables(lay):
    n_freq = DIFF_DIM // 4
    inv = ROPE_THETA ** (-jnp.arange(n_freq, dtype=F32) / n_freq)
    rows = lay.L // GRID_W
    r = jnp.repeat(jnp.arange(rows, dtype=F32), GRID_W)
    col = jnp.tile(jnp.arange(GRID_W, dtype=F32), rows)
    ang = jnp.concatenate([r[:, None] * inv, col[:, None] * inv], axis=-1)
    cos, sin = jnp.cos(ang), jnp.sin(ang)
    cos = jnp.concatenate([jnp.ones((lay.Lc, DIFF_DIM // 2), F32), cos], axis=0)
    sin = jnp.concatenate([jnp.zeros((lay.Lc, DIFF_DIM // 2), F32), sin], axis=0)
    cos_t = jnp.tile(cos, (1, 4))
    sin_t = jnp.tile(jnp.concatenate([-sin, sin], axis=-1), (1, 2))
    return cos_t, sin_t


def _main_cols(W, H):
    small0 = 7 * W
    small1 = small0 + 4 * H
    return small0, small1


def mixers(px1, px2, ps, lay, W, layer, need_ctx, p):
    H = W // HEAD_DIM
    lam_init = 0.8 - 0.6 * math.exp(-0.3 * layer)
    q_rot, k_rot = attn_prep(px1, p["cos_t"], p["sin_t"], p["qk_g2"], lay, W)
    att = attention(q_rot, k_rot, px1, p["attn_lambda"], p["attn_subln_g"], lay, W, lam_init)
    gates = gdn_gates(ps, p["alog"], p["dtb"], lay, H)
    gdn = gated_deltanet(px1, gates, p["gdn_conv_w"], p["gdn_norm_g"], lay, W, 3 * H)
    hg = hgrn2(px2, p["hg_lb_logits"], p["hg_norm_g"], lay, W, 0, layer)
    vgx, vge, x0u = hyena_prep(px2, p["hy_conv_w"], p["hy_conv_b"], lay, W, 5 * H)
    hy_x = hyena_stream(vgx, x0u, p["hy_bias"], p["hy_wts"], lay, lay.L, lay.Lc)
    if need_ctx:
        hy_e = hyena_stream(vge, x0u, p["hy_bias"], p["hy_wts"], lay, lay.Lc, 0)
    else:
        hy_e = jnp.zeros((lay.B, lay.Lc, W), BF16)
    hy = jnp.concatenate([hy_e.reshape(lay.B, lay.Lc, W), hy_x.reshape(lay.B, lay.L, W)], axis=1)
    return att, gdn, hg, hy.reshape(lay.T, W)


def _layer_params(l, lay, W, attn_qk_g, attn_lambda, attn_subln_g, gdn_conv_w, gdn_a_log, gdn_dt_bias, gdn_norm_g,
                  hg_lb_logits, hg_norm_g, hy_conv_w, hy_conv_b, hy_w1, hy_b1, hy_w2, hy_b2, hy_w3, hy_freq,
                  hy_bias, cos_t, sin_t):
    H = W // HEAD_DIM
    pad16 = lambda a: jnp.pad(a.astype(F32).reshape(1, 2 * H), ((0, 0), (0, LANES - 2 * H)))
    deltas = jnp.abs(jnp.linspace(math.log(HY_TARGET) / HY_SLOW_DECAY, math.log(HY_TARGET) / HY_FAST_DECAY, W, dtype=F32))
    return dict(
        cos_t=cos_t, sin_t=sin_t,
        qk_g2=jnp.tile(attn_qk_g[l].astype(F32), (1, 2)),
        attn_lambda=attn_lambda[l].astype(F32), attn_subln_g=attn_subln_g[l].astype(F32),
        gdn_conv_w=gdn_conv_w[l], alog=pad16(gdn_a_log[l]), dtb=pad16(gdn_dt_bias[l]), gdn_norm_g=gdn_norm_g[l],
        hg_lb_logits=hg_lb_logits.astype(F32).reshape(hg_lb_logits.shape[0], 2 * W), hg_norm_g=hg_norm_g[l],
        hy_conv_w=hy_conv_w[l], hy_conv_b=hy_conv_b[l].reshape(1, 3 * W), hy_bias=hy_bias[l].astype(F32),
        hy_wts=(jnp.pad(hy_w1[l].astype(F32), ((0, LANES - HY_EMB), (0, 0))), hy_b1[l].astype(F32),
                hy_w2[l].astype(F32), hy_b2[l].astype(F32), hy_w3[l].astype(F32), hy_freq[l].astype(F32),
                jnp.tile(deltas, 2).reshape(1, 2 * W)),
    )


def kernel(x, c, ctx, c_ctx, norm_g, w_mod, b_mod, ffn_w_in, ffn_w_out, w_in, attn_qk_g, attn_lambda, attn_subln_g, gdn_conv_w, gdn_a_log, gdn_dt_bias, gdn_norm_g, hg_lb_logits, hg_norm_g, hy_conv_w, hy_conv_b, hy_w1, hy_b1, hy_w2, hy_b2, hy_w3, hy_freq, hy_bias, w_gate, w_up, w_out):
    B, L, D = x.shape
    Lc = ctx.shape[1]
    depth = w_mod.shape[0]
    W = D // N_BRANCH
    H = W // HEAD_DIM
    lay = Layout(B, L, Lc)
    h = jnp.concatenate([ctx, x], axis=1).reshape(lay.T, D)
    G = -(-(B + 1) // 8) * 8
    c_all = jnp.concatenate([c, c_ctx[None], jnp.zeros((G - B - 1, D), F32)], axis=0)
    cos_t, sin_t = _rope_tables(lay)
    small0, small1 = _main_cols(W, H)
    w_gate_b, w_up_b = w_gate.astype(BF16), w_up.astype(BF16)
    for l in range(depth):
        last = l == depth - 1
        p = _layer_params(l, lay, W, attn_qk_g, attn_lambda, attn_subln_g, gdn_conv_w, gdn_a_log, gdn_dt_bias,
                          gdn_norm_g, hg_lb_logits, hg_norm_g, hy_conv_w, hy_conv_b, hy_w1, hy_b1, hy_w2, hy_b2,
                          hy_w3, hy_freq, hy_bias, cos_t, sin_t)
        mod = mod_table(c_all, w_mod, b_mod, l).reshape(G, MOD_CHUNKS, D)
        act = mm_swiglu(norm_mod(h, norm_g[l, 0], mod, lay, 0), ffn_w_in, (l, 0))
        h = mm_resid(act, ffn_w_out, (l, 0), h, mod, lay, 2, 0.5)
        xn = norm_mod(h, norm_g[l, 1], mod, lay, 1)
        cols_t = lambda lo, hi: w_in[l, :, lo:hi].T
        px1 = mm_plain(xn, cols_t(0, small0))
        px2 = mm_plain(xn, cols_t(small1, w_in.shape[-1]))
        ps = mm_plain(xn, jnp.pad(cols_t(small0, small1), ((0, LANES - (small1 - small0)), (0, 0))), tn_pref=LANES)
        branches = mixers(px1, px2, ps, lay, W, l, not last, p)
        if last:
            latent = lambda a: a.reshape(B, lay.S, a.shape[-1])[:, Lc:, :].reshape(B * L, a.shape[-1])
            xn, h, branches = latent(xn), latent(h), tuple(latent(a) for a in branches)
            lay = Layout(B, L, 0, gr=lay.gr)
        acc = merge_branches(xn, branches, w_gate_b, w_up_b, l)
        h = mm_resid(acc, w_out, (l,), h, mod, lay, 5, 1.0)
        act = mm_swiglu(norm_mod(h, norm_g[l, 2], mod, lay, 2), ffn_w_in, (l, 1))
        h = mm_resid(act, ffn_w_out, (l, 1), h, mod, lay, 8, 0.5)
    return h.reshape(B, L, D)
```
